```python
import math
import jax
import jax.numpy as jnp
from jax import lax
import numpy as np

D_MODEL = 1024
BATCH = 32
SEQ = 256
DEPTH = 2
DEC_BATCH = 2
DEC_SEQ = 4096
PAST_LEN = 512

GRID_W = 64
HEAD_DIM = 64
D_FF = 2816
N_MOD = 9
W_POOL = D_MODEL // 2
W_CONV = D_MODEL - W_POOL
POOL_WINDOWS = (2, 4, 8, 16)
N_POOL_GROUPS = len(POOL_WINDOWS)
POOL_G = W_POOL // N_POOL_GROUPS
CONV_K = 3
W_NAT = D_MODEL // 2
W_DIFF = D_MODEL - W_NAT
H_NAT = W_NAT // HEAD_DIM
H_DIFF = W_DIFF // (2 * HEAD_DIM)
NAT_WIN_R = 8
NAT_WIN_C = 16
ROPE_THETA = 10000.0
Q_BLOCK = 128
N_EVEN = (DEPTH + 1) // 2
N_ODD = DEPTH // 2
EVEN_IN = W_POOL + 3 * W_CONV
ODD_IN = 3 * W_NAT + 3 * W_DIFF
ATTN_SCALE = HEAD_DIM ** -0.5
EPS = 1e-6

kernel_name = 'hybrid_pool_conv_nat_diff_prefix_dit_step'


def rmsnorm(x, g):
    xf = x.astype(jnp.float32)
    y = xf * lax.rsqrt(jnp.mean(xf * xf, axis=-1, keepdims=True) + EPS)
    return (y * g.astype(jnp.float32)).astype(x.dtype)


def adaln_mods(cond, mod_w, mod_b):
    m = jax.nn.silu(cond) @ mod_w + mod_b
    return jnp.split(m[:, None, :], N_MOD, axis=-1)


def modulated_norm(x, shift, scale, g):
    return rmsnorm(x, g) * (1.0 + scale) + shift


def swiglu(h, w_in, w_out):
    a, b = jnp.split(h @ w_in, 2, axis=-1)
    return (jax.nn.silu(a) * b) @ w_out


def ffn_half_step(x, mods, g, w_in, w_out):
    shift, scale, gate = mods
    return x + 0.5 * gate * swiglu(modulated_norm(x, shift, scale, g), w_in, w_out)


def pool_mixer(xa, pool_w, pool_scale):
    b, n, _ = xa.shape
    xg = xa.reshape(b, n, N_POOL_GROUPS, POOL_G)
    t = np.arange(n)[None, :]
    half = np.array(POOL_WINDOWS)[:, None] // 2
    lo = np.clip(t - half, 0, n - 1).T
    hi = np.clip(t + half - 1, 0, n - 1).T
    cnt = (hi - lo + 1).astype(np.float32)
    gidx = np.arange(N_POOL_GROUPS)[None, :]
    cs = jnp.pad(jnp.cumsum(xg.astype(jnp.float32), axis=1), ((0, 0), (1, 0), (0, 0), (0, 0)))
    win_sum = cs[:, hi + 1, gidx] - cs[:, lo, gidx]
    mean = win_sum / jnp.asarray(cnt)[None, :, :, None]
    d = (mean - xg.astype(jnp.float32)).astype(xa.dtype)
    y = jnp.einsum('bngc,gcd->bngd', d, pool_w)
    return y.reshape(b, n, W_POOL) * pool_scale


def short_conv_mixer(u, conv_w):
    h, gb, gc = jnp.split(u, 3, axis=-1)
    z = gc * h
    n = z.shape[1]
    zp = jnp.pad(z, ((0, 0), (CONV_K // 2, CONV_K // 2), (0, 0)))
    y = sum(zp[:, j:j + n] * conv_w[j] for j in range(CONV_K))
    return gb * y


def even_mixer(h, w_in, pool_w, pool_scale, conv_w, w_out):
    u = h @ w_in
    ya = pool_mixer(u[..., :W_POOL], pool_w, pool_scale)
    yb = short_conv_mixer(u[..., W_POOL:], conv_w)
    return jnp.concatenate([ya, yb], axis=-1) @ w_out


def axial_rope_tables(n):
    t = np.arange(n)
    row = (t // GRID_W).astype(np.float64)
    col = (t % GRID_W).astype(np.float64)
    quarter = HEAD_DIM // 4
    inv = 1.0 / (ROPE_THETA ** (np.arange(quarter) / quarter))
    ang = np.stack([row[:, None] * inv[None], col[:, None] * inv[None]], axis=1)
    return jnp.asarray(np.cos(ang).astype(np.float32)), jnp.asarray(np.sin(ang).astype(np.float32))


def apply_axial_rope(x, cos, sin):
    shp = x.shape
    xr = x.reshape(shp[:-1] + (2, 2, HEAD_DIM // 4)).astype(jnp.float32)
    c = cos[None, :, None, None]
    s = sin[None, :, None, None]
    x1, x2 = xr[..., 0, :], xr[..., 1, :]
    out = jnp.stack([x1 * c - x2 * s, x2 * c + x1 * s], axis=-2)
    return out.reshape(shp).astype(x.dtype)


def _split_blocks(q):
    b, n = q.shape[:2]
    return jnp.moveaxis(q.reshape((b, n // Q_BLOCK, Q_BLOCK) + q.shape[2:]), 1, 0)


def _merge_blocks(o):
    nb, b = o.shape[:2]
    return jnp.moveaxis(o, 0, 1).reshape((b, nb * Q_BLOCK) + o.shape[3:])


def softmax_attn(q, k, v):
    def one(qb):
        s = jnp.einsum('bqhd,bkhd->bhqk', qb, k).astype(jnp.float32)
        p = jax.nn.softmax(s, axis=-1).astype(v.dtype)
        return jnp.einsum('bhqk,bkhd->bqhd', p, v)
    return _merge_blocks(lax.map(one, _split_blocks(q)))


def diff_attn(q, k, v, lam):
    def one(qb):
        s = jnp.einsum('bqhjd,bkhjd->bhjqk', qb, k).astype(jnp.float32)
        p = jax.nn.softmax(s, axis=-1)
        a = (p[:, :, 0] - lam * p[:, :, 1]).astype(v.dtype)
        return jnp.einsum('bhqk,bkhe->bqhe', a, v)
    return _merge_blocks(lax.map(one, _split_blocks(q)))


def diff_lambda_value(lam_p, lam_init):
    lp = lam_p.astype(jnp.float32)
    return jnp.exp(jnp.sum(lp[0] * lp[1])) - jnp.exp(jnp.sum(lp[2] * lp[3])) + lam_init


def diff_heads_out(o, lam_init, g):
    b, n = o.shape[:2]
    return (rmsnorm(o, g) * (1.0 - lam_init)).reshape(b, n, W_DIFF)


def nat_latent_attn(q, k, v, kc, vc, rpb):
    b, n, h, d = q.shape
    rows = n // GRID_W
    wr = min(NAT_WIN_R, rows)
    r = np.arange(rows)
    r0 = np.clip(r - wr // 2, 0, rows - wr)
    key_rows = r0[:, None] + np.arange(wr)[None]
    col = np.arange(GRID_W)
    c0 = np.clip(col - NAT_WIN_C // 2, 0, GRID_W - NAT_WIN_C)
    key_cols = c0[:, None] + np.arange(NAT_WIN_C)[None]
    dr = key_rows - r[:, None] + (NAT_WIN_R - 1)
    dc = key_cols - col[:, None] + (NAT_WIN_C - 1)
    col_bias = rpb[:, :, dc]
    kg = k.reshape(b, rows, GRID_W, h, d)
    vg = v.reshape(b, rows, GRID_W, h, d)
    qg = jnp.moveaxis(q.reshape(b, rows, GRID_W, h, d), 1, 0)
    n_loc = wr * NAT_WIN_C

    def one(args):
        q_row, kr, drr = args
        k_win = jnp.take(jnp.take(kg, kr, axis=1), key_cols, axis=2)
        v_win = jnp.take(jnp.take(vg, kr, axis=1), key_cols, axis=2)
        bias = jnp.moveaxis(jnp.take(col_bias, drr, axis=1), 1, 2)
        s_loc = jnp.einsum('bqhd,brqkhd->bhqrk', q_row, k_win).astype(jnp.float32) + bias[None].astype(jnp.float32)
        s_ctx = jnp.einsum('bqhd,bmhd->bhqm', q_row, kc).astype(jnp.float32)
        p = jax.nn.softmax(jnp.concatenate([s_loc.reshape(b, h, GRID_W, n_loc), s_ctx], axis=-1), axis=-1)
        p_loc = p[..., :n_loc].reshape(b, h, GRID_W, wr, NAT_WIN_C).astype(v.dtype)
        p_ctx = p[..., n_loc:].astype(v.dtype)
        return (jnp.einsum('bhqrk,brqkhd->bqhd', p_loc, v_win)
                + jnp.einsum('bhqm,bmhd->bqhd', p_ctx, vc))

    out = lax.map(one, (qg, jnp.asarray(key_rows, jnp.int32), jnp.asarray(dr, jnp.int32)))
    return jnp.moveaxis(out, 0, 1).reshape(b, n, h, d)


def odd_project(h, w_in):
    b, n, _ = h.shape
    cuts = np.cumsum([W_NAT, W_NAT, W_NAT, W_DIFF, W_DIFF]).tolist()
    nq, nk, nv, dq, dk, dv = jnp.split(h @ w_in, cuts, axis=-1)
    return (nq.reshape(b, n, H_NAT, HEAD_DIM), nk.reshape(b, n, H_NAT, HEAD_DIM),
            nv.reshape(b, n, H_NAT, HEAD_DIM), dq.reshape(b, n, H_DIFF, 2, HEAD_DIM),
            dk.reshape(b, n, H_DIFF, 2, HEAD_DIM), dv.reshape(b, n, H_DIFF, 2 * HEAD_DIM))


def odd_mixer_context(h, w_in, lam_p, dnorm, w_out, layer_idx):
    b, n, _ = h.shape
    nq, nk, nv, dq, dk, dv = odd_project(h, w_in)
    lam_init = 0.8 - 0.6 * math.exp(-0.3 * layer_idx)
    lam = diff_lambda_value(lam_p, lam_init)
    o_nat = softmax_attn(nq * ATTN_SCALE, nk, nv).reshape(b, n, W_NAT)
    o_diff = diff_heads_out(diff_attn(dq * ATTN_SCALE, dk, dv, lam), lam_init, dnorm)
    y = jnp.concatenate([o_nat, o_diff], axis=-1) @ w_out
    return y, nk, nv, dk, dv


def odd_mixer_latent(h, kc, vc, dkc, dvc, w_in, rpb, lam_p, dnorm, w_out, layer_idx, cos, sin):
    b, n, _ = h.shape
    nq, nk, nv, dq, dk, dv = odd_project(h, w_in)
    dq = apply_axial_rope(dq, cos, sin)
    dk = apply_axial_rope(dk, cos, sin)
    lam_init = 0.8 - 0.6 * math.exp(-0.3 * layer_idx)
    lam = diff_lambda_value(lam_p, lam_init)
    o_nat = nat_latent_attn(nq * ATTN_SCALE, nk, nv, kc, vc, rpb).reshape(b, n, W_NAT)
    k_all = jnp.concatenate([dk, dkc], axis=1)
    v_all = jnp.concatenate([dv, dvc], axis=1)
    o_diff = diff_heads_out(diff_attn(dq * ATTN_SCALE, k_all, v_all, lam), lam_init, dnorm)
    return jnp.concatenate([o_nat, o_diff], axis=-1) @ w_out


def setup_inputs(seed: int = 0) -> dict:
    key = jax.random.key(seed)
    ks = jax.random.split(key, 32)
    f32 = jnp.float32
    D = D_MODEL

    def nrm(k, shape, scale):
        return jax.random.normal(k, shape, f32) * scale

    def gain(k, shape):
        return 1.0 + 0.02 * jax.random.normal(k, shape, f32)

    return {
        'x_prompt': nrm(ks[0], (BATCH, SEQ, D), 1.0),
        'x_sample': nrm(ks[1], (DEC_BATCH, DEC_SEQ, D), 1.0),
        'cache_nat_k': nrm(ks[2], (DEC_BATCH, N_ODD, PAST_LEN, H_NAT, HEAD_DIM), 1.0),
        'cache_nat_v': nrm(ks[3], (DEC_BATCH, N_ODD, PAST_LEN, H_NAT, HEAD_DIM), 1.0),
        'cache_diff_k': nrm(ks[4], (DEC_BATCH, N_ODD, PAST_LEN, H_DIFF, 2, HEAD_DIM), 1.0),
        'cache_diff_v': nrm(ks[5], (DEC_BATCH, N_ODD, PAST_LEN, H_DIFF, 2 * HEAD_DIM), 1.0),
        'c': nrm(ks[6], (DEC_BATCH, D), 1.0),
        'c_ctx': nrm(ks[7], (D,), 1.0),
        'mod_w': nrm(ks[8], (DEPTH, D, N_MOD * D), 0.5 * D ** -0.5),
        'mod_b': nrm(ks[9], (DEPTH, N_MOD * D), 0.01),
        'norm_ffn1': gain(ks[10], (DEPTH, D)),
        'ffn1_w_in': nrm(ks[11], (DEPTH, D, 2 * D_FF), D ** -0.5),
        'ffn1_w_out': nrm(ks[12], (DEPTH, D_FF, D), D_FF ** -0.5),
        'norm_mix': gain(ks[13], (DEPTH, D)),
        'even_w_in': nrm(ks[14], (N_EVEN, D, EVEN_IN), D ** -0.5),
        'pool_w': nrm(ks[15], (N_EVEN, N_POOL_GROUPS, POOL_G, POOL_G), POOL_G ** -0.5),
        'pool_scale': gain(ks[16], (N_EVEN, W_POOL)),
        'conv_w': nrm(ks[17], (N_EVEN, CONV_K, W_CONV), CONV_K ** -0.5),
        'odd_w_in': nrm(ks[18], (N_ODD, D, ODD_IN), D ** -0.5),
        'nat_rpb': nrm(ks[19], (N_ODD, H_NAT, 2 * NAT_WIN_R - 1, 2 * NAT_WIN_C - 1), 0.1),
        'diff_lambda': nrm(ks[20], (N_ODD, 4, HEAD_DIM), 0.1),
        'diff_norm': gain(ks[21], (N_ODD, 2 * HEAD_DIM)),
        'mix_w_out': nrm(ks[22], (DEPTH, D, D), D ** -0.5),
        'norm_ffn2': gain(ks[23], (DEPTH, D)),
        'ffn2_w_in': nrm(ks[24], (DEPTH, D, 2 * D_FF), D ** -0.5),
        'ffn2_w_out': nrm(ks[25], (DEPTH, D_FF, D), D_FF ** -0.5),
        'final_norm': gain(ks[26], (D,)),
    }


def reference(x_prompt, x_sample, cache_nat_k, cache_nat_v, cache_diff_k, cache_diff_v, c, c_ctx,
              mod_w, mod_b, norm_ffn1, ffn1_w_in, ffn1_w_out, norm_mix, even_w_in, pool_w,
              pool_scale, conv_w, odd_w_in, nat_rpb, diff_lambda, diff_norm, mix_w_out,
              norm_ffn2, ffn2_w_in, ffn2_w_out, final_norm):
    cos, sin = axial_rope_tables(x_sample.shape[1])
    ctx, lat = x_prompt, x_sample
    new_nk, new_nv, new_dk, new_dv = [], [], [], []
    for l in range(DEPTH):
        m_ctx = adaln_mods(c_ctx[None, :], mod_w[l], mod_b[l])
        m_lat = adaln_mods(c, mod_w[l], mod_b[l])
        ctx = ffn_half_step(ctx, m_ctx[0:3], norm_ffn1[l], ffn1_w_in[l], ffn1_w_out[l])
        lat = ffn_half_step(lat, m_lat[0:3], norm_ffn1[l], ffn1_w_in[l], ffn1_w_out[l])
        h_ctx = modulated_norm(ctx, m_ctx[3], m_ctx[4], norm_mix[l])
        h_lat = modulated_norm(lat, m_lat[3], m_lat[4], norm_mix[l])
        if l % 2 == 0:
            e = l // 2
            y_ctx = even_mixer(h_ctx, even_w_in[e], pool_w[e], pool_scale[e], conv_w[e], mix_w_out[l])
            y_lat = even_mixer(h_lat, even_w_in[e], pool_w[e], pool_scale[e], conv_w[e], mix_w_out[l])
        else:
            o = l // 2
            y_ctx, nk, nv, dk, dv = odd_mixer_context(h_ctx, odd_w_in[o], diff_lambda[o], diff_norm[o],
                                                      mix_w_out[l], l)
            new_nk.append(nk)
            new_nv.append(nv)
            new_dk.append(dk)
            new_dv.append(dv)
            y_lat = odd_mixer_latent(h_lat, cache_nat_k[:, o], cache_nat_v[:, o], cache_diff_k[:, o],
                                     cache_diff_v[:, o], odd_w_in[o], nat_rpb[o], diff_lambda[o],
                                     diff_norm[o], mix_w_out[l], l, cos, sin)
        ctx = ctx + m_ctx[5] * y_ctx
        lat = lat + m_lat[5] * y_lat
        ctx = ffn_half_step(ctx, m_ctx[6:9], norm_ffn2[l], ffn2_w_in[l], ffn2_w_out[l])
        lat = ffn_half_step(lat, m_lat[6:9], norm_ffn2[l], ffn2_w_in[l], ffn2_w_out[l])
    y_prompt = rmsnorm(ctx, final_norm)
    y_sample = rmsnorm(lat, final_norm)
    new_nat_k = jnp.stack(new_nk, axis=1)
    new_nat_v = jnp.stack(new_nv, axis=1)
    new_diff_k = jnp.stack(new_dk, axis=1)
    new_diff_v = jnp.stack(new_dv, axis=1)
    return (y_prompt, y_sample, new_nat_k, new_nat_v, new_diff_k, new_diff_v)
```

```python
import functools
import math

import numpy as np
import jax
import jax.numpy as jnp
from jax import lax
from jax.experimental import pallas as pl
from jax.experimental.pallas import tpu as pltpu

D_MODEL = 1024
D_FF = 2816
N_MOD = 9
DEPTH = 2
GRID_W = 64
HEAD_DIM = 64
W_POOL = 512
W_CONV = 512
POOL_WINDOWS = (2, 4, 8, 16)
POOL_G = 128
W_NAT = 512
W_DIFF = 512
H_NAT = 8
H_DIFF = 4
NAT_WIN_R = 8
NAT_WIN_C = 16
ROPE_THETA = 10000.0
ATTN_SCALE = HEAD_DIM ** -0.5
EPS = 1e-6
NEG = -1e30

LANES = 128
SUBLANES = 8
VMEM_LIMIT = 60 * 1024 * 1024

BF16 = jnp.bfloat16
F32 = jnp.float32


def _dot(a, b):
    return jnp.dot(a, b, preferred_element_type=F32)


def _dot_nt(a, b):
    return lax.dot_general(a, b, (((1,), (1,)), ((), ())), preferred_element_type=F32)


def _sigmoid(x):
    return 1.0 / (1.0 + jnp.exp(-x))


def _rms(x):
    return x * lax.rsqrt(jnp.mean(x * x, axis=-1, keepdims=True) + EPS)


def _modnorm(x, shift, scale, g):
    return (_rms(x) * g) * (1.0 + scale) + shift


def _ffn_half_step(x, mods, base, g, w_in_ref, w_out_ref, ck):
    shift, scale, gate = mods[base:base + 1], mods[base + 1:base + 2], mods[base + 2:base + 3]
    h = _modnorm(x, shift, scale, g).astype(BF16)
    acc = None
    for c in range(D_FF // ck):
        a = _dot(h, w_in_ref[:, c * ck:(c + 1) * ck])
        b = _dot(h, w_in_ref[:, D_FF + c * ck:D_FF + (c + 1) * ck])
        gated = ((a * _sigmoid(a)) * b).astype(BF16)
        t = _dot(gated, w_out_ref[c * ck:(c + 1) * ck, :])
        acc = t if acc is None else acc + t
    return x + (0.5 * gate) * acc


def _const_spec(shape):
    nd = len(shape)
    return pl.BlockSpec(shape, lambda i, _nd=nd: (0,) * _nd, pipeline_mode=pl.Buffered(1))


def _mods_kernel(cond_ref, w_ref, b_ref, o_ref):
    cnd = cond_ref[...]
    s = (cnd * _sigmoid(cnd)).astype(BF16)
    o_ref[0] = _dot(s, w_ref[0].astype(BF16)) + b_ref[0]


def _adaln_mods(cond, mod_w, mod_b):
    tn = 1152
    nt = (N_MOD * D_MODEL) // tn
    return pl.pallas_call(
        _mods_kernel,
        grid=(DEPTH, nt),
        in_specs=[
            pl.BlockSpec((SUBLANES, D_MODEL), lambda l, j: (0, 0)),
            pl.BlockSpec((1, D_MODEL, tn), lambda l, j: (l, 0, j)),
            pl.BlockSpec((1, 1, tn), lambda l, j: (l, 0, j)),
        ],
        out_specs=pl.BlockSpec((1, SUBLANES, tn), lambda l, j: (l, 0, j)),
        out_shape=jax.ShapeDtypeStruct((DEPTH, SUBLANES, N_MOD * D_MODEL), F32),
        compiler_params=pltpu.CompilerParams(vmem_limit_bytes=VMEM_LIMIT),
        name="adaln_mods",
    )(cond, mod_w, mod_b.reshape(DEPTH, 1, N_MOD * D_MODEL))


def _rope(x, cos_t, sin_a, sin_b):
    w = x.shape[1]
    reps = w // LANES
    c = jnp.concatenate([cos_t] * reps, axis=1)
    sa = jnp.concatenate([sin_a] * reps, axis=1)
    sb = jnp.concatenate([sin_b] * reps, axis=1)
    quarter = HEAD_DIM // 4
    up = pltpu.roll(x, w - quarter, 1)
    dn = pltpu.roll(x, quarter, 1)
    return x * c + up * sa + dn * sb


def _ka_kernel(*refs, mode, ck):
    if mode == "lat":
        (x_ref, mods_ref, g1_ref, w_in_ref, w_out_ref, gm_ref, wp_ref, cos_ref, sa_ref, sb_ref) = refs[:10]
        outs = refs[10:]
    else:
        (x_ref, mods_ref, g1_ref, w_in_ref, w_out_ref, gm_ref, wp_ref) = refs[:7]
        outs = refs[7:]
    mods = mods_ref[0]
    x1 = _ffn_half_step(x_ref[...], mods, 0, g1_ref[...], w_in_ref, w_out_ref, ck)
    outs[0][...] = x1
    h = _modnorm(x1, mods[3:4], mods[4:5], gm_ref[...]).astype(BF16)
    if mode == "even":
        outs[1][...] = _dot(h, wp_ref[...])
        return
    for p in range(6):
        u = _dot(h, wp_ref[:, p * 512:(p + 1) * 512])
        if mode == "lat":
            if p in (3, 4):
                u = _rope(u, cos_ref[...], sa_ref[...], sb_ref[...])
            if p in (0, 3):
                u = u * ATTN_SCALE
            outs[1 + p][...] = u.astype(BF16)
        else:
            outs[1 + p][...] = u


def _kernel_a(x, mods_l, row_fn, g1, w_in, w_out, gm, wp, mode, tm, rope_tabs=None, ck=256):
    t = x.shape[0]
    nt = t // tm
    tok = lambda w: pl.BlockSpec((tm, w), lambda i: (i, 0))
    in_specs = [
        tok(D_MODEL),
        pl.BlockSpec((1, N_MOD, D_MODEL), lambda i: (row_fn(i), 0, 0)),
        _const_spec((1, D_MODEL)),
        _const_spec(w_in.shape),
        _const_spec(w_out.shape),
        _const_spec((1, D_MODEL)),
        _const_spec(wp.shape),
    ]
    args = [x, mods_l, g1.reshape(1, D_MODEL), w_in, w_out, gm.reshape(1, D_MODEL), wp]
    if mode == "lat":
        seq_tiles = rope_tabs[0].shape[0] // tm
        for tab in rope_tabs:
            in_specs.append(pl.BlockSpec((tm, LANES), lambda i, _s=seq_tiles: (i % _s, 0)))
            args.append(tab)
    out_shape = [jax.ShapeDtypeStruct((t, D_MODEL), F32)]
    out_specs = [tok(D_MODEL)]
    if mode == "even":
        out_shape.append(jax.ShapeDtypeStruct((t, wp.shape[1]), F32))
        out_specs.append(tok(wp.shape[1]))
    else:
        dt = BF16 if mode == "lat" else F32
        for _ in range(6):
            out_shape.append(jax.ShapeDtypeStruct((t, 512), dt))
            out_specs.append(tok(512))
    return pl.pallas_call(
        functools.partial(_ka_kernel, mode=mode, ck=ck),
        grid=(nt,),
        in_specs=in_specs,
        out_specs=out_specs,
        out_shape=out_shape,
        compiler_params=pltpu.CompilerParams(
            dimension_semantics=("arbitrary",), vmem_limit_bytes=VMEM_LIMIT),
        name="ffn1_proj_" + mode,
    )(*args)


HALO = 8


def _even_mix(ext_ref, pos, seq_len, tm, pw_ref, ps_ref, cw_ref):
    def rows(j, lo, hi):
        return ext_ref[HALO + j:HALO + j + tm, lo:hi]

    def valid(j):
        p = pos + j
        return jnp.logical_and(p >= 0, p < seq_len)

    ya = []
    for g, win in enumerate(POOL_WINDOWS):
        half = win // 2
        lo_c, hi_c = g * POOL_G, (g + 1) * POOL_G
        s = rows(0, lo_c, hi_c)
        for j in range(-half, half):
            if j == 0:
                continue
            s = s + jnp.where(valid(j), rows(j, lo_c, hi_c), 0.0)
        lo = jnp.clip(pos - half, 0, seq_len - 1)
        hi = jnp.clip(pos + half - 1, 0, seq_len - 1)
        cnt = (hi - lo + 1).astype(F32)
        d = (s / cnt - rows(0, lo_c, hi_c)).astype(BF16)
        ya.append(_dot(d, pw_ref[g]))
    ya = jnp.concatenate(ya, axis=1) * ps_ref[...]

    def z(j):
        return rows(j, W_POOL + 2 * W_CONV, W_POOL + 3 * W_CONV) * rows(j, W_POOL, W_POOL + W_CONV)

    vm = jnp.concatenate([valid(-1)] * (W_CONV // LANES), axis=1)
    vp = jnp.concatenate([valid(1)] * (W_CONV // LANES), axis=1)
    y = (jnp.where(vm, z(-1), 0.0) * cw_ref[0:1, :] + z(0) * cw_ref[1:2, :]
         + jnp.where(vp, z(1), 0.0) * cw_ref[2:3, :])
    yb = rows(0, W_POOL + W_CONV, W_POOL + 2 * W_CONV) * y
    return jnp.concatenate([ya, yb], axis=1)


def _kb_kernel(*refs, mode, ck, tm, seq_len, final):
    x_ref, mods_ref = refs[0], refs[1]
    k = 2
    if mode == "even":
        u_ref, up_ref, un_ref, pw_ref, ps_ref, cw_ref = refs[k:k + 6]
        k += 6
    else:
        n_parts = 2 if mode == "odd2" else 1
        o_refs = refs[k:k + n_parts]
        k += n_parts
    wmo_ref, g2_ref, w_in_ref, w_out_ref = refs[k:k + 4]
    k += 4
    if final:
        gf_ref = refs[k]
        k += 1
    out_ref = refs[k]
    k += 1
    mods = mods_ref[0]
    if mode == "even":
        ext_ref = refs[k]
        ext_ref[0:HALO, :] = up_ref[...]
        ext_ref[HALO:HALO + tm, :] = u_ref[...]
        ext_ref[HALO + tm:HALO + tm + HALO, :] = un_ref[...]
        row = lax.broadcasted_iota(jnp.int32, (tm, LANES), 0) + pl.program_id(0) * tm
        pos = jnp.bitwise_and(row, seq_len - 1)
        feat = _even_mix(ext_ref, pos, seq_len, tm, pw_ref, ps_ref, cw_ref).astype(BF16)
        y = _dot(feat, wmo_ref[...])
    else:
        y = None
        off = 0
        for o_ref in o_refs:
            w = o_ref.shape[1]
            t = _dot(o_ref[...], wmo_ref[off:off + w, :])
            y = t if y is None else y + t
            off += w
    x2 = x_ref[...] + mods[5:6] * y
    x3 = _ffn_half_step(x2, mods, 6, g2_ref[...], w_in_ref, w_out_ref, ck)
    if final:
        x3 = _rms(x3) * gf_ref[...]
    out_ref[...] = x3


def _kernel_b(x, mods_l, row_fn, mix_in, mix_params, wmo, g2, w_in, w_out, mode, tm, seq_len=None,
              final_g=None, ck=256):
    t = x.shape[0]
    nt = t // tm
    tok = lambda w: pl.BlockSpec((tm, w), lambda i: (i, 0))
    in_specs = [tok(D_MODEL), pl.BlockSpec((1, N_MOD, D_MODEL), lambda i: (row_fn(i), 0, 0))]
    args = [x, mods_l]
    scratch = []
    if mode == "even":
        u = mix_in[0]
        wu = u.shape[1]
        hb = tm // HALO
        last = t // HALO - 1
        in_specs += [
            tok(wu),
            pl.BlockSpec((HALO, wu), lambda i: (jnp.maximum(i * hb - 1, 0), 0)),
            pl.BlockSpec((HALO, wu), lambda i: (jnp.minimum((i + 1) * hb, last), 0)),
        ]
        args += [u, u, u]
        pool_w, pool_scale, conv_w = mix_params
        in_specs += [_const_spec(pool_w.shape), _const_spec((1, W_POOL)), _const_spec(conv_w.shape)]
        args += [pool_w, pool_scale.reshape(1, W_POOL), conv_w]
        scratch.append(pltpu.VMEM((tm + 2 * HALO, wu), F32))
    else:
        for o in mix_in:
            in_specs.append(tok(o.shape[1]))
            args.append(o)
    in_specs += [_const_spec(wmo.shape), _const_spec((1, D_MODEL)), _const_spec(w_in.shape),
                 _const_spec(w_out.shape)]
    args += [wmo, g2.reshape(1, D_MODEL), w_in, w_out]
    final = final_g is not None
    if final:
        in_specs.append(_const_spec((1, D_MODEL)))
        args.append(final_g.reshape(1, D_MODEL))
    kmode = mode if mode == "even" else ("odd2" if len(mix_in) == 2 else "odd1")
    return pl.pallas_call(
        functools.partial(_kb_kernel, mode=kmode, ck=ck, tm=tm, seq_len=seq_len, final=final),
        grid=(nt,),
        in_specs=in_specs,
        out_specs=tok(D_MODEL),
        out_shape=jax.ShapeDtypeStruct((t, D_MODEL), F32),
        scratch_shapes=scratch,
        compiler_params=pltpu.CompilerParams(
            dimension_semantics=("arbitrary",), vmem_limit_bytes=VMEM_LIMIT),
        name="mix_ffn2_" + kmode,
    )(*args)


def _diff_lambda(lam_ref, lam_init):
    lp = lam_ref[...]
    s1 = jnp.sum(lp[0:1] * lp[1:2], axis=-1, keepdims=True)
    s2 = jnp.sum(lp[2:3] * lp[3:4], axis=-1, keepdims=True)
    return jnp.exp(s1) - jnp.exp(s2) + lam_init


def _half_masks():
    lane = lax.broadcasted_iota(jnp.int32, (1, LANES), 1)
    lo = lane < HEAD_DIM
    return lo, jnp.logical_not(lo)


def _softmax_parts(parts):
    m = None
    for s in parts:
        mi = jnp.max(s, axis=-1, keepdims=True)
        m = mi if m is None else jnp.maximum(m, mi)
    es = [jnp.exp(s - m) for s in parts]
    l = None
    for e in es:
        li = jnp.sum(e, axis=-1, keepdims=True)
        l = li if l is None else l + li
    return es, l


def _diff_out_norm(o, dn_ref, lam_init):
    return (_rms(o) * dn_ref[...]) * (1.0 - lam_init)


def _ctx_attn_kernel(nq_ref, nk_ref, nv_ref, dq_ref, dk_ref, dv_ref, lam_ref, dn_ref, o_ref, *, lam_init):
    lo, hi = _half_masks()
    for hp in range(H_NAT // 2):
        sl = slice(hp * LANES, (hp + 1) * LANES)
        q = nq_ref[:, sl] * ATTN_SCALE
        k = nk_ref[:, sl].astype(BF16)
        v = nv_ref[:, sl].astype(BF16)
        outs = []
        for msk in (lo, hi):
            qm = jnp.where(msk, q, 0.0).astype(BF16)
            (e,), l = _softmax_parts([_dot_nt(qm, k)])
            outs.append(_dot(e.astype(BF16), v) * (1.0 / l))
        o_ref[:, sl] = jnp.where(lo, outs[0], outs[1]).astype(o_ref.dtype)
    lam = _diff_lambda(lam_ref, lam_init)
    for h in range(H_DIFF):
        sl = slice(h * LANES, (h + 1) * LANES)
        q = dq_ref[:, sl] * ATTN_SCALE
        k = dk_ref[:, sl].astype(BF16)
        v = dv_ref[:, sl].astype(BF16)
        ps = []
        for msk in (lo, hi):
            qm = jnp.where(msk, q, 0.0).astype(BF16)
            (e,), l = _softmax_parts([_dot_nt(qm, k)])
            ps.append(e * (1.0 / l))
        a = (ps[0] - lam * ps[1]).astype(BF16)
        o = _diff_out_norm(_dot(a, v), dn_ref, lam_init)
        o_ref[:, W_NAT + h * LANES:W_NAT + (h + 1) * LANES] = o.astype(o_ref.dtype)


def _ctx_attention(parts, lam_p, dnorm, seq, lam_init):
    t = parts[0].shape[0]
    nb = t // seq
    blk = pl.BlockSpec((seq, 512), lambda i: (i, 0))
    return pl.pallas_call(
        functools.partial(_ctx_attn_kernel, lam_init=lam_init),
        grid=(nb,),
        in_specs=[blk] * 6 + [_const_spec((4, HEAD_DIM)), _const_spec((1, 2 * HEAD_DIM))],
        out_specs=pl.BlockSpec((seq, D_MODEL), lambda i: (i, 0)),
        out_shape=jax.ShapeDtypeStruct((t, D_MODEL), BF16),
        compiler_params=pltpu.CompilerParams(
            dimension_semantics=("arbitrary",), vmem_limit_bytes=VMEM_LIMIT),
        name="ctx_attention",
    )(*parts, lam_p, dnorm.reshape(1, 2 * HEAD_DIM))


NAT_QROWS = 8
NAT_KROWS = 16
NAT_NDR = 2 * NAT_WIN_R


def _build_bias_tiles(rpb_ref, u_ref):
    c = lax.broadcasted_iota(jnp.int32, (GRID_W, LANES), 0)
    lane = lax.broadcasted_iota(jnp.int32, (GRID_W, LANES), 1)
    kc = jnp.bitwise_and(lane, GRID_W - 1)
    c0 = jnp.clip(c - NAT_WIN_C // 2, 0, GRID_W - NAT_WIN_C)
    col_ok = jnp.logical_and(kc >= c0, kc < c0 + NAT_WIN_C)
    left = lane < GRID_W
    n_dr = 2 * NAT_WIN_R - 1
    for h in range(H_NAT):
        tl, tr = [], []
        for dr in range(n_dr):
            row = jnp.broadcast_to(rpb_ref[h, dr:dr + 1, :], (GRID_W, LANES))
            tl.append(pltpu.roll(row, LANES - (NAT_WIN_C - 1), 1, stride=1, stride_axis=0))
            tr.append(pltpu.roll(row, LANES - (NAT_WIN_C - 1) + GRID_W, 1, stride=1, stride_axis=0))
        for di in range(NAT_NDR):
            dl = di - NAT_WIN_R + (NAT_WIN_R - 1)
            drr = dl + 1
            lv = tl[dl] if 0 <= dl < n_dr else jnp.full((GRID_W, LANES), NEG, F32)
            rv = tr[drr] if 0 <= drr < n_dr else jnp.full((GRID_W, LANES), NEG, F32)
            u_ref[h, di] = jnp.where(col_ok, jnp.where(left, lv, rv), NEG)


def _nat_lat_kernel(q_ref, k_ref, v_ref, kc_ref, vc_ref, rpb_ref, o_ref, u_ref, s_ref):
    b = pl.program_id(1)

    @pl.when(jnp.logical_and(pl.program_id(0) == 0, b == 0))
    def _():
        _build_bias_tiles(rpb_ref, u_ref)

    rows = k_ref.shape[0] // GRID_W
    kb = jnp.clip(b * NAT_QROWS - NAT_WIN_R // 2, 0, rows - NAT_KROWS)
    koff = pl.multiple_of(kb * GRID_W, GRID_W)
    lo, hi = _half_masks()
    left = lax.broadcasted_iota(jnp.int32, (GRID_W, LANES), 1) < GRID_W
    nq = NAT_QROWS * GRID_W
    for hp in range(H_NAT // 2):
        sl = slice(hp * LANES, (hp + 1) * LANES)
        q = q_ref[:, sl]
        k = k_ref[pl.ds(koff, NAT_KROWS * GRID_W), sl]
        v = v_ref[pl.ds(koff, NAT_KROWS * GRID_W), sl]
        kc = kc_ref[:, sl].astype(BF16)
        vc = vc_ref[:, sl].astype(BF16)
        outs = []
        for e, msk in enumerate((lo, hi)):
            h = 2 * hp + e
            qm = jnp.where(msk, q, jnp.zeros_like(q))
            s_ref[...] = _dot_nt(qm, k)
            for i in range(NAT_QROWS):
                r = b * NAT_QROWS + i
                r0 = jnp.clip(r - NAT_WIN_R // 2, 0, rows - NAT_WIN_R)
                for jp in range(NAT_KROWS // 2):
                    kl = kb + 2 * jp
                    ok_l = jnp.logical_and(kl >= r0, kl < r0 + NAT_WIN_R)
                    ok_r = jnp.logical_and(kl + 1 >= r0, kl + 1 < r0 + NAT_WIN_R)
                    di = jnp.clip(kl - r + NAT_WIN_R, 0, NAT_NDR - 1)
                    ok = jnp.where(left, ok_l.astype(jnp.int32), ok_r.astype(jnp.int32))
                    bias = jnp.where(ok > 0, u_ref[h, di], NEG)
                    tile = (slice(i * GRID_W, (i + 1) * GRID_W), slice(jp * LANES, (jp + 1) * LANES))
                    s_ref[tile] = s_ref[tile] + bias
            s_ctx = _dot_nt(qm, kc)
            (e_loc, e_ctx), l = _softmax_parts([s_ref[...], s_ctx])
            o = _dot(e_loc.astype(BF16), v) + _dot(e_ctx.astype(BF16), vc)
            outs.append(o * (1.0 / l))
        o_ref[:, sl] = jnp.where(lo, outs[0], outs[1]).astype(o_ref.dtype)


def _nat_lat_attention(q, k, v, kc, vc, rpb_pad, batch, seq):
    nq = NAT_QROWS * GRID_W
    nblk = seq // nq
    p = kc.shape[1]
    return pl.pallas_call(
        _nat_lat_kernel,
        grid=(batch, nblk),
        in_specs=[
            pl.BlockSpec((nq, W_NAT), lambda bb, i: (bb * nblk + i, 0)),
            pl.BlockSpec((seq, W_NAT), lambda bb, i: (bb, 0)),
            pl.BlockSpec((seq, W_NAT), lambda bb, i: (bb, 0)),
            pl.BlockSpec((None, p, W_NAT), lambda bb, i: (bb, 0, 0)),
            pl.BlockSpec((None, p, W_NAT), lambda bb, i: (bb, 0, 0)),
            pl.BlockSpec(rpb_pad.shape, lambda bb, i: (0, 0, 0)),
        ],
        out_specs=pl.BlockSpec((nq, W_NAT), lambda bb, i: (bb * nblk + i, 0)),
        out_shape=jax.ShapeDtypeStruct((batch * seq, W_NAT), BF16),
        scratch_shapes=[
            pltpu.VMEM((H_NAT, NAT_NDR, GRID_W, LANES), F32),
            pltpu.VMEM((nq, NAT_KROWS * GRID_W), F32),
        ],
        compiler_params=pltpu.CompilerParams(
            dimension_semantics=("arbitrary", "arbitrary"), vmem_limit_bytes=VMEM_LIMIT),
        name="nat_latent_attention",
    )(q, k, v, kc, vc, rpb_pad)


def _diff_lat_kernel(q_ref, k_ref, v_ref, kc_ref, vc_ref, lam_ref, dn_ref, o_ref, *, lam_init):
    lo, hi = _half_masks()
    lam = _diff_lambda(lam_ref, lam_init)
    q = q_ref[...]
    k = k_ref[...]
    kc = kc_ref[...].astype(BF16)
    ps = []
    for msk in (lo, hi):
        qm = jnp.where(msk, q, jnp.zeros_like(q))
        es, l = _softmax_parts([_dot_nt(qm, k), _dot_nt(qm, kc)])
        ps.append((es, 1.0 / l))
    (e0, r0), (e1, r1) = ps
    r1 = lam * r1
    a_lat = (e0[0] * r0 - e1[0] * r1).astype(BF16)
    a_ctx = (e0[1] * r0 - e1[1] * r1).astype(BF16)
    o = _dot(a_lat, v_ref[...]) + _dot(a_ctx, vc_ref[...].astype(BF16))
    o_ref[...] = _diff_out_norm(o, dn_ref, lam_init).astype(o_ref.dtype)


def _diff_lat_attention(q, k, v, kc, vc, lam_p, dnorm, batch, seq, lam_init, tq=256):
    nqb = seq // tq
    p = kc.shape[1]
    return pl.pallas_call(
        functools.partial(_diff_lat_kernel, lam_init=lam_init),
        grid=(batch, H_DIFF, nqb),
        in_specs=[
            pl.BlockSpec((tq, LANES), lambda bb, h, i: (bb * nqb + i, h)),
            pl.BlockSpec((seq, LANES), lambda bb, h, i: (bb, h)),
            pl.BlockSpec((seq, LANES), lambda bb, h, i: (bb, h)),
            pl.BlockSpec((None, p, LANES), lambda bb, h, i: (bb, 0, h)),
            pl.BlockSpec((None, p, LANES), lambda bb, h, i: (bb, 0, h)),
            pl.BlockSpec((4, HEAD_DIM), lambda bb, h, i: (0, 0)),
            pl.BlockSpec((1, 2 * HEAD_DIM), lambda bb, h, i: (0, 0)),
        ],
        out_specs=pl.BlockSpec((tq, LANES), lambda bb, h, i: (bb * nqb + i, h)),
        out_shape=jax.ShapeDtypeStruct((batch * seq, W_DIFF), BF16),
        compiler_params=pltpu.CompilerParams(
            dimension_semantics=("arbitrary", "arbitrary", "arbitrary"), vmem_limit_bytes=VMEM_LIMIT),
        name="diff_latent_attention",
    )(q, k, v, kc, vc, lam_p, dnorm.reshape(1, 2 * HEAD_DIM))


def _rope_tables(n):
    t = np.arange(n)
    row = (t // GRID_W).astype(np.float64)
    col = (t % GRID_W).astype(np.float64)
    quarter = HEAD_DIM // 4
    inv = 1.0 / (ROPE_THETA ** (np.arange(quarter) / quarter))
    cr, sr = np.cos(row[:, None] * inv[None]), np.sin(row[:, None] * inv[None])
    cc, sc = np.cos(col[:, None] * inv[None]), np.sin(col[:, None] * inv[None])
    zero = np.zeros_like(sr)
    cos_t = np.concatenate([cr, cr, cc, cc], axis=1)
    sin_a = np.concatenate([-sr, zero, -sc, zero], axis=1)
    sin_b = np.concatenate([zero, sr, zero, sc], axis=1)
    tile = lambda a: jnp.asarray(np.concatenate([a, a], axis=1).astype(np.float32))
    return tile(cos_t), tile(sin_a), tile(sin_b)


def kernel(x_prompt, x_sample, cache_nat_k, cache_nat_v, cache_diff_k, cache_diff_v, c, c_ctx, mod_w, mod_b, norm_ffn1, ffn1_w_in, ffn1_w_out, norm_mix, even_w_in, pool_w, pool_scale, conv_w, odd_w_in, nat_rpb, diff_lambda, diff_norm, mix_w_out, norm_ffn2, ffn2_w_in, ffn2_w_out, final_norm):
    batch, seq, _ = x_prompt.shape
    dbatch, dseq, _ = x_sample.shape
    past = cache_nat_k.shape[2]
    tm = 512

    cond = jnp.zeros((SUBLANES, D_MODEL), F32).at[0].set(c_ctx).at[1:1 + dbatch].set(c)
    mods = _adaln_mods(cond, mod_w, mod_b).reshape(DEPTH, SUBLANES, N_MOD, D_MODEL)

    ctx_row = lambda i: 0
    lat_row = lambda i: 1 + (i * tm) // dseq

    bf = lambda w: w.astype(BF16)
    f1_in, f1_out, f2_in, f2_out = bf(ffn1_w_in), bf(ffn1_w_out), bf(ffn2_w_in), bf(ffn2_w_out)
    wmo = bf(mix_w_out)
    w_even, w_odd, w_pool = bf(even_w_in), bf(odd_w_in), bf(pool_w)

    ctx = x_prompt.reshape(batch * seq, D_MODEL)
    lat = x_sample.reshape(dbatch * dseq, D_MODEL)

    l = 0
    streams = []
    for x, row_fn, n in ((ctx, ctx_row, seq), (lat, lat_row, dseq)):
        x1, u = _kernel_a(x, mods[l], row_fn, norm_ffn1[l], f1_in[l], f1_out[l], norm_mix[l], w_even[0],
                          "even", tm)
        x2 = _kernel_b(x1, mods[l], row_fn, [u], (w_pool[0], pool_scale[0], conv_w[0]), wmo[l],
                       norm_ffn2[l], f2_in[l], f2_out[l], "even", tm, seq_len=n)
        streams.append(x2)
    ctx, lat = streams

    l = 1
    lam_init = 0.8 - 0.6 * math.exp(-0.3 * l)
    ctx_parts = _kernel_a(ctx, mods[l], ctx_row, norm_ffn1[l], f1_in[l], f1_out[l], norm_mix[l], w_odd[0],
                          "ctx", tm)
    ctx1, (nq, nk, nv, dq, dk, dv) = ctx_parts[0], ctx_parts[1:]
    o_ctx = _ctx_attention([nq, nk, nv, dq, dk, dv], diff_lambda[0], diff_norm[0], seq, lam_init)
    y_prompt = _kernel_b(ctx1, mods[l], ctx_row, [o_ctx], None, wmo[l], norm_ffn2[l], f2_in[l], f2_out[l],
                         "odd", tm, final_g=final_norm)

    lat_parts = _kernel_a(lat, mods[l], lat_row, norm_ffn1[l], f1_in[l], f1_out[l], norm_mix[l], w_odd[0],
                          "lat", tm, rope_tabs=_rope_tables(dseq))
    lat1, (lq, lk, lv, ldq, ldk, ldv) = lat_parts[0], lat_parts[1:]
    rpb_pad = jnp.zeros((H_NAT, 2 * NAT_WIN_R, LANES), F32).at[:, :2 * NAT_WIN_R - 1,
                                                                :2 * NAT_WIN_C - 1].set(nat_rpb[0])
    o_nat = _nat_lat_attention(lq, lk, lv, cache_nat_k[:, 0].reshape(dbatch, past, W_NAT),
                               cache_nat_v[:, 0].reshape(dbatch, past, W_NAT), rpb_pad, dbatch, dseq)
    o_diff = _diff_lat_attention(ldq, ldk, ldv, cache_diff_k[:, 0].reshape(dbatch, past, W_DIFF),
                                 cache_diff_v[:, 0].reshape(dbatch, past, W_DIFF), diff_lambda[0],
                                 diff_norm[0], dbatch, dseq, lam_init)
    y_sample = _kernel_b(lat1, mods[l], lat_row, [o_nat, o_diff], None, wmo[l], norm_ffn2[l], f2_in[l],
                         f2_out[l], "odd", tm, final_g=final_norm)

    new_nat_k = nk.reshape(batch, 1, seq, H_NAT, HEAD_DIM)
    new_nat_v = nv.reshape(batch, 1, seq, H_NAT, HEAD_DIM)
    new_diff_k = dk.reshape(batch, 1, seq, H_DIFF, 2, HEAD_DIM)
    new_diff_v = dv.reshape(batch, 1, seq, H_DIFF, 2 * HEAD_DIM)
    return (y_prompt.reshape(batch, seq, D_MODEL), y_sample.reshape(dbatch, dseq, D_MODEL),
            new_nat_k, new_nat_v, new_diff_k, new_diff_v)
```

```python
import functools
import math

import numpy as np
import jax
import jax.numpy as jnp
from jax import lax
from jax.experimental import pallas as pl
from jax.experimental.pallas import tpu as pltpu

D_MODEL = 1024
D_FF = 2816
N_MOD = 9
DEPTH = 2
GRID_W = 64
HEAD_DIM = 64
W_POOL = 512
W_CONV = 512
POOL_WINDOWS = (2, 4, 8, 16)
POOL_G = 128
W_NAT = 512
W_DIFF = 512
H_NAT = 8
H_DIFF = 4
NAT_WIN_R = 8
NAT_WIN_C = 16
ROPE_THETA = 10000.0
ATTN_SCALE = HEAD_DIM ** -0.5
LOG2E = math.log2(math.e)
Q_SCALE = ATTN_SCALE * LOG2E
EPS = 1e-6
NEG = -1e30

LANES = 128
SUBLANES = 8
VMEM_LIMIT = 60 * 1024 * 1024

BF16 = jnp.bfloat16
F32 = jnp.float32


def _dot(a, b):
    return jnp.dot(a, b, preferred_element_type=F32)


def _dot_nt(a, b):
    return lax.dot_general(a, b, (((1,), (1,)), ((), ())), preferred_element_type=F32)


def _sigmoid(x):
    return 1.0 / (1.0 + jnp.exp(-x))


def _rms(x):
    return x * lax.rsqrt(jnp.mean(x * x, axis=-1, keepdims=True) + EPS)


def _modnorm(x, shift, scale, g):
    return (_rms(x) * g) * (1.0 + scale) + shift


def _ffn_half_step(x, mods, base, g, w_in_ref, w_out_ref, ck):
    shift, scale, gate = mods[base:base + 1], mods[base + 1:base + 2], mods[base + 2:base + 3]
    h = _modnorm(x, shift, scale, g).astype(BF16)
    acc = None
    for c in range(D_FF // ck):
        a = _dot(h, w_in_ref[:, c * ck:(c + 1) * ck])
        b = _dot(h, w_in_ref[:, D_FF + c * ck:D_FF + (c + 1) * ck])
        gated = ((a * _sigmoid(a)) * b).astype(BF16)
        t = _dot(gated, w_out_ref[c * ck:(c + 1) * ck, :])
        acc = t if acc is None else acc + t
    return x + (0.5 * gate) * acc


def _const_spec(shape):
    nd = len(shape)
    return pl.BlockSpec(shape, lambda i, _nd=nd: (0,) * _nd, pipeline_mode=pl.Buffered(1))


def _mods_kernel(cond_ref, w_ref, b_ref, o_ref):
    cnd = cond_ref[...]
    s = (cnd * _sigmoid(cnd)).astype(BF16)
    o_ref[0] = _dot(s, w_ref[0].astype(BF16)) + b_ref[0]


def _adaln_mods(cond, mod_w, mod_b):
    tn = 1152
    nt = (N_MOD * D_MODEL) // tn
    return pl.pallas_call(
        _mods_kernel,
        grid=(DEPTH, nt),
        in_specs=[
            pl.BlockSpec((SUBLANES, D_MODEL), lambda l, j: (0, 0)),
            pl.BlockSpec((1, D_MODEL, tn), lambda l, j: (l, 0, j)),
            pl.BlockSpec((1, 1, tn), lambda l, j: (l, 0, j)),
        ],
        out_specs=pl.BlockSpec((1, SUBLANES, tn), lambda l, j: (l, 0, j)),
        out_shape=jax.ShapeDtypeStruct((DEPTH, SUBLANES, N_MOD * D_MODEL), F32),
        compiler_params=pltpu.CompilerParams(vmem_limit_bytes=VMEM_LIMIT),
        name="adaln_mods",
    )(cond, mod_w, mod_b.reshape(DEPTH, 1, N_MOD * D_MODEL))


def _rope(x, cos_t, sin_a, sin_b):
    w = x.shape[1]
    reps = w // LANES
    c = jnp.concatenate([cos_t] * reps, axis=1)
    sa = jnp.concatenate([sin_a] * reps, axis=1)
    sb = jnp.concatenate([sin_b] * reps, axis=1)
    quarter = HEAD_DIM // 4
    up = pltpu.roll(x, w - quarter, 1)
    dn = pltpu.roll(x, quarter, 1)
    return x * c + up * sa + dn * sb


def _ka_kernel(*refs, mode, ck):
    if mode == "lat":
        (x_ref, mods_ref, g1_ref, w_in_ref, w_out_ref, gm_ref, wp_ref, cos_ref, sa_ref, sb_ref) = refs[:10]
        outs = refs[10:]
    else:
        (x_ref, mods_ref, g1_ref, w_in_ref, w_out_ref, gm_ref, wp_ref) = refs[:7]
        outs = refs[7:]
    mods = mods_ref[0]
    x1 = _ffn_half_step(x_ref[...], mods, 0, g1_ref[...], w_in_ref, w_out_ref, ck)
    outs[0][...] = x1
    h = _modnorm(x1, mods[3:4], mods[4:5], gm_ref[...]).astype(BF16)
    if mode == "even":
        outs[1][...] = _dot(h, wp_ref[...])
        return
    for p in range(6):
        u = _dot(h, wp_ref[:, p * 512:(p + 1) * 512])
        if mode == "lat":
            if p in (3, 4):
                u = _rope(u, cos_ref[...], sa_ref[...], sb_ref[...])
            if p in (0, 3):
                u = u * Q_SCALE
            outs[1 + p][...] = u.astype(BF16)
        else:
            outs[1 + p][...] = u


def _kernel_a(x, mods_l, row_fn, g1, w_in, w_out, gm, wp, mode, tm, rope_tabs=None, ck=256):
    t = x.shape[0]
    nt = t // tm
    tok = lambda w: pl.BlockSpec((tm, w), lambda i: (i, 0))
    in_specs = [
        tok(D_MODEL),
        pl.BlockSpec((1, N_MOD, D_MODEL), lambda i: (row_fn(i), 0, 0)),
        _const_spec((1, D_MODEL)),
        _const_spec(w_in.shape),
        _const_spec(w_out.shape),
        _const_spec((1, D_MODEL)),
        _const_spec(wp.shape),
    ]
    args = [x, mods_l, g1.reshape(1, D_MODEL), w_in, w_out, gm.reshape(1, D_MODEL), wp]
    if mode == "lat":
        seq_tiles = rope_tabs[0].shape[0] // tm
        for tab in rope_tabs:
            in_specs.append(pl.BlockSpec((tm, LANES), lambda i, _s=seq_tiles: (i % _s, 0)))
            args.append(tab)
    out_shape = [jax.ShapeDtypeStruct((t, D_MODEL), F32)]
    out_specs = [tok(D_MODEL)]
    if mode == "even":
        out_shape.append(jax.ShapeDtypeStruct((t, wp.shape[1]), F32))
        out_specs.append(tok(wp.shape[1]))
    else:
        dt = BF16 if mode == "lat" else F32
        for _ in range(6):
            out_shape.append(jax.ShapeDtypeStruct((t, 512), dt))
            out_specs.append(tok(512))
    return pl.pallas_call(
        functools.partial(_ka_kernel, mode=mode, ck=ck),
        grid=(nt,),
        in_specs=in_specs,
        out_specs=out_specs,
        out_shape=out_shape,
        compiler_params=pltpu.CompilerParams(
            dimension_semantics=("arbitrary",), vmem_limit_bytes=VMEM_LIMIT),
        name="ffn1_proj_" + mode,
    )(*args)


HALO = 8


def _even_mix(ext_ref, pos, seq_len, tm, pw_ref, ps_ref, cw_ref):
    def rows(j, lo, hi):
        return ext_ref[HALO + j:HALO + j + tm, lo:hi]

    def valid(j):
        p = pos + j
        return jnp.logical_and(p >= 0, p < seq_len)

    ya = []
    for g, win in enumerate(POOL_WINDOWS):
        half = win // 2
        lo_c, hi_c = g * POOL_G, (g + 1) * POOL_G
        s = rows(0, lo_c, hi_c)
        for j in range(-half, half):
            if j == 0:
                continue
            s = s + jnp.where(valid(j), rows(j, lo_c, hi_c), 0.0)
        lo = jnp.clip(pos - half, 0, seq_len - 1)
        hi = jnp.clip(pos + half - 1, 0, seq_len - 1)
        cnt = (hi - lo + 1).astype(F32)
        d = (s / cnt - rows(0, lo_c, hi_c)).astype(BF16)
        ya.append(_dot(d, pw_ref[g]))
    ya = jnp.concatenate(ya, axis=1) * ps_ref[...]

    def z(j):
        return rows(j, W_POOL + 2 * W_CONV, W_POOL + 3 * W_CONV) * rows(j, W_POOL, W_POOL + W_CONV)

    vm = jnp.concatenate([valid(-1)] * (W_CONV // LANES), axis=1)
    vp = jnp.concatenate([valid(1)] * (W_CONV // LANES), axis=1)
    y = (jnp.where(vm, z(-1), 0.0) * cw_ref[0:1, :] + z(0) * cw_ref[1:2, :]
         + jnp.where(vp, z(1), 0.0) * cw_ref[2:3, :])
    yb = rows(0, W_POOL + W_CONV, W_POOL + 2 * W_CONV) * y
    return jnp.concatenate([ya, yb], axis=1)


def _kb_kernel(*refs, mode, ck, tm, seq_len, final):
    x_ref, mods_ref = refs[0], refs[1]
    k = 2
    if mode == "even":
        u_ref, up_ref, un_ref, pw_ref, ps_ref, cw_ref = refs[k:k + 6]
        k += 6
    else:
        n_parts = 2 if mode == "odd2" else 1
        o_refs = refs[k:k + n_parts]
        k += n_parts
    wmo_ref, g2_ref, w_in_ref, w_out_ref = refs[k:k + 4]
    k += 4
    if final:
        gf_ref = refs[k]
        k += 1
    out_ref = refs[k]
    k += 1
    mods = mods_ref[0]
    if mode == "even":
        ext_ref = refs[k]
        ext_ref[0:HALO, :] = up_ref[...]
        ext_ref[HALO:HALO + tm, :] = u_ref[...]
        ext_ref[HALO + tm:HALO + tm + HALO, :] = un_ref[...]
        row = lax.broadcasted_iota(jnp.int32, (tm, LANES), 0) + pl.program_id(0) * tm
        pos = jnp.bitwise_and(row, seq_len - 1)
        feat = _even_mix(ext_ref, pos, seq_len, tm, pw_ref, ps_ref, cw_ref).astype(BF16)
        y = _dot(feat, wmo_ref[...])
    else:
        y = None
        off = 0
        for o_ref in o_refs:
            w = o_ref.shape[1]
            t = _dot(o_ref[...], wmo_ref[off:off + w, :])
            y = t if y is None else y + t
            off += w
    x2 = x_ref[...] + mods[5:6] * y
    x3 = _ffn_half_step(x2, mods, 6, g2_ref[...], w_in_ref, w_out_ref, ck)
    if final:
        x3 = _rms(x3) * gf_ref[...]
    out_ref[...] = x3


def _kernel_b(x, mods_l, row_fn, mix_in, mix_params, wmo, g2, w_in, w_out, mode, tm, seq_len=None,
              final_g=None, ck=256):
    t = x.shape[0]
    nt = t // tm
    tok = lambda w: pl.BlockSpec((tm, w), lambda i: (i, 0))
    in_specs = [tok(D_MODEL), pl.BlockSpec((1, N_MOD, D_MODEL), lambda i: (row_fn(i), 0, 0))]
    args = [x, mods_l]
    scratch = []
    if mode == "even":
        u = mix_in[0]
        wu = u.shape[1]
        hb = tm // HALO
        last = t // HALO - 1
        in_specs += [
            tok(wu),
            pl.BlockSpec((HALO, wu), lambda i: (jnp.maximum(i * hb - 1, 0), 0)),
            pl.BlockSpec((HALO, wu), lambda i: (jnp.minimum((i + 1) * hb, last), 0)),
        ]
        args += [u, u, u]
        pool_w, pool_scale, conv_w = mix_params
        in_specs += [_const_spec(pool_w.shape), _const_spec((1, W_POOL)), _const_spec(conv_w.shape)]
        args += [pool_w, pool_scale.reshape(1, W_POOL), conv_w]
        scratch.append(pltpu.VMEM((tm + 2 * HALO, wu), F32))
    else:
        for o in mix_in:
            in_specs.append(tok(o.shape[1]))
            args.append(o)
    in_specs += [_const_spec(wmo.shape), _const_spec((1, D_MODEL)), _const_spec(w_in.shape),
                 _const_spec(w_out.shape)]
    args += [wmo, g2.reshape(1, D_MODEL), w_in, w_out]
    final = final_g is not None
    if final:
        in_specs.append(_const_spec((1, D_MODEL)))
        args.append(final_g.reshape(1, D_MODEL))
    kmode = mode if mode == "even" else ("odd2" if len(mix_in) == 2 else "odd1")
    return pl.pallas_call(
        functools.partial(_kb_kernel, mode=kmode, ck=ck, tm=tm, seq_len=seq_len, final=final),
        grid=(nt,),
        in_specs=in_specs,
        out_specs=tok(D_MODEL),
        out_shape=jax.ShapeDtypeStruct((t, D_MODEL), F32),
        scratch_shapes=scratch,
        compiler_params=pltpu.CompilerParams(
            dimension_semantics=("arbitrary",), vmem_limit_bytes=VMEM_LIMIT),
        name="mix_ffn2_" + kmode,
    )(*args)


def _diff_lambda(lam_ref, lam_init):
    lp = lam_ref[...]
    s1 = jnp.sum(lp[0:1] * lp[1:2], axis=-1, keepdims=True)
    s2 = jnp.sum(lp[2:3] * lp[3:4], axis=-1, keepdims=True)
    return jnp.exp(s1) - jnp.exp(s2) + lam_init


def _half_masks():
    lane = lax.broadcasted_iota(jnp.int32, (1, LANES), 1)
    lo = lane < HEAD_DIM
    return lo, jnp.logical_not(lo)


def _softmax_parts(parts):
    m = None
    for s in parts:
        mi = jnp.max(s, axis=-1, keepdims=True)
        m = mi if m is None else jnp.maximum(m, mi)
    es = [jnp.exp2(s - m) for s in parts]
    l = None
    for e in es:
        li = jnp.sum(e, axis=-1, keepdims=True)
        l = li if l is None else l + li
    return es, l


def _diff_out_norm(o, dn_ref, lam_init):
    return (_rms(o) * dn_ref[...]) * (1.0 - lam_init)


def _ctx_attn_kernel(nq_ref, nk_ref, nv_ref, dq_ref, dk_ref, dv_ref, lam_ref, dn_ref, o_ref, *, lam_init):
    lo, hi = _half_masks()
    for hp in range(H_NAT // 2):
        sl = slice(hp * LANES, (hp + 1) * LANES)
        q = nq_ref[:, sl] * Q_SCALE
        k = nk_ref[:, sl].astype(BF16)
        v = nv_ref[:, sl].astype(BF16)
        outs = []
        for msk in (lo, hi):
            qm = jnp.where(msk, q, 0.0).astype(BF16)
            (e,), l = _softmax_parts([_dot_nt(qm, k)])
            outs.append(_dot(e.astype(BF16), v) * (1.0 / l))
        o_ref[:, sl] = jnp.where(lo, outs[0], outs[1]).astype(o_ref.dtype)
    lam = _diff_lambda(lam_ref, lam_init)
    for h in range(H_DIFF):
        sl = slice(h * LANES, (h + 1) * LANES)
        q = dq_ref[:, sl] * Q_SCALE
        k = dk_ref[:, sl].astype(BF16)
        v = dv_ref[:, sl].astype(BF16)
        ps = []
        for msk in (lo, hi):
            qm = jnp.where(msk, q, 0.0).astype(BF16)
            (e,), l = _softmax_parts([_dot_nt(qm, k)])
            ps.append(e * (1.0 / l))
        a = (ps[0] - lam * ps[1]).astype(BF16)
        o = _diff_out_norm(_dot(a, v), dn_ref, lam_init)
        o_ref[:, W_NAT + h * LANES:W_NAT + (h + 1) * LANES] = o.astype(o_ref.dtype)


def _ctx_attention(parts, lam_p, dnorm, seq, lam_init):
    t = parts[0].shape[0]
    nb = t // seq
    blk = pl.BlockSpec((seq, 512), lambda i: (i, 0))
    return pl.pallas_call(
        functools.partial(_ctx_attn_kernel, lam_init=lam_init),
        grid=(nb,),
        in_specs=[blk] * 6 + [_const_spec((4, HEAD_DIM)), _const_spec((1, 2 * HEAD_DIM))],
        out_specs=pl.BlockSpec((seq, D_MODEL), lambda i: (i, 0)),
        out_shape=jax.ShapeDtypeStruct((t, D_MODEL), BF16),
        compiler_params=pltpu.CompilerParams(
            dimension_semantics=("arbitrary",), vmem_limit_bytes=VMEM_LIMIT),
        name="ctx_attention",
    )(*parts, lam_p, dnorm.reshape(1, 2 * HEAD_DIM))


NAT_QROWS = 8
NAT_KROWS = 16
NAT_NDR = 2 * NAT_WIN_R


def _build_bias_tiles(rpb_ref, u_ref):
    c = lax.broadcasted_iota(jnp.int32, (GRID_W, LANES), 0)
    lane = lax.broadcasted_iota(jnp.int32, (GRID_W, LANES), 1)
    kc = jnp.bitwise_and(lane, GRID_W - 1)
    c0 = jnp.clip(c - NAT_WIN_C // 2, 0, GRID_W - NAT_WIN_C)
    col_ok = jnp.logical_and(kc >= c0, kc < c0 + NAT_WIN_C)
    left = lane < GRID_W
    n_dr = 2 * NAT_WIN_R - 1
    for h in range(H_NAT):
        tl, tr = [], []
        for dr in range(n_dr):
            row = jnp.broadcast_to(rpb_ref[h, dr:dr + 1, :] * LOG2E, (GRID_W, LANES))
            tl.append(pltpu.roll(row, LANES - (NAT_WIN_C - 1), 1, stride=1, stride_axis=0))
            tr.append(pltpu.roll(row, LANES - (NAT_WIN_C - 1) + GRID_W, 1, stride=1, stride_axis=0))
        for di in range(NAT_NDR):
            dl = di - NAT_WIN_R + (NAT_WIN_R - 1)
            drr = dl + 1
            lv = tl[dl] if 0 <= dl < n_dr else jnp.full((GRID_W, LANES), NEG, F32)
            rv = tr[drr] if 0 <= drr < n_dr else jnp.full((GRID_W, LANES), NEG, F32)
            u_ref[h, di] = jnp.where(col_ok, jnp.where(left, lv, rv), NEG)


def _nat_lat_kernel(q_ref, k_ref, v_ref, kc_ref, vc_ref, rpb_ref, o_ref, u_ref, s_ref):
    b = pl.program_id(1)

    @pl.when(jnp.logical_and(pl.program_id(0) == 0, b == 0))
    def _():
        _build_bias_tiles(rpb_ref, u_ref)

    rows = k_ref.shape[0] // GRID_W
    kb = jnp.clip(b * NAT_QROWS - NAT_WIN_R // 2, 0, rows - NAT_KROWS)
    koff = pl.multiple_of(kb * GRID_W, GRID_W)
    lo, hi = _half_masks()
    left = lax.broadcasted_iota(jnp.int32, (GRID_W, LANES), 1) < GRID_W
    nq = NAT_QROWS * GRID_W
    for hp in range(H_NAT // 2):
        sl = slice(hp * LANES, (hp + 1) * LANES)
        q = q_ref[:, sl]
        k = k_ref[pl.ds(koff, NAT_KROWS * GRID_W), sl]
        v = v_ref[pl.ds(koff, NAT_KROWS * GRID_W), sl]
        kc = kc_ref[:, sl]
        vc = vc_ref[:, sl]
        outs = []
        for e, msk in enumerate((lo, hi)):
            h = 2 * hp + e
            qm = jnp.where(msk, q, jnp.zeros_like(q))
            s_ref[...] = _dot_nt(qm, k)
            for i in range(NAT_QROWS):
                r = b * NAT_QROWS + i
                r0 = jnp.clip(r - NAT_WIN_R // 2, 0, rows - NAT_WIN_R)
                for jp in range(NAT_KROWS // 2):
                    kl = kb + 2 * jp
                    ok_l = jnp.logical_and(kl >= r0, kl < r0 + NAT_WIN_R)
                    ok_r = jnp.logical_and(kl + 1 >= r0, kl + 1 < r0 + NAT_WIN_R)
                    di = jnp.clip(kl - r + NAT_WIN_R, 0, NAT_NDR - 1)
                    ok = jnp.where(left, ok_l.astype(jnp.int32), ok_r.astype(jnp.int32))
                    bias = jnp.where(ok > 0, u_ref[h, di], NEG)
                    tile = (slice(i * GRID_W, (i + 1) * GRID_W), slice(jp * LANES, (jp + 1) * LANES))
                    s_ref[tile] = s_ref[tile] + bias
            s_ctx = _dot_nt(qm, kc)
            (e_loc, e_ctx), l = _softmax_parts([s_ref[...], s_ctx])
            o = _dot(e_loc.astype(BF16), v) + _dot(e_ctx.astype(BF16), vc)
            outs.append(o * (1.0 / l))
        o_ref[:, sl] = jnp.where(lo, outs[0], outs[1]).astype(o_ref.dtype)


def _nat_lat_attention(q, k, v, kc, vc, rpb_pad, batch, seq):
    nq = NAT_QROWS * GRID_W
    nblk = seq // nq
    p = kc.shape[1]
    return pl.pallas_call(
        _nat_lat_kernel,
        grid=(batch, nblk),
        in_specs=[
            pl.BlockSpec((nq, W_NAT), lambda bb, i: (bb * nblk + i, 0)),
            pl.BlockSpec((seq, W_NAT), lambda bb, i: (bb, 0)),
            pl.BlockSpec((seq, W_NAT), lambda bb, i: (bb, 0)),
            pl.BlockSpec((None, p, W_NAT), lambda bb, i: (bb, 0, 0)),
            pl.BlockSpec((None, p, W_NAT), lambda bb, i: (bb, 0, 0)),
            pl.BlockSpec(rpb_pad.shape, lambda bb, i: (0, 0, 0)),
        ],
        out_specs=pl.BlockSpec((nq, W_NAT), lambda bb, i: (bb * nblk + i, 0)),
        out_shape=jax.ShapeDtypeStruct((batch * seq, W_NAT), BF16),
        scratch_shapes=[
            pltpu.VMEM((H_NAT, NAT_NDR, GRID_W, LANES), F32),
            pltpu.VMEM((nq, NAT_KROWS * GRID_W), F32),
        ],
        compiler_params=pltpu.CompilerParams(
            dimension_semantics=("arbitrary", "arbitrary"), vmem_limit_bytes=VMEM_LIMIT),
        name="nat_latent_attention",
    )(q, k, v, kc, vc, rpb_pad)


DIFF_KCHUNK = 512


def _lane_fold(x, op):
    out = x[:, :LANES]
    for c in range(1, x.shape[1] // LANES):
        out = op(out, x[:, c * LANES:(c + 1) * LANES])
    return out


def _round_robin(*gens):
    gens = list(gens)
    while gens:
        for g in list(gens):
            try:
                next(g)
            except StopIteration:
                gens.remove(g)


def _diff_lat_kernel(q_ref, k_ref, v_ref, kc_ref, vc_ref, lam_ref, dn_ref, o_ref, sa_ref, sb_ref, *, lam_init):
    lo, hi = _half_masks()
    lam = _diff_lambda(lam_ref, lam_init)
    ck = DIFF_KCHUNK
    pieces = ([(k_ref, v_ref, c * ck) for c in range(k_ref.shape[0] // ck)]
              + [(kc_ref, vc_ref, c * ck) for c in range(kc_ref.shape[0] // ck)])
    row_max, row_sum = {}, {}

    def s_of(h):
        return (sa_ref, sb_ref)[h % 2]

    def scores(h):
        sl = slice(h * LANES, (h + 1) * LANES)
        q = q_ref[:, sl]
        ms = []
        for j, msk in enumerate((lo, hi)):
            qm = jnp.where(msk, q, jnp.zeros_like(q))
            mx = None
            for ci, (kr, _, off) in enumerate(pieces):
                s = _dot_nt(qm, kr[off:off + ck, sl])
                s_of(h)[j, :, ci * ck:(ci + 1) * ck] = s
                part = _lane_fold(s, jnp.maximum)
                mx = part if mx is None else jnp.maximum(mx, part)
                yield
            ms.append(jnp.max(mx, axis=-1, keepdims=True))
        row_max[h] = ms

    def exps(h):
        ls = []
        for j in range(2):
            acc = None
            for ci in range(len(pieces)):
                tile = (j, slice(None), slice(ci * ck, (ci + 1) * ck))
                e = jnp.exp2(s_of(h)[tile] - row_max[h][j])
                s_of(h)[tile] = e
                part = _lane_fold(e, jnp.add)
                acc = part if acc is None else acc + part
                yield
            ls.append(jnp.sum(acc, axis=-1, keepdims=True))
        row_sum[h] = ls

    def values(h):
        sl = slice(h * LANES, (h + 1) * LANES)
        l0, l1 = row_sum[h]
        cf = lam * l0 * (1.0 / l1)
        o = None
        for ci, (_, vr, off) in enumerate(pieces):
            cols = slice(ci * ck, (ci + 1) * ck)
            a = (s_of(h)[0, :, cols] - cf * s_of(h)[1, :, cols]).astype(BF16)
            t = _dot(a, vr[off:off + ck, sl])
            o = t if o is None else o + t
            yield
        o = o * (1.0 / l0)
        o_ref[:, sl] = _diff_out_norm(o, dn_ref, lam_init).astype(o_ref.dtype)

    _round_robin(scores(0))
    for h in range(H_DIFF):
        nxt = [scores(h + 1)] if h + 1 < H_DIFF else []
        _round_robin(exps(h), *nxt)
        _round_robin(values(h))


def _diff_lat_attention(q, k, v, kc, vc, lam_p, dnorm, batch, seq, lam_init, tq=256):
    nqb = seq // tq
    p = kc.shape[1]
    s_scratch = pltpu.VMEM((2, tq, seq + p), F32)
    return pl.pallas_call(
        functools.partial(_diff_lat_kernel, lam_init=lam_init),
        grid=(batch, nqb),
        in_specs=[
            pl.BlockSpec((tq, W_DIFF), lambda bb, i: (bb * nqb + i, 0)),
            pl.BlockSpec((seq, W_DIFF), lambda bb, i: (bb, 0)),
            pl.BlockSpec((seq, W_DIFF), lambda bb, i: (bb, 0)),
            pl.BlockSpec((None, p, W_DIFF), lambda bb, i: (bb, 0, 0)),
            pl.BlockSpec((None, p, W_DIFF), lambda bb, i: (bb, 0, 0)),
            pl.BlockSpec((4, HEAD_DIM), lambda bb, i: (0, 0)),
            pl.BlockSpec((1, 2 * HEAD_DIM), lambda bb, i: (0, 0)),
        ],
        out_specs=pl.BlockSpec((tq, W_DIFF), lambda bb, i: (bb * nqb + i, 0)),
        out_shape=jax.ShapeDtypeStruct((batch * seq, W_DIFF), BF16),
        scratch_shapes=[s_scratch, s_scratch],
        compiler_params=pltpu.CompilerParams(
            dimension_semantics=("arbitrary", "arbitrary"), vmem_limit_bytes=VMEM_LIMIT),
        name="diff_latent_attention",
    )(q, k, v, kc, vc, lam_p, dnorm.reshape(1, 2 * HEAD_DIM))


def _rope_tables(n):
    t = np.arange(n)
    row = (t // GRID_W).astype(np.float64)
    col = (t % GRID_W).astype(np.float64)
    quarter = HEAD_DIM // 4
    inv = 1.0 / (ROPE_THETA ** (np.arange(quarter) / quarter))
    cr, sr = np.cos(row[:, None] * inv[None]), np.sin(row[:, None] * inv[None])
    cc, sc = np.cos(col[:, None] * inv[None]), np.sin(col[:, None] * inv[None])
    zero = np.zeros_like(sr)
    cos_t = np.concatenate([cr, cr, cc, cc], axis=1)
    sin_a = np.concatenate([-sr, zero, -sc, zero], axis=1)
    sin_b = np.concatenate([zero, sr, zero, sc], axis=1)
    tile = lambda a: jnp.asarray(np.concatenate([a, a], axis=1).astype(np.float32))
    return tile(cos_t), tile(sin_a), tile(sin_b)


def kernel(x_prompt, x_sample, cache_nat_k, cache_nat_v, cache_diff_k, cache_diff_v, c, c_ctx, mod_w, mod_b, norm_ffn1, ffn1_w_in, ffn1_w_out, norm_mix, even_w_in, pool_w, pool_scale, conv_w, odd_w_in, nat_rpb, diff_lambda, diff_norm, mix_w_out, norm_ffn2, ffn2_w_in, ffn2_w_out, final_norm):
    batch, seq, _ = x_prompt.shape
    dbatch, dseq, _ = x_sample.shape
    past = cache_nat_k.shape[2]
    tm = 512

    cond = jnp.zeros((SUBLANES, D_MODEL), F32).at[0].set(c_ctx).at[1:1 + dbatch].set(c)
    mods = _adaln_mods(cond, mod_w, mod_b).reshape(DEPTH, SUBLANES, N_MOD, D_MODEL)

    ctx_row = lambda i: 0
    lat_row = lambda i: 1 + (i * tm) // dseq

    bf = lambda w: [w[i].astype(BF16) for i in range(w.shape[0])]
    f1_in, f1_out, f2_in, f2_out = bf(ffn1_w_in), bf(ffn1_w_out), bf(ffn2_w_in), bf(ffn2_w_out)
    wmo = bf(mix_w_out)
    w_even, w_odd, w_pool = bf(even_w_in), bf(odd_w_in), bf(pool_w)
    cache_bf = lambda a, w: a[:, 0].reshape(dbatch, past, w).astype(BF16)

    ctx = x_prompt.reshape(batch * seq, D_MODEL)
    lat = x_sample.reshape(dbatch * dseq, D_MODEL)

    l = 0
    streams = []
    for x, row_fn, n in ((ctx, ctx_row, seq), (lat, lat_row, dseq)):
        x1, u = _kernel_a(x, mods[l], row_fn, norm_ffn1[l], f1_in[l], f1_out[l], norm_mix[l], w_even[0],
                          "even", tm)
        x2 = _kernel_b(x1, mods[l], row_fn, [u], (w_pool[0], pool_scale[0], conv_w[0]), wmo[l],
                       norm_ffn2[l], f2_in[l], f2_out[l], "even", tm, seq_len=n)
        streams.append(x2)
    ctx, lat = streams

    l = 1
    lam_init = 0.8 - 0.6 * math.exp(-0.3 * l)
    ctx_parts = _kernel_a(ctx, mods[l], ctx_row, norm_ffn1[l], f1_in[l], f1_out[l], norm_mix[l], w_odd[0],
                          "ctx", tm)
    ctx1, (nq, nk, nv, dq, dk, dv) = ctx_parts[0], ctx_parts[1:]
    o_ctx = _ctx_attention([nq, nk, nv, dq, dk, dv], diff_lambda[0], diff_norm[0], seq, lam_init)
    y_prompt = _kernel_b(ctx1, mods[l], ctx_row, [o_ctx], None, wmo[l], norm_ffn2[l], f2_in[l], f2_out[l],
                         "odd", tm, final_g=final_norm)

    lat_parts = _kernel_a(lat, mods[l], lat_row, norm_ffn1[l], f1_in[l], f1_out[l], norm_mix[l], w_odd[0],
                          "lat", tm, rope_tabs=_rope_tables(dseq))
    lat1, (lq, lk, lv, ldq, ldk, ldv) = lat_parts[0], lat_parts[1:]
    rpb_pad = jnp.zeros((H_NAT, 2 * NAT_WIN_R, LANES), F32).at[:, :2 * NAT_WIN_R - 1,
                                                                :2 * NAT_WIN_C - 1].set(nat_rpb[0])
    o_nat = _nat_lat_attention(lq, lk, lv, cache_bf(cache_nat_k, W_NAT), cache_bf(cache_nat_v, W_NAT),
                               rpb_pad, dbatch, dseq)
    o_diff = _diff_lat_attention(ldq, ldk, ldv, cache_bf(cache_diff_k, W_DIFF), cache_bf(cache_diff_v, W_DIFF),
                                 diff_lambda[0], diff_norm[0], dbatch, dseq, lam_init)
    y_sample = _kernel_b(lat1, mods[l], lat_row, [o_nat, o_diff], None, wmo[l], norm_ffn2[l], f2_in[l],
                         f2_out[l], "odd", tm, final_g=final_norm)

    new_nat_k = nk.reshape(batch, 1, seq, H_NAT, HEAD_DIM)
    new_nat_v = nv.reshape(batch, 1, seq, H_NAT, HEAD_DIM)
    new_diff_k = dk.reshape(batch, 1, seq, H_DIFF, 2, HEAD_DIM)
    new_diff_v = dv.reshape(batch, 1, seq, H_DIFF, 2 * HEAD_DIM)
    return (y_prompt.reshape(batch, seq, D_MODEL), y_sample.reshape(dbatch, dseq, D_MODEL),
            new_nat_k, new_nat_v, new_diff_k, new_diff_v)
```

```python
import functools
import math

import numpy as np
import jax
import jax.numpy as jnp
from jax import lax
from jax.experimental import pallas as pl
from jax.experimental.pallas import tpu as pltpu

D_MODEL = 1024
D_FF = 2816
N_MOD = 9
DEPTH = 2
GRID_W = 64
HEAD_DIM = 64
W_POOL = 512
W_CONV = 512
POOL_WINDOWS = (2, 4, 8, 16)
POOL_G = 128
W_NAT = 512
W_DIFF = 512
H_NAT = 8
H_DIFF = 4
NAT_WIN_R = 8
NAT_WIN_C = 16
ROPE_THETA = 10000.0
ATTN_SCALE = HEAD_DIM ** -0.5
LOG2E = math.log2(math.e)
Q_SCALE = ATTN_SCALE * LOG2E
EPS = 1e-6
NEG = -1e30

LANES = 128
SUBLANES = 8
VMEM_LIMIT = 60 * 1024 * 1024

BF16 = jnp.bfloat16
F32 = jnp.float32


def _dot(a, b):
    return jnp.dot(a, b, preferred_element_type=F32)


def _dot_nt(a, b):
    return lax.dot_general(a, b, (((1,), (1,)), ((), ())), preferred_element_type=F32)


def _sigmoid(x):
    return 1.0 / (1.0 + jnp.exp(-x))


def _rms(x):
    return x * lax.rsqrt(jnp.mean(x * x, axis=-1, keepdims=True) + EPS)


def _modnorm(x, shift, scale, g):
    return (_rms(x) * g) * (1.0 + scale) + shift


def _ffn_half_step(x, mods, base, g, w_in_ref, w_out_ref, ck):
    shift, scale, gate = mods[base:base + 1], mods[base + 1:base + 2], mods[base + 2:base + 3]
    h = _modnorm(x, shift, scale, g).astype(BF16)
    acc = None
    for c in range(D_FF // ck):
        a = _dot(h, w_in_ref[:, c * ck:(c + 1) * ck])
        b = _dot(h, w_in_ref[:, D_FF + c * ck:D_FF + (c + 1) * ck])
        gated = ((a * _sigmoid(a)) * b).astype(BF16)
        t = _dot(gated, w_out_ref[c * ck:(c + 1) * ck, :])
        acc = t if acc is None else acc + t
    return x + (0.5 * gate) * acc


def _const_spec(shape):
    nd = len(shape)
    return pl.BlockSpec(shape, lambda i, _nd=nd: (0,) * _nd, pipeline_mode=pl.Buffered(1))


def _mods_kernel(cond_ref, w_ref, b_ref, o_ref):
    cnd = cond_ref[...]
    s = (cnd * _sigmoid(cnd)).astype(BF16)
    o_ref[0] = _dot(s, w_ref[0].astype(BF16)) + b_ref[0]


def _adaln_mods(cond, mod_w, mod_b):
    tn = 1152
    nt = (N_MOD * D_MODEL) // tn
    return pl.pallas_call(
        _mods_kernel,
        grid=(DEPTH, nt),
        in_specs=[
            pl.BlockSpec((SUBLANES, D_MODEL), lambda l, j: (0, 0)),
            pl.BlockSpec((1, D_MODEL, tn), lambda l, j: (l, 0, j)),
            pl.BlockSpec((1, 1, tn), lambda l, j: (l, 0, j)),
        ],
        out_specs=pl.BlockSpec((1, SUBLANES, tn), lambda l, j: (l, 0, j)),
        out_shape=jax.ShapeDtypeStruct((DEPTH, SUBLANES, N_MOD * D_MODEL), F32),
        compiler_params=pltpu.CompilerParams(vmem_limit_bytes=VMEM_LIMIT),
        name="adaln_mods",
    )(cond, mod_w, mod_b.reshape(DEPTH, 1, N_MOD * D_MODEL))


def _rope(x, cos_t, sin_a, sin_b):
    w = x.shape[1]
    reps = w // LANES
    c = jnp.concatenate([cos_t] * reps, axis=1)
    sa = jnp.concatenate([sin_a] * reps, axis=1)
    sb = jnp.concatenate([sin_b] * reps, axis=1)
    quarter = HEAD_DIM // 4
    up = pltpu.roll(x, w - quarter, 1)
    dn = pltpu.roll(x, quarter, 1)
    return x * c + up * sa + dn * sb


def _ka_kernel(*refs, mode, ck, seq):
    n_in = {"even": 7, "ctx": 8, "lat": 10}[mode]
    x_ref, mods_ref, g1_ref, w_in_ref, w_out_ref, gm_ref, wp_ref = refs[:7]
    outs = refs[n_in:]
    mods = mods_ref[0]
    x1 = _ffn_half_step(x_ref[...], mods, 0, g1_ref[...], w_in_ref, w_out_ref, ck)
    outs[0][...] = x1
    h = _modnorm(x1, mods[3:4], mods[4:5], gm_ref[...]).astype(BF16)
    if mode == "even":
        outs[1][...] = _dot(h, wp_ref[...])
        return
    if mode == "lat":
        cos_ref, sa_ref, sb_ref = refs[7:10]
    else:
        wpt_ref = refs[7]
    for p in range(6):
        if mode == "ctx" and p in (1, 2, 4):
            ut = _dot_nt(wpt_ref[p * 512:(p + 1) * 512, :], h)
            for bb in range(ut.shape[1] // seq):
                outs[1 + p][bb] = ut[:, bb * seq:(bb + 1) * seq]
            continue
        u = _dot(h, wp_ref[:, p * 512:(p + 1) * 512])
        if mode == "lat" and p in (3, 4):
            u = _rope(u, cos_ref[...], sa_ref[...], sb_ref[...])
        if p in (0, 3):
            u = u * Q_SCALE
        outs[1 + p][...] = u.astype(outs[1 + p].dtype)


def _layer_spec(w, l):
    nd = w.ndim - 1
    return pl.BlockSpec((None,) + w.shape[1:], lambda i, _l=l, _nd=nd: (_l,) + (0,) * _nd,
                        pipeline_mode=pl.Buffered(1))


def _kernel_a(x, mods_l, row_fn, l, g1, w_in, w_out, gm, wp, mode, tm, rope_tabs=None, wpt=None, seq=None,
              ck=256):
    t = x.shape[0]
    nt = t // tm
    tok = lambda w: pl.BlockSpec((tm, w), lambda i: (i, 0))
    in_specs = [
        tok(D_MODEL),
        pl.BlockSpec((1, N_MOD, D_MODEL), lambda i: (row_fn(i), 0, 0)),
        _layer_spec(g1, l), _layer_spec(w_in, l), _layer_spec(w_out, l), _layer_spec(gm, l), _layer_spec(wp, 0),
    ]
    args = [x, mods_l, g1, w_in, w_out, gm, wp]
    n_proj = wp.shape[2]
    out_shape = [jax.ShapeDtypeStruct((t, D_MODEL), F32)]
    out_specs = [tok(D_MODEL)]
    if mode == "even":
        out_shape.append(jax.ShapeDtypeStruct((t, n_proj), F32))
        out_specs.append(tok(n_proj))
    elif mode == "lat":
        seq_tiles = rope_tabs[0].shape[0] // tm
        for tab in rope_tabs:
            in_specs.append(pl.BlockSpec((tm, LANES), lambda i, _s=seq_tiles: (i % _s, 0)))
            args.append(tab)
        for _ in range(6):
            out_shape.append(jax.ShapeDtypeStruct((t, 512), BF16))
            out_specs.append(tok(512))
    else:
        in_specs.append(_layer_spec(wpt, 0))
        args.append(wpt)
        spt = tm // seq
        for p in range(6):
            if p in (1, 2, 4):
                out_shape.append(jax.ShapeDtypeStruct((t // seq, 512, seq), F32))
                out_specs.append(pl.BlockSpec((spt, 512, seq), lambda i: (i, 0, 0)))
            else:
                out_shape.append(jax.ShapeDtypeStruct((t, 512), BF16 if p in (0, 3) else F32))
                out_specs.append(tok(512))
    return pl.pallas_call(
        functools.partial(_ka_kernel, mode=mode, ck=ck, seq=seq),
        grid=(nt,),
        in_specs=in_specs,
        out_specs=out_specs,
        out_shape=out_shape,
        compiler_params=pltpu.CompilerParams(
            dimension_semantics=("arbitrary",), vmem_limit_bytes=VMEM_LIMIT),
        name="ffn1_proj_" + mode,
    )(*args)


HALO = 8


def _even_mix(ext_ref, pos, seq_len, tm, pw_ref, ps_ref, cw_ref):
    def rows(j, lo, hi):
        return ext_ref[HALO + j:HALO + j + tm, lo:hi]

    def valid(j):
        p = pos + j
        return jnp.logical_and(p >= 0, p < seq_len)

    ya = []
    for g, win in enumerate(POOL_WINDOWS):
        half = win // 2
        lo_c, hi_c = g * POOL_G, (g + 1) * POOL_G
        s = rows(0, lo_c, hi_c)
        for j in range(-half, half):
            if j == 0:
                continue
            s = s + jnp.where(valid(j), rows(j, lo_c, hi_c), 0.0)
        lo = jnp.clip(pos - half, 0, seq_len - 1)
        hi = jnp.clip(pos + half - 1, 0, seq_len - 1)
        cnt = (hi - lo + 1).astype(F32)
        d = (s / cnt - rows(0, lo_c, hi_c)).astype(BF16)
        ya.append(_dot(d, pw_ref[g]))
    ya = jnp.concatenate(ya, axis=1) * ps_ref[...]

    def z(j):
        return rows(j, W_POOL + 2 * W_CONV, W_POOL + 3 * W_CONV) * rows(j, W_POOL, W_POOL + W_CONV)

    vm = jnp.concatenate([valid(-1)] * (W_CONV // LANES), axis=1)
    vp = jnp.concatenate([valid(1)] * (W_CONV // LANES), axis=1)
    y = (jnp.where(vm, z(-1), 0.0) * cw_ref[0:1, :] + z(0) * cw_ref[1:2, :]
         + jnp.where(vp, z(1), 0.0) * cw_ref[2:3, :])
    yb = rows(0, W_POOL + W_CONV, W_POOL + 2 * W_CONV) * y
    return jnp.concatenate([ya, yb], axis=1)


def _kb_kernel(*refs, mode, ck, tm, seq_len, final):
    x_ref, mods_ref = refs[0], refs[1]
    k = 2
    if mode == "even":
        u_ref, up_ref, un_ref, pw_ref, ps_ref, cw_ref = refs[k:k + 6]
        k += 6
    else:
        n_parts = 2 if mode == "odd2" else 1
        o_refs = refs[k:k + n_parts]
        k += n_parts
    wmo_ref, g2_ref, w_in_ref, w_out_ref = refs[k:k + 4]
    k += 4
    if final:
        gf_ref = refs[k]
        k += 1
    out_ref = refs[k]
    k += 1
    mods = mods_ref[0]
    if mode == "even":
        ext_ref = refs[k]
        ext_ref[0:HALO, :] = up_ref[...]
        ext_ref[HALO:HALO + tm, :] = u_ref[...]
        ext_ref[HALO + tm:HALO + tm + HALO, :] = un_ref[...]
        row = lax.broadcasted_iota(jnp.int32, (tm, LANES), 0) + pl.program_id(0) * tm
        pos = jnp.bitwise_and(row, seq_len - 1)
        feat = _even_mix(ext_ref, pos, seq_len, tm, pw_ref, ps_ref, cw_ref).astype(BF16)
        y = _dot(feat, wmo_ref[...])
    else:
        y = None
        off = 0
        for o_ref in o_refs:
            w = o_ref.shape[1]
            t = _dot(o_ref[...], wmo_ref[off:off + w, :])
            y = t if y is None else y + t
            off += w
    x2 = x_ref[...] + mods[5:6] * y
    x3 = _ffn_half_step(x2, mods, 6, g2_ref[...], w_in_ref, w_out_ref, ck)
    if final:
        x3 = _rms(x3) * gf_ref[...]
    out_ref[...] = x3


def _kernel_b(x, mods_l, row_fn, l, mix_in, mix_params, wmo, g2, w_in, w_out, mode, tm, seq_len=None,
              final_g=None, ck=256):
    t = x.shape[0]
    nt = t // tm
    tok = lambda w: pl.BlockSpec((tm, w), lambda i: (i, 0))
    in_specs = [tok(D_MODEL), pl.BlockSpec((1, N_MOD, D_MODEL), lambda i: (row_fn(i), 0, 0))]
    args = [x, mods_l]
    scratch = []
    if mode == "even":
        u = mix_in[0]
        wu = u.shape[1]
        hb = tm // HALO
        last = t // HALO - 1
        in_specs += [
            tok(wu),
            pl.BlockSpec((HALO, wu), lambda i: (jnp.maximum(i * hb - 1, 0), 0)),
            pl.BlockSpec((HALO, wu), lambda i: (jnp.minimum((i + 1) * hb, last), 0)),
        ]
        args += [u, u, u]
        for w in mix_params:
            in_specs.append(_layer_spec(w, 0))
            args.append(w)
        scratch.append(pltpu.VMEM((tm + 2 * HALO, wu), F32))
    else:
        for o in mix_in:
            in_specs.append(tok(o.shape[1]))
            args.append(o)
    in_specs += [_layer_spec(wmo, l), _layer_spec(g2, l), _layer_spec(w_in, l), _layer_spec(w_out, l)]
    args += [wmo, g2, w_in, w_out]
    final = final_g is not None
    if final:
        in_specs.append(_const_spec((1, D_MODEL)))
        args.append(final_g.reshape(1, D_MODEL))
    kmode = mode if mode == "even" else ("odd2" if len(mix_in) == 2 else "odd1")
    return pl.pallas_call(
        functools.partial(_kb_kernel, mode=kmode, ck=ck, tm=tm, seq_len=seq_len, final=final),
        grid=(nt,),
        in_specs=in_specs,
        out_specs=tok(D_MODEL),
        out_shape=jax.ShapeDtypeStruct((t, D_MODEL), F32),
        scratch_shapes=scratch,
        compiler_params=pltpu.CompilerParams(
            dimension_semantics=("arbitrary",), vmem_limit_bytes=VMEM_LIMIT),
        name="mix_ffn2_" + kmode,
    )(*args)


def _diff_lambda(lam_ref, lam_init):
    lp = lam_ref[...]
    s1 = jnp.sum(lp[0:1] * lp[1:2], axis=-1, keepdims=True)
    s2 = jnp.sum(lp[2:3] * lp[3:4], axis=-1, keepdims=True)
    return jnp.exp(s1) - jnp.exp(s2) + lam_init


def _half_masks():
    lane = lax.broadcasted_iota(jnp.int32, (1, LANES), 1)
    lo = lane < HEAD_DIM
    return lo, jnp.logical_not(lo)


def _softmax_parts(parts):
    m = None
    for s in parts:
        mi = jnp.max(s, axis=-1, keepdims=True)
        m = mi if m is None else jnp.maximum(m, mi)
    es = [jnp.exp2(s - m) for s in parts]
    l = None
    for e in es:
        li = jnp.sum(e, axis=-1, keepdims=True)
        l = li if l is None else l + li
    return es, l


def _diff_out_norm(o, dn_ref, lam_init):
    return (_rms(o) * dn_ref[...]) * (1.0 - lam_init)


def _ctx_attn_kernel(nq_ref, nkt_ref, nvt_ref, dq_ref, dkt_ref, dv_ref, lam_ref, dn_ref, o_ref, *, lam_init):
    lo, hi = _half_masks()
    for hp in range(H_NAT // 2):
        sl = slice(hp * LANES, (hp + 1) * LANES)
        q = nq_ref[:, sl]
        kt = nkt_ref[sl, :].astype(BF16)
        vt = nvt_ref[sl, :].astype(BF16)
        outs = []
        for msk in (lo, hi):
            qm = jnp.where(msk, q, jnp.zeros_like(q))
            (e,), l = _softmax_parts([_dot(qm, kt)])
            outs.append(_dot_nt(e.astype(BF16), vt) * (1.0 / l))
        o_ref[:, sl] = jnp.where(lo, outs[0], outs[1]).astype(o_ref.dtype)
    lam = _diff_lambda(lam_ref, lam_init)
    for h in range(H_DIFF):
        sl = slice(h * LANES, (h + 1) * LANES)
        q = dq_ref[:, sl]
        kt = dkt_ref[sl, :].astype(BF16)
        v = dv_ref[:, sl].astype(BF16)
        ps = []
        for msk in (lo, hi):
            qm = jnp.where(msk, q, jnp.zeros_like(q))
            (e,), l = _softmax_parts([_dot(qm, kt)])
            ps.append(e * (1.0 / l))
        a = (ps[0] - lam * ps[1]).astype(BF16)
        o = _diff_out_norm(_dot(a, v), dn_ref, lam_init)
        o_ref[:, W_NAT + h * LANES:W_NAT + (h + 1) * LANES] = o.astype(o_ref.dtype)


def _ctx_attention(parts, lam_p, dnorm, seq, lam_init):
    nq, nkt, nvt, dq, dkt, dv = parts
    t = nq.shape[0]
    nb = t // seq
    tok = pl.BlockSpec((seq, 512), lambda i: (i, 0))
    fmaj = pl.BlockSpec((None, 512, seq), lambda i: (i, 0, 0))
    return pl.pallas_call(
        functools.partial(_ctx_attn_kernel, lam_init=lam_init),
        grid=(nb,),
        in_specs=[tok, fmaj, fmaj, tok, fmaj, tok, _const_spec((4, HEAD_DIM)), _const_spec((1, 2 * HEAD_DIM))],
        out_specs=pl.BlockSpec((seq, D_MODEL), lambda i: (i, 0)),
        out_shape=jax.ShapeDtypeStruct((t, D_MODEL), BF16),
        compiler_params=pltpu.CompilerParams(
            dimension_semantics=("arbitrary",), vmem_limit_bytes=VMEM_LIMIT),
        name="ctx_attention",
    )(nq, nkt, nvt, dq, dkt, dv, lam_p, dnorm.reshape(1, 2 * HEAD_DIM))


NAT_QROWS = 8
NAT_KROWS = 16
NAT_NDR = 2 * NAT_WIN_R


def _build_bias_tiles(rpb_ref, u_ref):
    c = lax.broadcasted_iota(jnp.int32, (GRID_W, LANES), 0)
    lane = lax.broadcasted_iota(jnp.int32, (GRID_W, LANES), 1)
    kc = jnp.bitwise_and(lane, GRID_W - 1)
    c0 = jnp.clip(c - NAT_WIN_C // 2, 0, GRID_W - NAT_WIN_C)
    col_ok = jnp.logical_and(kc >= c0, kc < c0 + NAT_WIN_C)
    left = lane < GRID_W
    n_dr = 2 * NAT_WIN_R - 1
    for h in range(H_NAT):
        tl, tr = [], []
        for dr in range(n_dr):
            row = jnp.broadcast_to(rpb_ref[h, dr:dr + 1, :] * LOG2E, (GRID_W, LANES))
            tl.append(pltpu.roll(row, LANES - (NAT_WIN_C - 1), 1, stride=1, stride_axis=0))
            tr.append(pltpu.roll(row, LANES - (NAT_WIN_C - 1) + GRID_W, 1, stride=1, stride_axis=0))
        for di in range(NAT_NDR):
            dl = di - NAT_WIN_R + (NAT_WIN_R - 1)
            drr = dl + 1
            lv = tl[dl] if 0 <= dl < n_dr else jnp.full((GRID_W, LANES), NEG, F32)
            rv = tr[drr] if 0 <= drr < n_dr else jnp.full((GRID_W, LANES), NEG, F32)
            u_ref[h, di] = jnp.where(col_ok, jnp.where(left, lv, rv), NEG)


def _nat_lat_kernel(q_ref, k_ref, v_ref, kc_ref, vc_ref, rpb_ref, o_ref, u_ref, s_ref):
    b = pl.program_id(1)

    @pl.when(jnp.logical_and(pl.program_id(0) == 0, b == 0))
    def _():
        _build_bias_tiles(rpb_ref, u_ref)

    rows = k_ref.shape[0] // GRID_W
    kb = jnp.clip(b * NAT_QROWS - NAT_WIN_R // 2, 0, rows - NAT_KROWS)
    koff = pl.multiple_of(kb * GRID_W, GRID_W)
    lo, hi = _half_masks()
    left = lax.broadcasted_iota(jnp.int32, (GRID_W, LANES), 1) < GRID_W
    nq = NAT_QROWS * GRID_W
    for hp in range(H_NAT // 2):
        sl = slice(hp * LANES, (hp + 1) * LANES)
        q = q_ref[:, sl]
        k = k_ref[pl.ds(koff, NAT_KROWS * GRID_W), sl]
        v = v_ref[pl.ds(koff, NAT_KROWS * GRID_W), sl]
        kc = kc_ref[:, sl]
        vc = vc_ref[:, sl]
        outs = []
        for e, msk in enumerate((lo, hi)):
            h = 2 * hp + e
            qm = jnp.where(msk, q, jnp.zeros_like(q))
            s_ref[...] = _dot_nt(qm, k)
            for i in range(NAT_QROWS):
                r = b * NAT_QROWS + i
                r0 = jnp.clip(r - NAT_WIN_R // 2, 0, rows - NAT_WIN_R)
                for jp in range(NAT_KROWS // 2):
                    kl = kb + 2 * jp
                    ok_l = jnp.logical_and(kl >= r0, kl < r0 + NAT_WIN_R)
                    ok_r = jnp.logical_and(kl + 1 >= r0, kl + 1 < r0 + NAT_WIN_R)
                    di = jnp.clip(kl - r + NAT_WIN_R, 0, NAT_NDR - 1)
                    ok = jnp.where(left, ok_l.astype(jnp.int32), ok_r.astype(jnp.int32))
                    bias = jnp.where(ok > 0, u_ref[h, di], NEG)
                    tile = (slice(i * GRID_W, (i + 1) * GRID_W), slice(jp * LANES, (jp + 1) * LANES))
                    s_ref[tile] = s_ref[tile] + bias
            s_ctx = _dot_nt(qm, kc)
            (e_loc, e_ctx), l = _softmax_parts([s_ref[...], s_ctx])
            o = _dot(e_loc.astype(BF16), v) + _dot(e_ctx.astype(BF16), vc)
            outs.append(o * (1.0 / l))
        o_ref[:, sl] = jnp.where(lo, outs[0], outs[1]).astype(o_ref.dtype)


def _nat_lat_attention(q, k, v, kc, vc, rpb_pad, batch, seq):
    nq = NAT_QROWS * GRID_W
    nblk = seq // nq
    p = kc.shape[1]
    return pl.pallas_call(
        _nat_lat_kernel,
        grid=(batch, nblk),
        in_specs=[
            pl.BlockSpec((nq, W_NAT), lambda bb, i: (bb * nblk + i, 0)),
            pl.BlockSpec((seq, W_NAT), lambda bb, i: (bb, 0)),
            pl.BlockSpec((seq, W_NAT), lambda bb, i: (bb, 0)),
            pl.BlockSpec((None, p, W_NAT), lambda bb, i: (bb, 0, 0)),
            pl.BlockSpec((None, p, W_NAT), lambda bb, i: (bb, 0, 0)),
            pl.BlockSpec(rpb_pad.shape, lambda bb, i: (0, 0, 0)),
        ],
        out_specs=pl.BlockSpec((nq, W_NAT), lambda bb, i: (bb * nblk + i, 0)),
        out_shape=jax.ShapeDtypeStruct((batch * seq, W_NAT), BF16),
        scratch_shapes=[
            pltpu.VMEM((H_NAT, NAT_NDR, GRID_W, LANES), F32),
            pltpu.VMEM((nq, NAT_KROWS * GRID_W), F32),
        ],
        compiler_params=pltpu.CompilerParams(
            dimension_semantics=("arbitrary", "arbitrary"), vmem_limit_bytes=VMEM_LIMIT),
        name="nat_latent_attention",
    )(q, k, v, kc, vc, rpb_pad)


DIFF_KCHUNK = 512


def _lane_fold(x, op):
    out = x[:, :LANES]
    for c in range(1, x.shape[1] // LANES):
        out = op(out, x[:, c * LANES:(c + 1) * LANES])
    return out


def _round_robin(*gens):
    gens = list(gens)
    while gens:
        for g in list(gens):
            try:
                next(g)
            except StopIteration:
                gens.remove(g)


def _diff_lat_kernel(q_ref, k_ref, v_ref, kc_ref, vc_ref, lam_ref, dn_ref, o_ref, sa_ref, sb_ref, *, lam_init):
    lo, hi = _half_masks()
    lam = _diff_lambda(lam_ref, lam_init)
    ck = DIFF_KCHUNK
    pieces = ([(k_ref, v_ref, c * ck) for c in range(k_ref.shape[0] // ck)]
              + [(kc_ref, vc_ref, c * ck) for c in range(kc_ref.shape[0] // ck)])
    row_max, row_sum = {}, {}

    def s_of(h):
        return (sa_ref, sb_ref)[h % 2]

    def scores(h):
        sl = slice(h * LANES, (h + 1) * LANES)
        q = q_ref[:, sl]
        ms = []
        for j, msk in enumerate((lo, hi)):
            qm = jnp.where(msk, q, jnp.zeros_like(q))
            mx = None
            for ci, (kr, _, off) in enumerate(pieces):
                s = _dot_nt(qm, kr[off:off + ck, sl])
                s_of(h)[j, :, ci * ck:(ci + 1) * ck] = s
                part = _lane_fold(s, jnp.maximum)
                mx = part if mx is None else jnp.maximum(mx, part)
                yield
            ms.append(jnp.max(mx, axis=-1, keepdims=True))
        row_max[h] = ms

    def exps(h):
        ls = []
        for j in range(2):
            acc = None
            for ci in range(len(pieces)):
                tile = (j, slice(None), slice(ci * ck, (ci + 1) * ck))
                e = jnp.exp2(s_of(h)[tile] - row_max[h][j])
                s_of(h)[tile] = e
                part = _lane_fold(e, jnp.add)
                acc = part if acc is None else acc + part
                yield
            ls.append(jnp.sum(acc, axis=-1, keepdims=True))
        row_sum[h] = ls

    def values(h):
        sl = slice(h * LANES, (h + 1) * LANES)
        l0, l1 = row_sum[h]
        cf = lam * l0 * (1.0 / l1)
        o = None
        for ci, (_, vr, off) in enumerate(pieces):
            cols = slice(ci * ck, (ci + 1) * ck)
            a = (s_of(h)[0, :, cols] - cf * s_of(h)[1, :, cols]).astype(BF16)
            t = _dot(a, vr[off:off + ck, sl])
            o = t if o is None else o + t
            yield
        o = o * (1.0 / l0)
        o_ref[:, sl] = _diff_out_norm(o, dn_ref, lam_init).astype(o_ref.dtype)

    _round_robin(scores(0))
    for h in range(H_DIFF):
        nxt = [scores(h + 1)] if h + 1 < H_DIFF else []
        _round_robin(exps(h), *nxt)
        _round_robin(values(h))


def _diff_lat_attention(q, k, v, kc, vc, lam_p, dnorm, batch, seq, lam_init, tq=256):
    nqb = seq // tq
    p = kc.shape[1]
    s_scratch = pltpu.VMEM((2, tq, seq + p), F32)
    return pl.pallas_call(
        functools.partial(_diff_lat_kernel, lam_init=lam_init),
        grid=(batch, nqb),
        in_specs=[
            pl.BlockSpec((tq, W_DIFF), lambda bb, i: (bb * nqb + i, 0)),
            pl.BlockSpec((seq, W_DIFF), lambda bb, i: (bb, 0)),
            pl.BlockSpec((seq, W_DIFF), lambda bb, i: (bb, 0)),
            pl.BlockSpec((None, p, W_DIFF), lambda bb, i: (bb, 0, 0)),
            pl.BlockSpec((None, p, W_DIFF), lambda bb, i: (bb, 0, 0)),
            pl.BlockSpec((4, HEAD_DIM), lambda bb, i: (0, 0)),
            pl.BlockSpec((1, 2 * HEAD_DIM), lambda bb, i: (0, 0)),
        ],
        out_specs=pl.BlockSpec((tq, W_DIFF), lambda bb, i: (bb * nqb + i, 0)),
        out_shape=jax.ShapeDtypeStruct((batch * seq, W_DIFF), BF16),
        scratch_shapes=[s_scratch, s_scratch],
        compiler_params=pltpu.CompilerParams(
            dimension_semantics=("arbitrary", "arbitrary"), vmem_limit_bytes=VMEM_LIMIT),
        name="diff_latent_attention",
    )(q, k, v, kc, vc, lam_p, dnorm.reshape(1, 2 * HEAD_DIM))


def _rope_tables(n):
    t = np.arange(n)
    row = (t // GRID_W).astype(np.float64)
    col = (t % GRID_W).astype(np.float64)
    quarter = HEAD_DIM // 4
    inv = 1.0 / (ROPE_THETA ** (np.arange(quarter) / quarter))
    cr, sr = np.cos(row[:, None] * inv[None]), np.sin(row[:, None] * inv[None])
    cc, sc = np.cos(col[:, None] * inv[None]), np.sin(col[:, None] * inv[None])
    zero = np.zeros_like(sr)
    cos_t = np.concatenate([cr, cr, cc, cc], axis=1)
    sin_a = np.concatenate([-sr, zero, -sc, zero], axis=1)
    sin_b = np.concatenate([zero, sr, zero, sc], axis=1)
    tile = lambda a: jnp.asarray(np.concatenate([a, a], axis=1).astype(np.float32))
    return tile(cos_t), tile(sin_a), tile(sin_b)


def kernel(x_prompt, x_sample, cache_nat_k, cache_nat_v, cache_diff_k, cache_diff_v, c, c_ctx, mod_w, mod_b, norm_ffn1, ffn1_w_in, ffn1_w_out, norm_mix, even_w_in, pool_w, pool_scale, conv_w, odd_w_in, nat_rpb, diff_lambda, diff_norm, mix_w_out, norm_ffn2, ffn2_w_in, ffn2_w_out, final_norm):
    batch, seq, _ = x_prompt.shape
    dbatch, dseq, _ = x_sample.shape
    past = cache_nat_k.shape[2]
    tm = 512

    cond = jnp.zeros((SUBLANES, D_MODEL), F32).at[0].set(c_ctx).at[1:1 + dbatch].set(c)
    mods = _adaln_mods(cond, mod_w, mod_b).reshape(DEPTH, SUBLANES, N_MOD, D_MODEL)

    ctx_row = lambda i: 0
    lat_row = lambda i: 1 + (i * tm) // dseq

    bf = lambda w: w.astype(BF16)
    f1_in, f1_out, f2_in, f2_out = bf(ffn1_w_in), bf(ffn1_w_out), bf(ffn2_w_in), bf(ffn2_w_out)
    wmo = bf(mix_w_out)
    w_even, w_odd, w_pool = bf(even_w_in), bf(odd_w_in), bf(pool_w)
    w_odd_t = bf(jnp.swapaxes(odd_w_in, 1, 2))
    cache_bf = lambda a, w: a[:, 0].reshape(dbatch, past, w).astype(BF16)
    gain = lambda g: g.reshape(g.shape[0], 1, g.shape[1])
    g_ffn1, g_mix, g_ffn2 = gain(norm_ffn1), gain(norm_mix), gain(norm_ffn2)
    even_params = (w_pool, gain(pool_scale), conv_w)

    ctx = x_prompt.reshape(batch * seq, D_MODEL)
    lat = x_sample.reshape(dbatch * dseq, D_MODEL)

    l = 0
    streams = []
    for x, row_fn, n in ((ctx, ctx_row, seq), (lat, lat_row, dseq)):
        x1, u = _kernel_a(x, mods[l], row_fn, l, g_ffn1, f1_in, f1_out, g_mix, w_even, "even", tm)
        x2 = _kernel_b(x1, mods[l], row_fn, l, [u], even_params, wmo, g_ffn2, f2_in, f2_out, "even", tm,
                       seq_len=n)
        streams.append(x2)
    ctx, lat = streams

    l = 1
    lam_init = 0.8 - 0.6 * math.exp(-0.3 * l)
    ctx1, nq, nkt, nvt, dq, dkt, dv = _kernel_a(ctx, mods[l], ctx_row, l, g_ffn1, f1_in, f1_out, g_mix, w_odd,
                                                 "ctx", tm, wpt=w_odd_t, seq=seq)
    o_ctx = _ctx_attention([nq, nkt, nvt, dq, dkt, dv], diff_lambda[0], diff_norm[0], seq, lam_init)
    y_prompt = _kernel_b(ctx1, mods[l], ctx_row, l, [o_ctx], None, wmo, g_ffn2, f2_in, f2_out, "odd", tm,
                         final_g=final_norm)

    lat1, lq, lk, lv, ldq, ldk, ldv = _kernel_a(lat, mods[l], lat_row, l, g_ffn1, f1_in, f1_out, g_mix, w_odd,
                                                 "lat", tm, rope_tabs=_rope_tables(dseq))
    rpb_pad = jnp.zeros((H_NAT, 2 * NAT_WIN_R, LANES), F32).at[:, :2 * NAT_WIN_R - 1,
                                                                :2 * NAT_WIN_C - 1].set(nat_rpb[0])
    o_nat = _nat_lat_attention(lq, lk, lv, cache_bf(cache_nat_k, W_NAT), cache_bf(cache_nat_v, W_NAT),
                               rpb_pad, dbatch, dseq)
    o_diff = _diff_lat_attention(ldq, ldk, ldv, cache_bf(cache_diff_k, W_DIFF), cache_bf(cache_diff_v, W_DIFF),
                                 diff_lambda[0], diff_norm[0], dbatch, dseq, lam_init)
    y_sample = _kernel_b(lat1, mods[l], lat_row, l, [o_nat, o_diff], None, wmo, g_ffn2, f2_in, f2_out, "odd", tm,
                         final_g=final_norm)

    new_nat_k = nkt.reshape(batch, 1, H_NAT, HEAD_DIM, seq).transpose(0, 1, 4, 2, 3)
    new_nat_v = nvt.reshape(batch, 1, H_NAT, HEAD_DIM, seq).transpose(0, 1, 4, 2, 3)
    new_diff_k = dkt.reshape(batch, 1, H_DIFF, 2, HEAD_DIM, seq).transpose(0, 1, 5, 2, 3, 4)
    new_diff_v = dv.reshape(batch, 1, seq, H_DIFF, 2 * HEAD_DIM)
    return (y_prompt.reshape(batch, seq, D_MODEL), y_sample.reshape(dbatch, dseq, D_MODEL),
            new_nat_k, new_nat_v, new_diff_k, new_diff_v)
```

```python
import functools
import math

import numpy as np
import jax
import jax.numpy as jnp
from jax import lax
from jax.experimental import pallas as pl
from jax.experimental.pallas import tpu as pltpu

D_MODEL = 1024
D_FF = 2816
N_MOD = 9
DEPTH = 2
GRID_W = 64
HEAD_DIM = 64
W_POOL = 512
W_CONV = 512
POOL_WINDOWS = (2, 4, 8, 16)
POOL_G = 128
W_NAT = 512
W_DIFF = 512
H_NAT = 8
H_DIFF = 4
NAT_WIN_R = 8
NAT_WIN_C = 16
ROPE_THETA = 10000.0
ATTN_SCALE = HEAD_DIM ** -0.5
LOG2E = math.log2(math.e)
Q_SCALE = ATTN_SCALE * LOG2E
EPS = 1e-6
NEG = -1e30

LANES = 128
SUBLANES = 8
VMEM_LIMIT = 60 * 1024 * 1024

BF16 = jnp.bfloat16
F32 = jnp.float32


def _dot(a, b):
    return jnp.dot(a, b, preferred_element_type=F32)


def _dot_nt(a, b):
    return lax.dot_general(a, b, (((1,), (1,)), ((), ())), preferred_element_type=F32)


def _sigmoid(x):
    return 1.0 / (1.0 + jnp.exp(-x))


def _rms(x):
    return x * lax.rsqrt(jnp.mean(x * x, axis=-1, keepdims=True) + EPS)


def _modnorm(x, shift, scale, g):
    return _rms(x) * (g * (1.0 + scale)) + shift


def _ffn_half_step(x, mods, base, g, w_in_ref, w_out_ref, ck):
    shift, scale, gate = mods[base:base + 1], mods[base + 1:base + 2], mods[base + 2:base + 3]
    h = _modnorm(x, shift, scale, g).astype(BF16)
    acc = None
    for c in range(D_FF // ck):
        a = _dot(h, w_in_ref[:, c * ck:(c + 1) * ck])
        b = _dot(h, w_in_ref[:, D_FF + c * ck:D_FF + (c + 1) * ck])
        gated = ((a * _sigmoid(a)) * b).astype(BF16)
        t = _dot(gated, w_out_ref[c * ck:(c + 1) * ck, :])
        acc = t if acc is None else acc + t
    return x + (0.5 * gate) * acc


def _const_spec(shape):
    nd = len(shape)
    return pl.BlockSpec(shape, lambda i, _nd=nd: (0,) * _nd, pipeline_mode=pl.Buffered(1))


def _mods_kernel(cond_ref, w_ref, b_ref, o_ref):
    cnd = cond_ref[...]
    s = (cnd * _sigmoid(cnd)).astype(BF16)
    o_ref[0] = _dot(s, w_ref[0].astype(BF16)) + b_ref[0]


def _adaln_mods(cond, mod_w, mod_b):
    tn = 1152
    nt = (N_MOD * D_MODEL) // tn
    return pl.pallas_call(
        _mods_kernel,
        grid=(DEPTH, nt),
        in_specs=[
            pl.BlockSpec((SUBLANES, D_MODEL), lambda l, j: (0, 0)),
            pl.BlockSpec((1, D_MODEL, tn), lambda l, j: (l, 0, j)),
            pl.BlockSpec((1, 1, tn), lambda l, j: (l, 0, j)),
        ],
        out_specs=pl.BlockSpec((1, SUBLANES, tn), lambda l, j: (l, 0, j)),
        out_shape=jax.ShapeDtypeStruct((DEPTH, SUBLANES, N_MOD * D_MODEL), F32),
        compiler_params=pltpu.CompilerParams(vmem_limit_bytes=VMEM_LIMIT),
        name="adaln_mods",
    )(cond, mod_w, mod_b.reshape(DEPTH, 1, N_MOD * D_MODEL))


def _rope(x, cos_t, sin_a, sin_b):
    w = x.shape[1]
    reps = w // LANES
    c = jnp.concatenate([cos_t] * reps, axis=1)
    sa = jnp.concatenate([sin_a] * reps, axis=1)
    sb = jnp.concatenate([sin_b] * reps, axis=1)
    quarter = HEAD_DIM // 4
    up = pltpu.roll(x, w - quarter, 1)
    dn = pltpu.roll(x, quarter, 1)
    return x * c + up * sa + dn * sb


def _ka_kernel(*refs, mode, ck, seq):
    n_in = {"even": 7, "ctx": 8, "lat": 10}[mode]
    x_ref, mods_ref, g1_ref, w_in_ref, w_out_ref, gm_ref, wp_ref = refs[:7]
    outs = refs[n_in:]
    mods = mods_ref[0]
    x1 = _ffn_half_step(x_ref[...], mods, 0, g1_ref[...], w_in_ref, w_out_ref, ck)
    outs[0][...] = x1
    h = _modnorm(x1, mods[3:4], mods[4:5], gm_ref[...]).astype(BF16)
    if mode == "even":
        outs[1][...] = _dot(h, wp_ref[...])
        return
    if mode == "lat":
        cos_ref, sa_ref, sb_ref = refs[7:10]
    else:
        wpt_ref = refs[7]
    for p in range(6):
        if mode == "ctx" and p in (1, 2, 4):
            ut = _dot_nt(wpt_ref[p * 512:(p + 1) * 512, :], h)
            for bb in range(ut.shape[1] // seq):
                outs[1 + p][bb] = ut[:, bb * seq:(bb + 1) * seq]
            continue
        u = _dot(h, wp_ref[:, p * 512:(p + 1) * 512])
        if mode == "lat" and p in (3, 4):
            u = _rope(u, cos_ref[...], sa_ref[...], sb_ref[...])
        if p in (0, 3):
            u = u * Q_SCALE
        outs[1 + p][...] = u.astype(outs[1 + p].dtype)


def _layer_spec(w, l):
    nd = w.ndim - 1
    return pl.BlockSpec((None,) + w.shape[1:], lambda i, _l=l, _nd=nd: (_l,) + (0,) * _nd,
                        pipeline_mode=pl.Buffered(1))


def _kernel_a(x, mods_l, row_fn, l, g1, w_in, w_out, gm, wp, mode, tm, rope_tabs=None, wpt=None, seq=None,
              ck=256):
    t = x.shape[0]
    nt = t // tm
    tok = lambda w: pl.BlockSpec((tm, w), lambda i: (i, 0))
    in_specs = [
        tok(D_MODEL),
        pl.BlockSpec((1, N_MOD, D_MODEL), lambda i: (row_fn(i), 0, 0)),
        _layer_spec(g1, l), _layer_spec(w_in, l), _layer_spec(w_out, l), _layer_spec(gm, l), _layer_spec(wp, 0),
    ]
    args = [x, mods_l, g1, w_in, w_out, gm, wp]
    n_proj = wp.shape[2]
    out_shape = [jax.ShapeDtypeStruct((t, D_MODEL), F32)]
    out_specs = [tok(D_MODEL)]
    if mode == "even":
        out_shape.append(jax.ShapeDtypeStruct((t, n_proj), F32))
        out_specs.append(tok(n_proj))
    elif mode == "lat":
        seq_tiles = rope_tabs[0].shape[0] // tm
        for tab in rope_tabs:
            in_specs.append(pl.BlockSpec((tm, LANES), lambda i, _s=seq_tiles: (i % _s, 0)))
            args.append(tab)
        for _ in range(6):
            out_shape.append(jax.ShapeDtypeStruct((t, 512), BF16))
            out_specs.append(tok(512))
    else:
        in_specs.append(_layer_spec(wpt, 0))
        args.append(wpt)
        spt = tm // seq
        for p in range(6):
            if p in (1, 2, 4):
                out_shape.append(jax.ShapeDtypeStruct((t // seq, 512, seq), F32))
                out_specs.append(pl.BlockSpec((spt, 512, seq), lambda i: (i, 0, 0)))
            else:
                out_shape.append(jax.ShapeDtypeStruct((t, 512), BF16 if p in (0, 3) else F32))
                out_specs.append(tok(512))
    return pl.pallas_call(
        functools.partial(_ka_kernel, mode=mode, ck=ck, seq=seq),
        grid=(nt,),
        in_specs=in_specs,
        out_specs=out_specs,
        out_shape=out_shape,
        compiler_params=pltpu.CompilerParams(
            dimension_semantics=("arbitrary",), vmem_limit_bytes=VMEM_LIMIT),
        name="ffn1_proj_" + mode,
    )(*args)


HALO = 8


def _even_mix(ext_ref, base, n, pos, seq_len, pw_ref, ps_ref, cw_ref):
    def rows(j, lo, hi):
        return ext_ref[base + HALO + j:base + HALO + j + n, lo:hi]

    ya = []
    for g, win in enumerate(POOL_WINDOWS):
        half = win // 2
        lo_c, hi_c = g * POOL_G, (g + 1) * POOL_G
        s = rows(-half, lo_c, hi_c)
        for j in range(-half + 1, half):
            s = s + rows(j, lo_c, hi_c)
        lo = jnp.clip(pos - half, 0, seq_len - 1)
        hi = jnp.clip(pos + half - 1, 0, seq_len - 1)
        cnt = (hi - lo + 1).astype(F32)
        d = (s / cnt - rows(0, lo_c, hi_c)).astype(BF16)
        ya.append(_dot(d, pw_ref[g]))
    ya = jnp.concatenate(ya, axis=1) * ps_ref[...]

    def z(j):
        return rows(j, W_POOL + 2 * W_CONV, W_POOL + 3 * W_CONV) * rows(j, W_POOL, W_POOL + W_CONV)

    y = z(-1) * cw_ref[0:1, :] + z(0) * cw_ref[1:2, :] + z(1) * cw_ref[2:3, :]
    yb = rows(0, W_POOL + W_CONV, W_POOL + 2 * W_CONV) * y
    return jnp.concatenate([ya, yb], axis=1)


def _kb_kernel(*refs, mode, ck, tm, seq_len, final):
    x_ref, mods_ref = refs[0], refs[1]
    k = 2
    if mode == "even":
        u_ref, up_ref, un_ref, pw_ref, ps_ref, cw_ref = refs[k:k + 6]
        k += 6
    else:
        n_parts = 2 if mode == "odd2" else 1
        o_refs = refs[k:k + n_parts]
        k += n_parts
    wmo_ref, g2_ref, w_in_ref, w_out_ref = refs[k:k + 4]
    k += 4
    if final:
        gf_ref = refs[k]
        k += 1
    out_ref = refs[k]
    k += 1
    mods = mods_ref[0]
    if mode == "even":
        ext_ref = refs[k]
        n_seg = min(seq_len, tm)
        stride = n_seg + 2 * HALO
        row0 = pl.program_id(0) * tm
        feats = []
        for si in range(tm // n_seg):
            base = si * stride
            if n_seg == seq_len:
                before = after = jnp.zeros((HALO, ext_ref.shape[1]), F32)
            else:
                first = jnp.bitwise_and(row0, seq_len - 1) == 0
                last = jnp.bitwise_and(row0 + tm, seq_len - 1) == 0
                before = jnp.where(first, 0.0, up_ref[...])
                after = jnp.where(last, 0.0, un_ref[...])
            ext_ref[base:base + HALO, :] = before
            ext_ref[base + HALO:base + HALO + n_seg, :] = u_ref[si * n_seg:(si + 1) * n_seg, :]
            ext_ref[base + HALO + n_seg:base + stride, :] = after
            row = lax.broadcasted_iota(jnp.int32, (n_seg, LANES), 0) + (row0 + si * n_seg)
            pos = jnp.bitwise_and(row, seq_len - 1)
            feats.append(_even_mix(ext_ref, base, n_seg, pos, seq_len, pw_ref, ps_ref, cw_ref))
        feat = jnp.concatenate(feats, axis=0).astype(BF16)
        y = _dot(feat, wmo_ref[...])
    else:
        y = None
        off = 0
        for o_ref in o_refs:
            w = o_ref.shape[1]
            t = _dot(o_ref[...], wmo_ref[off:off + w, :])
            y = t if y is None else y + t
            off += w
    x2 = x_ref[...] + mods[5:6] * y
    x3 = _ffn_half_step(x2, mods, 6, g2_ref[...], w_in_ref, w_out_ref, ck)
    if final:
        x3 = _rms(x3) * gf_ref[...]
    out_ref[...] = x3


def _kernel_b(x, mods_l, row_fn, l, mix_in, mix_params, wmo, g2, w_in, w_out, mode, tm, seq_len=None,
              final_g=None, ck=256):
    t = x.shape[0]
    nt = t // tm
    tok = lambda w: pl.BlockSpec((tm, w), lambda i: (i, 0))
    in_specs = [tok(D_MODEL), pl.BlockSpec((1, N_MOD, D_MODEL), lambda i: (row_fn(i), 0, 0))]
    args = [x, mods_l]
    scratch = []
    if mode == "even":
        u = mix_in[0]
        wu = u.shape[1]
        hb = tm // HALO
        last = t // HALO - 1
        in_specs += [
            tok(wu),
            pl.BlockSpec((HALO, wu), lambda i: (jnp.maximum(i * hb - 1, 0), 0)),
            pl.BlockSpec((HALO, wu), lambda i: (jnp.minimum((i + 1) * hb, last), 0)),
        ]
        args += [u, u, u]
        for w in mix_params:
            in_specs.append(_layer_spec(w, 0))
            args.append(w)
        n_seg = min(seq_len, tm)
        assert tm % n_seg == 0 and seq_len % n_seg == 0 and seq_len & (seq_len - 1) == 0
        scratch.append(pltpu.VMEM(((tm // n_seg) * (n_seg + 2 * HALO), wu), F32))
    else:
        for o in mix_in:
            in_specs.append(tok(o.shape[1]))
            args.append(o)
    in_specs += [_layer_spec(wmo, l), _layer_spec(g2, l), _layer_spec(w_in, l), _layer_spec(w_out, l)]
    args += [wmo, g2, w_in, w_out]
    final = final_g is not None
    if final:
        in_specs.append(_const_spec((1, D_MODEL)))
        args.append(final_g.reshape(1, D_MODEL))
    kmode = mode if mode == "even" else ("odd2" if len(mix_in) == 2 else "odd1")
    return pl.pallas_call(
        functools.partial(_kb_kernel, mode=kmode, ck=ck, tm=tm, seq_len=seq_len, final=final),
        grid=(nt,),
        in_specs=in_specs,
        out_specs=tok(D_MODEL),
        out_shape=jax.ShapeDtypeStruct((t, D_MODEL), F32),
        scratch_shapes=scratch,
        compiler_params=pltpu.CompilerParams(
            dimension_semantics=("arbitrary",), vmem_limit_bytes=VMEM_LIMIT),
        name="mix_ffn2_" + kmode,
    )(*args)


def _diff_lambda(lam_ref, lam_init):
    lp = lam_ref[...]
    s1 = jnp.sum(lp[0:1] * lp[1:2], axis=-1, keepdims=True)
    s2 = jnp.sum(lp[2:3] * lp[3:4], axis=-1, keepdims=True)
    return jnp.exp(s1) - jnp.exp(s2) + lam_init


def _half_masks():
    lane = lax.broadcasted_iota(jnp.int32, (1, LANES), 1)
    lo = lane < HEAD_DIM
    return lo, jnp.logical_not(lo)


def _softmax_parts(parts):
    m = None
    for s in parts:
        mi = jnp.max(s, axis=-1, keepdims=True)
        m = mi if m is None else jnp.maximum(m, mi)
    es = [jnp.exp2(s - m) for s in parts]
    l = None
    for e in es:
        li = jnp.sum(e, axis=-1, keepdims=True)
        l = li if l is None else l + li
    return es, l


def _diff_out_norm(o, dn_ref, lam_init):
    return (_rms(o) * dn_ref[...]) * (1.0 - lam_init)


def _ctx_attn_kernel(nq_ref, nkt_ref, nvt_ref, dq_ref, dkt_ref, dv_ref, lam_ref, dn_ref, o_ref, *, lam_init):
    lo, hi = _half_masks()
    for hp in range(H_NAT // 2):
        sl = slice(hp * LANES, (hp + 1) * LANES)
        q = nq_ref[:, sl]
        kt = nkt_ref[sl, :].astype(BF16)
        vt = nvt_ref[sl, :].astype(BF16)
        outs = []
        for msk in (lo, hi):
            qm = jnp.where(msk, q, jnp.zeros_like(q))
            (e,), l = _softmax_parts([_dot(qm, kt)])
            outs.append(_dot_nt(e.astype(BF16), vt) * (1.0 / l))
        o_ref[:, sl] = jnp.where(lo, outs[0], outs[1]).astype(o_ref.dtype)
    lam = _diff_lambda(lam_ref, lam_init)
    for h in range(H_DIFF):
        sl = slice(h * LANES, (h + 1) * LANES)
        q = dq_ref[:, sl]
        kt = dkt_ref[sl, :].astype(BF16)
        v = dv_ref[:, sl].astype(BF16)
        ps = []
        for msk in (lo, hi):
            qm = jnp.where(msk, q, jnp.zeros_like(q))
            (e,), l = _softmax_parts([_dot(qm, kt)])
            ps.append(e * (1.0 / l))
        a = (ps[0] - lam * ps[1]).astype(BF16)
        o = _diff_out_norm(_dot(a, v), dn_ref, lam_init)
        o_ref[:, W_NAT + h * LANES:W_NAT + (h + 1) * LANES] = o.astype(o_ref.dtype)


def _ctx_attention(parts, lam_p, dnorm, seq, lam_init):
    nq, nkt, nvt, dq, dkt, dv = parts
    t = nq.shape[0]
    nb = t // seq
    tok = pl.BlockSpec((seq, 512), lambda i: (i, 0))
    fmaj = pl.BlockSpec((None, 512, seq), lambda i: (i, 0, 0))
    return pl.pallas_call(
        functools.partial(_ctx_attn_kernel, lam_init=lam_init),
        grid=(nb,),
        in_specs=[tok, fmaj, fmaj, tok, fmaj, tok, _const_spec((4, HEAD_DIM)), _const_spec((1, 2 * HEAD_DIM))],
        out_specs=pl.BlockSpec((seq, D_MODEL), lambda i: (i, 0)),
        out_shape=jax.ShapeDtypeStruct((t, D_MODEL), BF16),
        compiler_params=pltpu.CompilerParams(
            dimension_semantics=("arbitrary",), vmem_limit_bytes=VMEM_LIMIT),
        name="ctx_attention",
    )(nq, nkt, nvt, dq, dkt, dv, lam_p, dnorm.reshape(1, 2 * HEAD_DIM))


NAT_QROWS = 8
NAT_KROWS = 16
NAT_NDR = 2 * NAT_WIN_R


def _build_bias_tiles(rpb_ref, u_ref):
    c = lax.broadcasted_iota(jnp.int32, (GRID_W, LANES), 0)
    lane = lax.broadcasted_iota(jnp.int32, (GRID_W, LANES), 1)
    kc = jnp.bitwise_and(lane, GRID_W - 1)
    c0 = jnp.clip(c - NAT_WIN_C // 2, 0, GRID_W - NAT_WIN_C)
    col_ok = jnp.logical_and(kc >= c0, kc < c0 + NAT_WIN_C)
    left = lane < GRID_W
    n_dr = 2 * NAT_WIN_R - 1
    for h in range(H_NAT):
        tl, tr = [], []
        for dr in range(n_dr):
            row = jnp.broadcast_to(rpb_ref[h, dr:dr + 1, :] * LOG2E, (GRID_W, LANES))
            tl.append(pltpu.roll(row, LANES - (NAT_WIN_C - 1), 1, stride=1, stride_axis=0))
            tr.append(pltpu.roll(row, LANES - (NAT_WIN_C - 1) + GRID_W, 1, stride=1, stride_axis=0))
        for di in range(NAT_NDR):
            dl = di - NAT_WIN_R + (NAT_WIN_R - 1)
            drr = dl + 1
            lv = tl[dl] if 0 <= dl < n_dr else jnp.full((GRID_W, LANES), NEG, F32)
            rv = tr[drr] if 0 <= drr < n_dr else jnp.full((GRID_W, LANES), NEG, F32)
            u_ref[h, di] = jnp.where(col_ok, jnp.where(left, lv, rv), NEG)


def _nat_lat_kernel(q_ref, k_ref, v_ref, kc_ref, vc_ref, rpb_ref, o_ref, u_ref, s_ref):
    b = pl.program_id(1)

    @pl.when(jnp.logical_and(pl.program_id(0) == 0, b == 0))
    def _():
        _build_bias_tiles(rpb_ref, u_ref)

    rows = k_ref.shape[0] // GRID_W
    kb = jnp.clip(b * NAT_QROWS - NAT_WIN_R // 2, 0, rows - NAT_KROWS)
    koff = pl.multiple_of(kb * GRID_W, GRID_W)
    lo, hi = _half_masks()
    left = lax.broadcasted_iota(jnp.int32, (GRID_W, LANES), 1) < GRID_W
    nq = NAT_QROWS * GRID_W
    for hp in range(H_NAT // 2):
        sl = slice(hp * LANES, (hp + 1) * LANES)
        q = q_ref[:, sl]
        k = k_ref[pl.ds(koff, NAT_KROWS * GRID_W), sl]
        v = v_ref[pl.ds(koff, NAT_KROWS * GRID_W), sl]
        kc = kc_ref[:, sl]
        vc = vc_ref[:, sl]
        outs = []
        for e, msk in enumerate((lo, hi)):
            h = 2 * hp + e
            qm = jnp.where(msk, q, jnp.zeros_like(q))
            s_ref[...] = _dot_nt(qm, k)
            for i in range(NAT_QROWS):
                r = b * NAT_QROWS + i
                r0 = jnp.clip(r - NAT_WIN_R // 2, 0, rows - NAT_WIN_R)
                for jp in range(NAT_KROWS // 2):
                    kl = kb + 2 * jp
                    ok_l = jnp.logical_and(kl >= r0, kl < r0 + NAT_WIN_R)
                    ok_r = jnp.logical_and(kl + 1 >= r0, kl + 1 < r0 + NAT_WIN_R)
                    di = jnp.clip(kl - r + NAT_WIN_R, 0, NAT_NDR - 1)
                    ok = jnp.where(left, ok_l.astype(jnp.int32), ok_r.astype(jnp.int32))
                    bias = jnp.where(ok > 0, u_ref[h, di], NEG)
                    tile = (slice(i * GRID_W, (i + 1) * GRID_W), slice(jp * LANES, (jp + 1) * LANES))
                    s_ref[tile] = s_ref[tile] + bias
            s_ctx = _dot_nt(qm, kc)
            (e_loc, e_ctx), l = _softmax_parts([s_ref[...], s_ctx])
            o = _dot(e_loc.astype(BF16), v) + _dot(e_ctx.astype(BF16), vc)
            outs.append(o * (1.0 / l))
        o_ref[:, sl] = jnp.where(lo, outs[0], outs[1]).astype(o_ref.dtype)


def _nat_lat_attention(q, k, v, kc, vc, rpb_pad, batch, seq):
    nq = NAT_QROWS * GRID_W
    nblk = seq // nq
    p = kc.shape[1]
    return pl.pallas_call(
        _nat_lat_kernel,
        grid=(batch, nblk),
        in_specs=[
            pl.BlockSpec((nq, W_NAT), lambda bb, i: (bb * nblk + i, 0)),
            pl.BlockSpec((seq, W_NAT), lambda bb, i: (bb, 0)),
            pl.BlockSpec((seq, W_NAT), lambda bb, i: (bb, 0)),
            pl.BlockSpec((None, p, W_NAT), lambda bb, i: (bb, 0, 0)),
            pl.BlockSpec((None, p, W_NAT), lambda bb, i: (bb, 0, 0)),
            pl.BlockSpec(rpb_pad.shape, lambda bb, i: (0, 0, 0)),
        ],
        out_specs=pl.BlockSpec((nq, W_NAT), lambda bb, i: (bb * nblk + i, 0)),
        out_shape=jax.ShapeDtypeStruct((batch * seq, W_NAT), BF16),
        scratch_shapes=[
            pltpu.VMEM((H_NAT, NAT_NDR, GRID_W, LANES), F32),
            pltpu.VMEM((nq, NAT_KROWS * GRID_W), F32),
        ],
        compiler_params=pltpu.CompilerParams(
            dimension_semantics=("arbitrary", "arbitrary"), vmem_limit_bytes=VMEM_LIMIT),
        name="nat_latent_attention",
    )(q, k, v, kc, vc, rpb_pad)


DIFF_KCHUNK = 256


def _lane_fold(x, op):
    out = x[:, :LANES]
    for c in range(1, x.shape[1] // LANES):
        out = op(out, x[:, c * LANES:(c + 1) * LANES])
    return out


def _round_robin(*gens):
    gens = list(gens)
    while gens:
        for g in list(gens):
            try:
                next(g)
            except StopIteration:
                gens.remove(g)


def _diff_lat_kernel(q_ref, k_ref, v_ref, kc_ref, vc_ref, lam_ref, dn_ref, o_ref, sa_ref, sb_ref, *, lam_init):
    lo, hi = _half_masks()
    lam = _diff_lambda(lam_ref, lam_init)
    ck = DIFF_KCHUNK
    pieces = ([(k_ref, v_ref, c * ck) for c in range(k_ref.shape[0] // ck)]
              + [(kc_ref, vc_ref, c * ck) for c in range(kc_ref.shape[0] // ck)])
    row_max, row_sum = {}, {}

    def s_of(h):
        return (sa_ref, sb_ref)[h % 2]

    def scores(h):
        sl = slice(h * LANES, (h + 1) * LANES)
        q = q_ref[:, sl]
        ms = []
        for j, msk in enumerate((lo, hi)):
            qm = jnp.where(msk, q, jnp.zeros_like(q))
            mx = None
            for ci, (kr, _, off) in enumerate(pieces):
                s = _dot_nt(qm, kr[off:off + ck, sl])
                s_of(h)[j, :, ci * ck:(ci + 1) * ck] = s
                part = _lane_fold(s, jnp.maximum)
                mx = part if mx is None else jnp.maximum(mx, part)
                yield
            ms.append(jnp.max(mx, axis=-1, keepdims=True))
        row_max[h] = ms

    def exps(h):
        ls = []
        for j in range(2):
            acc = None
            for ci in range(len(pieces)):
                tile = (j, slice(None), slice(ci * ck, (ci + 1) * ck))
                e = jnp.exp2(s_of(h)[tile] - row_max[h][j])
                s_of(h)[tile] = e
                part = _lane_fold(e, jnp.add)
                acc = part if acc is None else acc + part
                yield
            ls.append(jnp.sum(acc, axis=-1, keepdims=True))
        row_sum[h] = ls

    def values(h):
        sl = slice(h * LANES, (h + 1) * LANES)
        l0, l1 = row_sum[h]
        cf = lam * l0 * (1.0 / l1)
        o = None
        for ci, (_, vr, off) in enumerate(pieces):
            cols = slice(ci * ck, (ci + 1) * ck)
            a = (s_of(h)[0, :, cols] - cf * s_of(h)[1, :, cols]).astype(BF16)
            t = _dot(a, vr[off:off + ck, sl])
            o = t if o is None else o + t
            yield
        o = o * (1.0 / l0)
        o_ref[:, sl] = _diff_out_norm(o, dn_ref, lam_init).astype(o_ref.dtype)

    _round_robin(scores(0))
    for h in range(H_DIFF):
        nxt = [scores(h + 1)] if h + 1 < H_DIFF else []
        _round_robin(exps(h), *nxt)
        _round_robin(values(h))


def _diff_lat_attention(q, k, v, kc, vc, lam_p, dnorm, batch, seq, lam_init, tq=256):
    nqb = seq // tq
    p = kc.shape[1]
    s_scratch = pltpu.VMEM((2, tq, seq + p), F32)
    return pl.pallas_call(
        functools.partial(_diff_lat_kernel, lam_init=lam_init),
        grid=(batch, nqb),
        in_specs=[
            pl.BlockSpec((tq, W_DIFF), lambda bb, i: (bb * nqb + i, 0)),
            pl.BlockSpec((seq, W_DIFF), lambda bb, i: (bb, 0)),
            pl.BlockSpec((seq, W_DIFF), lambda bb, i: (bb, 0)),
            pl.BlockSpec((None, p, W_DIFF), lambda bb, i: (bb, 0, 0)),
            pl.BlockSpec((None, p, W_DIFF), lambda bb, i: (bb, 0, 0)),
            pl.BlockSpec((4, HEAD_DIM), lambda bb, i: (0, 0)),
            pl.BlockSpec((1, 2 * HEAD_DIM), lambda bb, i: (0, 0)),
        ],
        out_specs=pl.BlockSpec((tq, W_DIFF), lambda bb, i: (bb * nqb + i, 0)),
        out_shape=jax.ShapeDtypeStruct((batch * seq, W_DIFF), BF16),
        scratch_shapes=[s_scratch, s_scratch],
        compiler_params=pltpu.CompilerParams(
            dimension_semantics=("arbitrary", "arbitrary"), vmem_limit_bytes=VMEM_LIMIT),
        name="diff_latent_attention",
    )(q, k, v, kc, vc, lam_p, dnorm.reshape(1, 2 * HEAD_DIM))


def _rope_tables(n):
    t = np.arange(n)
    row = (t // GRID_W).astype(np.float64)
    col = (t % GRID_W).astype(np.float64)
    quarter = HEAD_DIM // 4
    inv = 1.0 / (ROPE_THETA ** (np.arange(quarter) / quarter))
    cr, sr = np.cos(row[:, None] * inv[None]), np.sin(row[:, None] * inv[None])
    cc, sc = np.cos(col[:, None] * inv[None]), np.sin(col[:, None] * inv[None])
    zero = np.zeros_like(sr)
    cos_t = np.concatenate([cr, cr, cc, cc], axis=1)
    sin_a = np.concatenate([-sr, zero, -sc, zero], axis=1)
    sin_b = np.concatenate([zero, sr, zero, sc], axis=1)
    tile = lambda a: jnp.asarray(np.concatenate([a, a], axis=1).astype(np.float32))
    return tile(cos_t), tile(sin_a), tile(sin_b)


def kernel(x_prompt, x_sample, cache_nat_k, cache_nat_v, cache_diff_k, cache_diff_v, c, c_ctx, mod_w, mod_b, norm_ffn1, ffn1_w_in, ffn1_w_out, norm_mix, even_w_in, pool_w, pool_scale, conv_w, odd_w_in, nat_rpb, diff_lambda, diff_norm, mix_w_out, norm_ffn2, ffn2_w_in, ffn2_w_out, final_norm):
    batch, seq, _ = x_prompt.shape
    dbatch, dseq, _ = x_sample.shape
    past = cache_nat_k.shape[2]
    tm = 512

    cond = jnp.zeros((SUBLANES, D_MODEL), F32).at[0].set(c_ctx).at[1:1 + dbatch].set(c)
    mods = _adaln_mods(cond, mod_w, mod_b).reshape(DEPTH, SUBLANES, N_MOD, D_MODEL)

    ctx_row = lambda i: 0
    lat_row = lambda i: 1 + (i * tm) // dseq

    bf = lambda w: w.astype(BF16)
    f1_in, f1_out, f2_in, f2_out = bf(ffn1_w_in), bf(ffn1_w_out), bf(ffn2_w_in), bf(ffn2_w_out)
    wmo = bf(mix_w_out)
    w_even, w_odd, w_pool = bf(even_w_in), bf(odd_w_in), bf(pool_w)
    w_odd_t = bf(jnp.swapaxes(odd_w_in, 1, 2))
    cache_bf = lambda a, w: a[:, 0].reshape(dbatch, past, w).astype(BF16)
    gain = lambda g: g.reshape(g.shape[0], 1, g.shape[1])
    g_ffn1, g_mix, g_ffn2 = gain(norm_ffn1), gain(norm_mix), gain(norm_ffn2)
    even_params = (w_pool, gain(pool_scale), conv_w)

    ctx = x_prompt.reshape(batch * seq, D_MODEL)
    lat = x_sample.reshape(dbatch * dseq, D_MODEL)

    l = 0
    streams = []
    for x, row_fn, n in ((ctx, ctx_row, seq), (lat, lat_row, dseq)):
        x1, u = _kernel_a(x, mods[l], row_fn, l, g_ffn1, f1_in, f1_out, g_mix, w_even, "even", tm)
        x2 = _kernel_b(x1, mods[l], row_fn, l, [u], even_params, wmo, g_ffn2, f2_in, f2_out, "even", tm,
                       seq_len=n)
        streams.append(x2)
    ctx, lat = streams

    l = 1
    lam_init = 0.8 - 0.6 * math.exp(-0.3 * l)
    ctx1, nq, nkt, nvt, dq, dkt, dv = _kernel_a(ctx, mods[l], ctx_row, l, g_ffn1, f1_in, f1_out, g_mix, w_odd,
                                                 "ctx", tm, wpt=w_odd_t, seq=seq)
    o_ctx = _ctx_attention([nq, nkt, nvt, dq, dkt, dv], diff_lambda[0], diff_norm[0], seq, lam_init)
    y_prompt = _kernel_b(ctx1, mods[l], ctx_row, l, [o_ctx], None, wmo, g_ffn2, f2_in, f2_out, "odd", tm,
                         final_g=final_norm)

    lat1, lq, lk, lv, ldq, ldk, ldv = _kernel_a(lat, mods[l], lat_row, l, g_ffn1, f1_in, f1_out, g_mix, w_odd,
                                                 "lat", tm, rope_tabs=_rope_tables(dseq))
    rpb_pad = jnp.zeros((H_NAT, 2 * NAT_WIN_R, LANES), F32).at[:, :2 * NAT_WIN_R - 1,
                                                                :2 * NAT_WIN_C - 1].set(nat_rpb[0])
    o_nat = _nat_lat_attention(lq, lk, lv, cache_bf(cache_nat_k, W_NAT), cache_bf(cache_nat_v, W_NAT),
                               rpb_pad, dbatch, dseq)
    o_diff = _diff_lat_attention(ldq, ldk, ldv, cache_bf(cache_diff_k, W_DIFF), cache_bf(cache_diff_v, W_DIFF),
                                 diff_lambda[0], diff_norm[0], dbatch, dseq, lam_init)
    y_sample = _kernel_b(lat1, mods[l], lat_row, l, [o_nat, o_diff], None, wmo, g_ffn2, f2_in, f2_out, "odd", tm,
                         final_g=final_norm)

    new_nat_k = nkt.reshape(batch, 1, H_NAT, HEAD_DIM, seq).transpose(0, 1, 4, 2, 3)
    new_nat_v = nvt.reshape(batch, 1, H_NAT, HEAD_DIM, seq).transpose(0, 1, 4, 2, 3)
    new_diff_k = dkt.reshape(batch, 1, H_DIFF, 2, HEAD_DIM, seq).transpose(0, 1, 5, 2, 3, 4)
    new_diff_v = dv.reshape(batch, 1, seq, H_DIFF, 2 * HEAD_DIM)
    return (y_prompt.reshape(batch, seq, D_MODEL), y_sample.reshape(dbatch, dseq, D_MODEL),
            new_nat_k, new_nat_v, new_diff_k, new_diff_v)
```

```python
import functools
import math

import numpy as np
import jax
import jax.numpy as jnp
from jax import lax
from jax.experimental import pallas as pl
from jax.experimental.pallas import tpu as pltpu

D_MODEL = 1024
D_FF = 2816
N_MOD = 9
DEPTH = 2
GRID_W = 64
HEAD_DIM = 64
W_POOL = 512
W_CONV = 512
POOL_WINDOWS = (2, 4, 8, 16)
POOL_G = 128
W_NAT = 512
W_DIFF = 512
H_NAT = 8
H_DIFF = 4
NAT_WIN_R = 8
NAT_WIN_C = 16
ROPE_THETA = 10000.0
ATTN_SCALE = HEAD_DIM ** -0.5
LOG2E = math.log2(math.e)
Q_SCALE = ATTN_SCALE * LOG2E
EPS = 1e-6
NEG = -1e30

LANES = 128
SUBLANES = 8
VMEM_LIMIT = 60 * 1024 * 1024

BF16 = jnp.bfloat16
F32 = jnp.float32


def _dot(a, b):
    return jnp.dot(a, b, preferred_element_type=F32)


def _dot_nt(a, b):
    return lax.dot_general(a, b, (((1,), (1,)), ((), ())), preferred_element_type=F32)


def _sigmoid(x):
    return 1.0 / (1.0 + jnp.exp(-x))


def _rms(x):
    return x * lax.rsqrt(jnp.mean(x * x, axis=-1, keepdims=True) + EPS)


def _modnorm(x, shift, scale, g):
    return _rms(x) * (g * (1.0 + scale)) + shift


def _ffn_half_step(x, mods, base, g, w_in_ref, w_out_ref, ck):
    shift, scale, gate = mods[base:base + 1], mods[base + 1:base + 2], mods[base + 2:base + 3]
    h = _modnorm(x, shift, scale, g).astype(BF16)
    acc = None
    for c in range(D_FF // ck):
        a = _dot(h, w_in_ref[:, c * ck:(c + 1) * ck])
        b = _dot(h, w_in_ref[:, D_FF + c * ck:D_FF + (c + 1) * ck])
        gated = ((a * _sigmoid(a)) * b).astype(BF16)
        t = _dot(gated, w_out_ref[c * ck:(c + 1) * ck, :])
        acc = t if acc is None else acc + t
    return x + (0.5 * gate) * acc


def _const_spec(shape):
    nd = len(shape)
    return pl.BlockSpec(shape, lambda i, _nd=nd: (0,) * _nd, pipeline_mode=pl.Buffered(1))


def _mods_kernel(cond_ref, w_ref, b_ref, o_ref):
    cnd = cond_ref[...]
    s = (cnd * _sigmoid(cnd)).astype(BF16)
    o_ref[0] = _dot(s, w_ref[0].astype(BF16)) + b_ref[0]


def _adaln_mods(cond, mod_w, mod_b):
    tn = 1152
    nt = (N_MOD * D_MODEL) // tn
    return pl.pallas_call(
        _mods_kernel,
        grid=(DEPTH, nt),
        in_specs=[
            pl.BlockSpec((SUBLANES, D_MODEL), lambda l, j: (0, 0)),
            pl.BlockSpec((1, D_MODEL, tn), lambda l, j: (l, 0, j)),
            pl.BlockSpec((1, 1, tn), lambda l, j: (l, 0, j)),
        ],
        out_specs=pl.BlockSpec((1, SUBLANES, tn), lambda l, j: (l, 0, j)),
        out_shape=jax.ShapeDtypeStruct((DEPTH, SUBLANES, N_MOD * D_MODEL), F32),
        compiler_params=pltpu.CompilerParams(vmem_limit_bytes=VMEM_LIMIT),
        name="adaln_mods",
    )(cond, mod_w, mod_b.reshape(DEPTH, 1, N_MOD * D_MODEL))


def _rope(x, cos_t, sin_a, sin_b):
    w = x.shape[1]
    reps = w // LANES
    c = jnp.concatenate([cos_t] * reps, axis=1)
    sa = jnp.concatenate([sin_a] * reps, axis=1)
    sb = jnp.concatenate([sin_b] * reps, axis=1)
    quarter = HEAD_DIM // 4
    up = pltpu.roll(x, w - quarter, 1)
    dn = pltpu.roll(x, quarter, 1)
    return x * c + up * sa + dn * sb


def _side_casts(cast_ins, cast_outs):
    for ci, co in zip(cast_ins, cast_outs):
        co[0] = ci[...].astype(BF16)


def _ka_kernel(*refs, mode, ck, seq, n_cast):
    n_in = {"even": 7, "ctx": 8, "lat": 10}[mode]
    x_ref, mods_ref, g1_ref, w_in_ref, w_out_ref, gm_ref, wp_ref = refs[:7]
    outs = refs[n_in + n_cast:]
    _side_casts(refs[n_in:n_in + n_cast], outs[len(outs) - n_cast:])
    mods = mods_ref[0]
    x1 = _ffn_half_step(x_ref[...], mods, 0, g1_ref[...], w_in_ref, w_out_ref, ck)
    outs[0][...] = x1
    h = _modnorm(x1, mods[3:4], mods[4:5], gm_ref[...]).astype(BF16)
    if mode == "even":
        outs[1][...] = _dot(h, wp_ref[...])
        return
    if mode == "lat":
        cos_ref, sa_ref, sb_ref = refs[7:10]
    else:
        wpt_ref = refs[7]
    for p in range(6):
        if mode == "ctx" and p in (1, 2, 4):
            ut = _dot_nt(wpt_ref[p * 512:(p + 1) * 512, :], h)
            for bb in range(ut.shape[1] // seq):
                outs[1 + p][bb] = ut[:, bb * seq:(bb + 1) * seq]
            continue
        u = _dot(h, wp_ref[:, p * 512:(p + 1) * 512])
        if mode == "lat" and p in (3, 4):
            u = _rope(u, cos_ref[...], sa_ref[...], sb_ref[...])
        if p in (0, 3):
            u = u * Q_SCALE
        outs[1 + p][...] = u.astype(outs[1 + p].dtype)


def _layer_spec(wl):
    w, l = wl
    nd = w.ndim - 1
    return pl.BlockSpec((None,) + w.shape[1:], lambda i, _l=l, _nd=nd: (_l,) + (0,) * _nd,
                        pipeline_mode=pl.Buffered(1))


def _cast_specs(casts, nt):
    in_specs, args, out_specs, out_shape = [], [], [], []
    for w, l in casts:
        _, rows, cols = w.shape
        rb = rows // nt
        assert rows % nt == 0 and rb % (2 * SUBLANES) == 0
        in_specs.append(pl.BlockSpec((None, rb, cols), lambda i, _l=l: (_l, i, 0)))
        args.append(w)
        out_specs.append(pl.BlockSpec((1, rb, cols), lambda i: (0, i, 0)))
        out_shape.append(jax.ShapeDtypeStruct((1, rows, cols), BF16))
    return in_specs, args, out_specs, out_shape


def _cast_kernel(*refs):
    n = len(refs) // 2
    _side_casts(refs[:n], refs[n:])


def _cast_call(casts, nt=16):
    c_in, c_args, c_out, c_shape = _cast_specs(casts, nt)
    return pl.pallas_call(
        _cast_kernel, grid=(nt,), in_specs=c_in, out_specs=c_out, out_shape=c_shape,
        compiler_params=pltpu.CompilerParams(dimension_semantics=("arbitrary",), vmem_limit_bytes=VMEM_LIMIT),
        name="cast_weights",
    )(*c_args)


def _kernel_a(x, mods_l, row_fn, g1, w_in, w_out, gm, wp, mode, tm, rope_tabs=None, wpt=None, seq=None,
              casts=(), ck=256):
    t = x.shape[0]
    nt = t // tm
    tok = lambda w: pl.BlockSpec((tm, w), lambda i: (i, 0))
    weights = [g1, w_in, w_out, gm, wp]
    in_specs = [tok(D_MODEL), pl.BlockSpec((1, N_MOD, D_MODEL), lambda i: (row_fn(i), 0, 0))]
    in_specs += [_layer_spec(w) for w in weights]
    args = [x, mods_l] + [w[0] for w in weights]
    n_proj = wp[0].shape[2]
    out_shape = [jax.ShapeDtypeStruct((t, D_MODEL), F32)]
    out_specs = [tok(D_MODEL)]
    if mode == "even":
        out_shape.append(jax.ShapeDtypeStruct((t, n_proj), F32))
        out_specs.append(tok(n_proj))
    elif mode == "lat":
        seq_tiles = rope_tabs[0].shape[0] // tm
        for tab in rope_tabs:
            in_specs.append(pl.BlockSpec((tm, LANES), lambda i, _s=seq_tiles: (i % _s, 0)))
            args.append(tab)
        for _ in range(6):
            out_shape.append(jax.ShapeDtypeStruct((t, 512), BF16))
            out_specs.append(tok(512))
    else:
        in_specs.append(_layer_spec(wpt))
        args.append(wpt[0])
        spt = tm // seq
        for p in range(6):
            if p in (1, 2, 4):
                out_shape.append(jax.ShapeDtypeStruct((t // seq, 512, seq), F32))
                out_specs.append(pl.BlockSpec((spt, 512, seq), lambda i: (i, 0, 0)))
            else:
                out_shape.append(jax.ShapeDtypeStruct((t, 512), BF16 if p in (0, 3) else F32))
                out_specs.append(tok(512))
    c_in, c_args, c_out, c_shape = _cast_specs(casts, nt)
    return pl.pallas_call(
        functools.partial(_ka_kernel, mode=mode, ck=ck, seq=seq, n_cast=len(casts)),
        grid=(nt,),
        in_specs=in_specs + c_in,
        out_specs=out_specs + c_out,
        out_shape=out_shape + c_shape,
        compiler_params=pltpu.CompilerParams(
            dimension_semantics=("arbitrary",), vmem_limit_bytes=VMEM_LIMIT),
        name="ffn1_proj_" + mode,
    )(*args, *c_args)


HALO = 8


def _even_mix(ext_ref, base, n, pos, seq_len, pw_ref, ps_ref, cw_ref):
    def rows(j, lo, hi):
        return ext_ref[base + HALO + j:base + HALO + j + n, lo:hi]

    ya = []
    for g, win in enumerate(POOL_WINDOWS):
        half = win // 2
        lo_c, hi_c = g * POOL_G, (g + 1) * POOL_G
        s = rows(-half, lo_c, hi_c)
        for j in range(-half + 1, half):
            s = s + rows(j, lo_c, hi_c)
        lo = jnp.clip(pos - half, 0, seq_len - 1)
        hi = jnp.clip(pos + half - 1, 0, seq_len - 1)
        cnt = (hi - lo + 1).astype(F32)
        d = (s / cnt - rows(0, lo_c, hi_c)).astype(BF16)
        ya.append(_dot(d, pw_ref[g]))
    ya = jnp.concatenate(ya, axis=1) * ps_ref[...]

    def z(j):
        return rows(j, W_POOL + 2 * W_CONV, W_POOL + 3 * W_CONV) * rows(j, W_POOL, W_POOL + W_CONV)

    y = z(-1) * cw_ref[0:1, :] + z(0) * cw_ref[1:2, :] + z(1) * cw_ref[2:3, :]
    yb = rows(0, W_POOL + W_CONV, W_POOL + 2 * W_CONV) * y
    return jnp.concatenate([ya, yb], axis=1)


def _kb_kernel(*refs, mode, ck, tm, seq_len, final, n_cast):
    x_ref, mods_ref = refs[0], refs[1]
    k = 2
    if mode == "even":
        u_ref, up_ref, un_ref, pw_ref, ps_ref, cw_ref = refs[k:k + 6]
        k += 6
    else:
        n_parts = 2 if mode == "odd2" else 1
        o_refs = refs[k:k + n_parts]
        k += n_parts
    wmo_ref, g2_ref, w_in_ref, w_out_ref = refs[k:k + 4]
    k += 4
    if final:
        gf_ref = refs[k]
        k += 1
    out_ref = refs[k + n_cast]
    _side_casts(refs[k:k + n_cast], refs[k + n_cast + 1:k + 2 * n_cast + 1])
    k += 2 * n_cast + 1
    mods = mods_ref[0]
    if mode == "even":
        ext_ref = refs[k]
        n_seg = min(seq_len, tm)
        stride = n_seg + 2 * HALO
        row0 = pl.program_id(0) * tm
        feats = []
        for si in range(tm // n_seg):
            base = si * stride
            if n_seg == seq_len:
                before = after = jnp.zeros((HALO, ext_ref.shape[1]), F32)
            else:
                first = jnp.bitwise_and(row0, seq_len - 1) == 0
                last = jnp.bitwise_and(row0 + tm, seq_len - 1) == 0
                before = jnp.where(first, 0.0, up_ref[...])
                after = jnp.where(last, 0.0, un_ref[...])
            ext_ref[base:base + HALO, :] = before
            ext_ref[base + HALO:base + HALO + n_seg, :] = u_ref[si * n_seg:(si + 1) * n_seg, :]
            ext_ref[base + HALO + n_seg:base + stride, :] = after
            row = lax.broadcasted_iota(jnp.int32, (n_seg, LANES), 0) + (row0 + si * n_seg)
            pos = jnp.bitwise_and(row, seq_len - 1)
            feats.append(_even_mix(ext_ref, base, n_seg, pos, seq_len, pw_ref, ps_ref, cw_ref))
        feat = jnp.concatenate(feats, axis=0).astype(BF16)
        y = _dot(feat, wmo_ref[...])
    else:
        y = None
        off = 0
        for o_ref in o_refs:
            w = o_ref.shape[1]
            t = _dot(o_ref[...], wmo_ref[off:off + w, :])
            y = t if y is None else y + t
            off += w
    x2 = x_ref[...] + mods[5:6] * y
    x3 = _ffn_half_step(x2, mods, 6, g2_ref[...], w_in_ref, w_out_ref, ck)
    if final:
        x3 = _rms(x3) * gf_ref[...]
    out_ref[...] = x3


def _kernel_b(x, mods_l, row_fn, mix_in, mix_params, wmo, g2, w_in, w_out, mode, tm, seq_len=None,
              final_g=None, casts=(), ck=256):
    t = x.shape[0]
    nt = t // tm
    tok = lambda w: pl.BlockSpec((tm, w), lambda i: (i, 0))
    in_specs = [tok(D_MODEL), pl.BlockSpec((1, N_MOD, D_MODEL), lambda i: (row_fn(i), 0, 0))]
    args = [x, mods_l]
    scratch = []
    if mode == "even":
        u = mix_in[0]
        wu = u.shape[1]
        hb = tm // HALO
        last = t // HALO - 1
        in_specs += [
            tok(wu),
            pl.BlockSpec((HALO, wu), lambda i: (jnp.maximum(i * hb - 1, 0), 0)),
            pl.BlockSpec((HALO, wu), lambda i: (jnp.minimum((i + 1) * hb, last), 0)),
        ]
        args += [u, u, u]
        for w in mix_params:
            in_specs.append(_layer_spec(w))
            args.append(w[0])
        n_seg = min(seq_len, tm)
        assert tm % n_seg == 0 and seq_len % n_seg == 0 and seq_len & (seq_len - 1) == 0
        scratch.append(pltpu.VMEM(((tm // n_seg) * (n_seg + 2 * HALO), wu), F32))
    else:
        for o in mix_in:
            in_specs.append(tok(o.shape[1]))
            args.append(o)
    in_specs += [_layer_spec(w) for w in (wmo, g2, w_in, w_out)]
    args += [w[0] for w in (wmo, g2, w_in, w_out)]
    final = final_g is not None
    if final:
        in_specs.append(_const_spec((1, D_MODEL)))
        args.append(final_g.reshape(1, D_MODEL))
    kmode = mode if mode == "even" else ("odd2" if len(mix_in) == 2 else "odd1")
    c_in, c_args, c_out, c_shape = _cast_specs(casts, nt)
    return pl.pallas_call(
        functools.partial(_kb_kernel, mode=kmode, ck=ck, tm=tm, seq_len=seq_len, final=final, n_cast=len(casts)),
        grid=(nt,),
        in_specs=in_specs + c_in,
        out_specs=[tok(D_MODEL)] + c_out,
        out_shape=[jax.ShapeDtypeStruct((t, D_MODEL), F32)] + c_shape,
        scratch_shapes=scratch,
        compiler_params=pltpu.CompilerParams(
            dimension_semantics=("arbitrary",), vmem_limit_bytes=VMEM_LIMIT),
        name="mix_ffn2_" + kmode,
    )(*args, *c_args)


def _diff_lambda(lam_ref, lam_init):
    lp = lam_ref[...]
    s1 = jnp.sum(lp[0:1] * lp[1:2], axis=-1, keepdims=True)
    s2 = jnp.sum(lp[2:3] * lp[3:4], axis=-1, keepdims=True)
    return jnp.exp(s1) - jnp.exp(s2) + lam_init


def _half_masks():
    lane = lax.broadcasted_iota(jnp.int32, (1, LANES), 1)
    lo = lane < HEAD_DIM
    return lo, jnp.logical_not(lo)


def _softmax_parts(parts):
    m = None
    for s in parts:
        mi = jnp.max(s, axis=-1, keepdims=True)
        m = mi if m is None else jnp.maximum(m, mi)
    es = [jnp.exp2(s - m) for s in parts]
    l = None
    for e in es:
        li = jnp.sum(e, axis=-1, keepdims=True)
        l = li if l is None else l + li
    return es, l


def _diff_out_norm(o, dn_ref, lam_init):
    return (_rms(o) * dn_ref[...]) * (1.0 - lam_init)


def _ctx_attn_kernel(nq_ref, nkt_ref, nvt_ref, dq_ref, dkt_ref, dv_ref, lam_ref, dn_ref, o_ref, *, lam_init):
    lo, hi = _half_masks()
    for hp in range(H_NAT // 2):
        sl = slice(hp * LANES, (hp + 1) * LANES)
        q = nq_ref[:, sl]
        kt = nkt_ref[sl, :].astype(BF16)
        vt = nvt_ref[sl, :].astype(BF16)
        outs = []
        for msk in (lo, hi):
            qm = jnp.where(msk, q, jnp.zeros_like(q))
            (e,), l = _softmax_parts([_dot(qm, kt)])
            outs.append(_dot_nt(e.astype(BF16), vt) * (1.0 / l))
        o_ref[:, sl] = jnp.where(lo, outs[0], outs[1]).astype(o_ref.dtype)
    lam = _diff_lambda(lam_ref, lam_init)
    for h in range(H_DIFF):
        sl = slice(h * LANES, (h + 1) * LANES)
        q = dq_ref[:, sl]
        kt = dkt_ref[sl, :].astype(BF16)
        v = dv_ref[:, sl].astype(BF16)
        ps = []
        for msk in (lo, hi):
            qm = jnp.where(msk, q, jnp.zeros_like(q))
            (e,), l = _softmax_parts([_dot(qm, kt)])
            ps.append(e * (1.0 / l))
        a = (ps[0] - lam * ps[1]).astype(BF16)
        o = _diff_out_norm(_dot(a, v), dn_ref, lam_init)
        o_ref[:, W_NAT + h * LANES:W_NAT + (h + 1) * LANES] = o.astype(o_ref.dtype)


def _ctx_attention(parts, lam_p, dnorm, seq, lam_init):
    nq, nkt, nvt, dq, dkt, dv = parts
    t = nq.shape[0]
    nb = t // seq
    tok = pl.BlockSpec((seq, 512), lambda i: (i, 0))
    fmaj = pl.BlockSpec((None, 512, seq), lambda i: (i, 0, 0))
    return pl.pallas_call(
        functools.partial(_ctx_attn_kernel, lam_init=lam_init),
        grid=(nb,),
        in_specs=[tok, fmaj, fmaj, tok, fmaj, tok, _const_spec((4, HEAD_DIM)), _const_spec((1, 2 * HEAD_DIM))],
        out_specs=pl.BlockSpec((seq, D_MODEL), lambda i: (i, 0)),
        out_shape=jax.ShapeDtypeStruct((t, D_MODEL), BF16),
        compiler_params=pltpu.CompilerParams(
            dimension_semantics=("arbitrary",), vmem_limit_bytes=VMEM_LIMIT),
        name="ctx_attention",
    )(nq, nkt, nvt, dq, dkt, dv, lam_p, dnorm.reshape(1, 2 * HEAD_DIM))


NAT_QROWS = 8
NAT_KROWS = 16
NAT_NDR = 2 * NAT_WIN_R


def _build_bias_tiles(rpb_ref, u_ref):
    c = lax.broadcasted_iota(jnp.int32, (GRID_W, LANES), 0)
    lane = lax.broadcasted_iota(jnp.int32, (GRID_W, LANES), 1)
    kc = jnp.bitwise_and(lane, GRID_W - 1)
    c0 = jnp.clip(c - NAT_WIN_C // 2, 0, GRID_W - NAT_WIN_C)
    col_ok = jnp.logical_and(kc >= c0, kc < c0 + NAT_WIN_C)
    left = lane < GRID_W
    n_dr = 2 * NAT_WIN_R - 1
    for h in range(H_NAT):
        tl, tr = [], []
        for dr in range(n_dr):
            row = jnp.broadcast_to(rpb_ref[h, dr:dr + 1, :] * LOG2E, (GRID_W, LANES))
            tl.append(pltpu.roll(row, LANES - (NAT_WIN_C - 1), 1, stride=1, stride_axis=0))
            tr.append(pltpu.roll(row, LANES - (NAT_WIN_C - 1) + GRID_W, 1, stride=1, stride_axis=0))
        for di in range(NAT_NDR):
            dl = di - NAT_WIN_R + (NAT_WIN_R - 1)
            drr = dl + 1
            lv = tl[dl] if 0 <= dl < n_dr else jnp.full((GRID_W, LANES), NEG, F32)
            rv = tr[drr] if 0 <= drr < n_dr else jnp.full((GRID_W, LANES), NEG, F32)
            u_ref[h, di] = jnp.where(col_ok, jnp.where(left, lv, rv), NEG)


def _nat_lat_kernel(q_ref, k_ref, v_ref, kc_ref, vc_ref, rpb_ref, o_ref, u_ref, s_ref):
    b = pl.program_id(1)

    @pl.when(jnp.logical_and(pl.program_id(0) == 0, b == 0))
    def _():
        _build_bias_tiles(rpb_ref, u_ref)

    rows = k_ref.shape[0] // GRID_W
    kb = jnp.clip(b * NAT_QROWS - NAT_WIN_R // 2, 0, rows - NAT_KROWS)
    koff = pl.multiple_of(kb * GRID_W, GRID_W)
    lo, hi = _half_masks()
    left = lax.broadcasted_iota(jnp.int32, (GRID_W, LANES), 1) < GRID_W
    nq = NAT_QROWS * GRID_W
    for hp in range(H_NAT // 2):
        sl = slice(hp * LANES, (hp + 1) * LANES)
        q = q_ref[:, sl]
        k = k_ref[pl.ds(koff, NAT_KROWS * GRID_W), sl]
        v = v_ref[pl.ds(koff, NAT_KROWS * GRID_W), sl]
        kc = kc_ref[:, sl]
        vc = vc_ref[:, sl]
        outs = []
        for e, msk in enumerate((lo, hi)):
            h = 2 * hp + e
            qm = jnp.where(msk, q, jnp.zeros_like(q))
            s_ref[...] = _dot_nt(qm, k)
            for i in range(NAT_QROWS):
                r = b * NAT_QROWS + i
                r0 = jnp.clip(r - NAT_WIN_R // 2, 0, rows - NAT_WIN_R)
                for jp in range(NAT_KROWS // 2):
                    kl = kb + 2 * jp
                    ok_l = jnp.logical_and(kl >= r0, kl < r0 + NAT_WIN_R)
                    ok_r = jnp.logical_and(kl + 1 >= r0, kl + 1 < r0 + NAT_WIN_R)
                    di = jnp.clip(kl - r + NAT_WIN_R, 0, NAT_NDR - 1)
                    ok = jnp.where(left, ok_l.astype(jnp.int32), ok_r.astype(jnp.int32))
                    bias = jnp.where(ok > 0, u_ref[h, di], NEG)
                    tile = (slice(i * GRID_W, (i + 1) * GRID_W), slice(jp * LANES, (jp + 1) * LANES))
                    s_ref[tile] = s_ref[tile] + bias
            s_ctx = _dot_nt(qm, kc)
            (e_loc, e_ctx), l = _softmax_parts([s_ref[...], s_ctx])
            o = _dot(e_loc.astype(BF16), v) + _dot(e_ctx.astype(BF16), vc)
            outs.append(o * (1.0 / l))
        o_ref[:, sl] = jnp.where(lo, outs[0], outs[1]).astype(o_ref.dtype)


def _nat_lat_attention(q, k, v, kc, vc, rpb_pad, batch, seq):
    nq = NAT_QROWS * GRID_W
    nblk = seq // nq
    p = kc.shape[1]
    return pl.pallas_call(
        _nat_lat_kernel,
        grid=(batch, nblk),
        in_specs=[
            pl.BlockSpec((nq, W_NAT), lambda bb, i: (bb * nblk + i, 0)),
            pl.BlockSpec((seq, W_NAT), lambda bb, i: (bb, 0)),
            pl.BlockSpec((seq, W_NAT), lambda bb, i: (bb, 0)),
            pl.BlockSpec((None, p, W_NAT), lambda bb, i: (bb, 0, 0)),
            pl.BlockSpec((None, p, W_NAT), lambda bb, i: (bb, 0, 0)),
            pl.BlockSpec(rpb_pad.shape, lambda bb, i: (0, 0, 0)),
        ],
        out_specs=pl.BlockSpec((nq, W_NAT), lambda bb, i: (bb * nblk + i, 0)),
        out_shape=jax.ShapeDtypeStruct((batch * seq, W_NAT), BF16),
        scratch_shapes=[
            pltpu.VMEM((H_NAT, NAT_NDR, GRID_W, LANES), F32),
            pltpu.VMEM((nq, NAT_KROWS * GRID_W), F32),
        ],
        compiler_params=pltpu.CompilerParams(
            dimension_semantics=("arbitrary", "arbitrary"), vmem_limit_bytes=VMEM_LIMIT),
        name="nat_latent_attention",
    )(q, k, v, kc, vc, rpb_pad)


DIFF_KCHUNK = 256


def _lane_fold(x, op):
    out = x[:, :LANES]
    for c in range(1, x.shape[1] // LANES):
        out = op(out, x[:, c * LANES:(c + 1) * LANES])
    return out


def _round_robin(*gens):
    gens = list(gens)
    while gens:
        for g in list(gens):
            try:
                next(g)
            except StopIteration:
                gens.remove(g)


def _diff_lat_kernel(q_ref, k_ref, v_ref, kc_ref, vc_ref, lam_ref, dn_ref, o_ref, sa_ref, sb_ref, *, lam_init):
    lo, hi = _half_masks()
    lam = _diff_lambda(lam_ref, lam_init)
    ck = DIFF_KCHUNK
    pieces = ([(k_ref, v_ref, c * ck) for c in range(k_ref.shape[0] // ck)]
              + [(kc_ref, vc_ref, c * ck) for c in range(kc_ref.shape[0] // ck)])
    row_max, row_sum = {}, {}

    def s_of(h):
        return (sa_ref, sb_ref)[h % 2]

    def scores(h):
        sl = slice(h * LANES, (h + 1) * LANES)
        q = q_ref[:, sl]
        ms = []
        for j, msk in enumerate((lo, hi)):
            qm = jnp.where(msk, q, jnp.zeros_like(q))
            mx = None
            for ci, (kr, _, off) in enumerate(pieces):
                s = _dot_nt(qm, kr[off:off + ck, sl])
                s_of(h)[j, :, ci * ck:(ci + 1) * ck] = s
                part = _lane_fold(s, jnp.maximum)
                mx = part if mx is None else jnp.maximum(mx, part)
                yield
            ms.append(jnp.max(mx, axis=-1, keepdims=True))
        row_max[h] = ms

    def exps(h):
        ls = []
        for j in range(2):
            acc = None
            for ci in range(len(pieces)):
                tile = (j, slice(None), slice(ci * ck, (ci + 1) * ck))
                e = jnp.exp2(s_of(h)[tile] - row_max[h][j])
                s_of(h)[tile] = e
                part = _lane_fold(e, jnp.add)
                acc = part if acc is None else acc + part
                yield
            ls.append(jnp.sum(acc, axis=-1, keepdims=True))
        row_sum[h] = ls

    def values(h):
        sl = slice(h * LANES, (h + 1) * LANES)
        l0, l1 = row_sum[h]
        cf = lam * l0 * (1.0 / l1)
        o = None
        for ci, (_, vr, off) in enumerate(pieces):
            cols = slice(ci * ck, (ci + 1) * ck)
            a = (s_of(h)[0, :, cols] - cf * s_of(h)[1, :, cols]).astype(BF16)
            t = _dot(a, vr[off:off + ck, sl])
            o = t if o is None else o + t
            yield
        o = o * (1.0 / l0)
        o_ref[:, sl] = _diff_out_norm(o, dn_ref, lam_init).astype(o_ref.dtype)

    _round_robin(scores(0))
    for h in range(H_DIFF):
        nxt = [scores(h + 1)] if h + 1 < H_DIFF else []
        _round_robin(exps(h), *nxt)
        _round_robin(values(h))


def _diff_lat_attention(q, k, v, kc, vc, lam_p, dnorm, batch, seq, lam_init, tq=256):
    nqb = seq // tq
    p = kc.shape[1]
    s_scratch = pltpu.VMEM((2, tq, seq + p), F32)
    return pl.pallas_call(
        functools.partial(_diff_lat_kernel, lam_init=lam_init),
        grid=(batch, nqb),
        in_specs=[
            pl.BlockSpec((tq, W_DIFF), lambda bb, i: (bb * nqb + i, 0)),
            pl.BlockSpec((seq, W_DIFF), lambda bb, i: (bb, 0)),
            pl.BlockSpec((seq, W_DIFF), lambda bb, i: (bb, 0)),
            pl.BlockSpec((None, p, W_DIFF), lambda bb, i: (bb, 0, 0)),
            pl.BlockSpec((None, p, W_DIFF), lambda bb, i: (bb, 0, 0)),
            pl.BlockSpec((4, HEAD_DIM), lambda bb, i: (0, 0)),
            pl.BlockSpec((1, 2 * HEAD_DIM), lambda bb, i: (0, 0)),
        ],
        out_specs=pl.BlockSpec((tq, W_DIFF), lambda bb, i: (bb * nqb + i, 0)),
        out_shape=jax.ShapeDtypeStruct((batch * seq, W_DIFF), BF16),
        scratch_shapes=[s_scratch, s_scratch],
        compiler_params=pltpu.CompilerParams(
            dimension_semantics=("arbitrary", "arbitrary"), vmem_limit_bytes=VMEM_LIMIT),
        name="diff_latent_attention",
    )(q, k, v, kc, vc, lam_p, dnorm.reshape(1, 2 * HEAD_DIM))


def _rope_tables(n):
    t = np.arange(n)
    row = (t // GRID_W).astype(np.float64)
    col = (t % GRID_W).astype(np.float64)
    quarter = HEAD_DIM // 4
    inv = 1.0 / (ROPE_THETA ** (np.arange(quarter) / quarter))
    cr, sr = np.cos(row[:, None] * inv[None]), np.sin(row[:, None] * inv[None])
    cc, sc = np.cos(col[:, None] * inv[None]), np.sin(col[:, None] * inv[None])
    zero = np.zeros_like(sr)
    cos_t = np.concatenate([cr, cr, cc, cc], axis=1)
    sin_a = np.concatenate([-sr, zero, -sc, zero], axis=1)
    sin_b = np.concatenate([zero, sr, zero, sc], axis=1)
    tile = lambda a: jnp.asarray(np.concatenate([a, a], axis=1).astype(np.float32))
    return tile(cos_t), tile(sin_a), tile(sin_b)


def kernel(x_prompt, x_sample, cache_nat_k, cache_nat_v, cache_diff_k, cache_diff_v, c, c_ctx, mod_w, mod_b, norm_ffn1, ffn1_w_in, ffn1_w_out, norm_mix, even_w_in, pool_w, pool_scale, conv_w, odd_w_in, nat_rpb, diff_lambda, diff_norm, mix_w_out, norm_ffn2, ffn2_w_in, ffn2_w_out, final_norm):
    batch, seq, _ = x_prompt.shape
    dbatch, dseq, _ = x_sample.shape
    past = cache_nat_k.shape[2]
    tm = 512

    cond = jnp.zeros((SUBLANES, D_MODEL), F32).at[0].set(c_ctx).at[1:1 + dbatch].set(c)
    mods = _adaln_mods(cond, mod_w, mod_b).reshape(DEPTH, SUBLANES, N_MOD, D_MODEL)

    ctx_row = lambda i: 0
    lat_row = lambda i: 1 + (i * tm) // dseq

    cache_bf = lambda a, w: a[:, 0].reshape(dbatch, past, w).astype(BF16)
    gain = lambda g: g.reshape(g.shape[0], 1, g.shape[1])
    g_ffn1, g_mix, g_ffn2 = gain(norm_ffn1), gain(norm_mix), gain(norm_ffn2)
    pool3 = pool_w.reshape(pool_w.shape[0], len(POOL_WINDOWS) * POOL_G, POOL_G)

    f1_in0, f1_out0, w_even0 = _cast_call([(ffn1_w_in, 0), (ffn1_w_out, 0), (even_w_in, 0)])
    w_odd_t = jnp.swapaxes(odd_w_in, 1, 2).astype(BF16)

    ctx = x_prompt.reshape(batch * seq, D_MODEL)
    lat = x_sample.reshape(dbatch * dseq, D_MODEL)

    l = 0
    x1c, uc, f2_in0, f2_out0, wmo0, pool0 = _kernel_a(
        ctx, mods[l], ctx_row, (g_ffn1, l), (f1_in0, 0), (f1_out0, 0), (g_mix, l), (w_even0, 0), "even", tm,
        casts=[(ffn2_w_in, 0), (ffn2_w_out, 0), (mix_w_out, 0), (pool3, 0)])
    even_params = ((pool0.reshape(1, len(POOL_WINDOWS), POOL_G, POOL_G), 0), (gain(pool_scale), 0), (conv_w, 0))
    ctx, f1_in1, f1_out1 = _kernel_b(
        x1c, mods[l], ctx_row, [uc], even_params, (wmo0, 0), (g_ffn2, l), (f2_in0, 0), (f2_out0, 0), "even", tm,
        seq_len=seq, casts=[(ffn1_w_in, 1), (ffn1_w_out, 1)])
    x1l, ul, w_odd1, f2_in1 = _kernel_a(
        lat, mods[l], lat_row, (g_ffn1, l), (f1_in0, 0), (f1_out0, 0), (g_mix, l), (w_even0, 0), "even", tm,
        casts=[(odd_w_in, 0), (ffn2_w_in, 1)])
    lat, f2_out1, wmo1 = _kernel_b(
        x1l, mods[l], lat_row, [ul], even_params, (wmo0, 0), (g_ffn2, l), (f2_in0, 0), (f2_out0, 0), "even", tm,
        seq_len=dseq, casts=[(ffn2_w_out, 1), (mix_w_out, 1)])

    l = 1
    lam_init = 0.8 - 0.6 * math.exp(-0.3 * l)
    ffn1 = ((g_ffn1, l), (f1_in1, 0), (f1_out1, 0), (g_mix, l), (w_odd1, 0))
    ffn2 = ((wmo1, 0), (g_ffn2, l), (f2_in1, 0), (f2_out1, 0))
    ctx1, nq, nkt, nvt, dq, dkt, dv = _kernel_a(ctx, mods[l], ctx_row, *ffn1, "ctx", tm, wpt=(w_odd_t, 0), seq=seq)
    o_ctx = _ctx_attention([nq, nkt, nvt, dq, dkt, dv], diff_lambda[0], diff_norm[0], seq, lam_init)
    y_prompt, = _kernel_b(ctx1, mods[l], ctx_row, [o_ctx], None, *ffn2, "odd", tm, final_g=final_norm)

    lat1, lq, lk, lv, ldq, ldk, ldv = _kernel_a(lat, mods[l], lat_row, *ffn1, "lat", tm,
                                                 rope_tabs=_rope_tables(dseq))
    rpb_pad = jnp.zeros((H_NAT, 2 * NAT_WIN_R, LANES), F32).at[:, :2 * NAT_WIN_R - 1,
                                                                :2 * NAT_WIN_C - 1].set(nat_rpb[0])
    o_nat = _nat_lat_attention(lq, lk, lv, cache_bf(cache_nat_k, W_NAT), cache_bf(cache_nat_v, W_NAT),
                               rpb_pad, dbatch, dseq)
    o_diff = _diff_lat_attention(ldq, ldk, ldv, cache_bf(cache_diff_k, W_DIFF), cache_bf(cache_diff_v, W_DIFF),
                                 diff_lambda[0], diff_norm[0], dbatch, dseq, lam_init)
    y_sample, = _kernel_b(lat1, mods[l], lat_row, [o_nat, o_diff], None, *ffn2, "odd", tm, final_g=final_norm)

    new_nat_k = nkt.reshape(batch, 1, H_NAT, HEAD_DIM, seq).transpose(0, 1, 4, 2, 3)
    new_nat_v = nvt.reshape(batch, 1, H_NAT, HEAD_DIM, seq).transpose(0, 1, 4, 2, 3)
    new_diff_k = dkt.reshape(batch, 1, H_DIFF, 2, HEAD_DIM, seq).transpose(0, 1, 5, 2, 3, 4)
    new_diff_v = dv.reshape(batch, 1, seq, H_DIFF, 2 * HEAD_DIM)
    return (y_prompt.reshape(batch, seq, D_MODEL), y_sample.reshape(dbatch, dseq, D_MODEL),
            new_nat_k, new_nat_v, new_diff_k, new_diff_v)
```

```python
import functools
import math

import numpy as np
import jax
import jax.numpy as jnp
from jax import lax
from jax.experimental import pallas as pl
from jax.experimental.pallas import tpu as pltpu

D_MODEL = 1024
D_FF = 2816
N_MOD = 9
DEPTH = 2
GRID_W = 64
HEAD_DIM = 64
W_POOL = 512
W_CONV = 512
POOL_WINDOWS = (2, 4, 8, 16)
POOL_G = 128
W_NAT = 512
W_DIFF = 512
H_NAT = 8
H_DIFF = 4
NAT_WIN_R = 8
NAT_WIN_C = 16
ROPE_THETA = 10000.0
ATTN_SCALE = HEAD_DIM ** -0.5
LOG2E = math.log2(math.e)
Q_SCALE = ATTN_SCALE * LOG2E
EPS = 1e-6
NEG = -1e30

LANES = 128
SUBLANES = 8
VMEM_LIMIT = 60 * 1024 * 1024

BF16 = jnp.bfloat16
F32 = jnp.float32


def _dot(a, b):
    return jnp.dot(a, b, preferred_element_type=F32)


def _dot_nt(a, b):
    return lax.dot_general(a, b, (((1,), (1,)), ((), ())), preferred_element_type=F32)


def _sigmoid(x):
    return 1.0 / (1.0 + jnp.exp(-x))


def _rms(x):
    return x * lax.rsqrt(jnp.mean(x * x, axis=-1, keepdims=True) + EPS)


def _modnorm(x, shift, scale, g):
    return _rms(x) * (g * (1.0 + scale)) + shift


def _ffn_half_step(x, mods, base, g, w_in_ref, w_out_ref, ck):
    shift, scale, gate = mods[base:base + 1], mods[base + 1:base + 2], mods[base + 2:base + 3]
    h = _modnorm(x, shift, scale, g).astype(BF16)
    acc = None
    for c in range(D_FF // ck):
        a = _dot(h, w_in_ref[:, c * ck:(c + 1) * ck])
        b = _dot(h, w_in_ref[:, D_FF + c * ck:D_FF + (c + 1) * ck])
        gated = ((a * _sigmoid(a)) * b).astype(BF16)
        t = _dot(gated, w_out_ref[c * ck:(c + 1) * ck, :])
        acc = t if acc is None else acc + t
    return x + (0.5 * gate) * acc


def _const_spec(shape):
    nd = len(shape)
    return pl.BlockSpec(shape, lambda i, _nd=nd: (0,) * _nd, pipeline_mode=pl.Buffered(1))


def _mods_kernel(cond_ref, w_ref, b_ref, o_ref):
    cnd = cond_ref[...]
    s = (cnd * _sigmoid(cnd)).astype(BF16)
    o_ref[0] = _dot(s, w_ref[0].astype(BF16)) + b_ref[0]


def _adaln_mods(cond, mod_w, mod_b):
    tn = 1152
    nt = (N_MOD * D_MODEL) // tn
    return pl.pallas_call(
        _mods_kernel,
        grid=(DEPTH, nt),
        in_specs=[
            pl.BlockSpec((SUBLANES, D_MODEL), lambda l, j: (0, 0)),
            pl.BlockSpec((1, D_MODEL, tn), lambda l, j: (l, 0, j)),
            pl.BlockSpec((1, 1, tn), lambda l, j: (l, 0, j)),
        ],
        out_specs=pl.BlockSpec((1, SUBLANES, tn), lambda l, j: (l, 0, j)),
        out_shape=jax.ShapeDtypeStruct((DEPTH, SUBLANES, N_MOD * D_MODEL), F32),
        compiler_params=pltpu.CompilerParams(vmem_limit_bytes=VMEM_LIMIT),
        name="adaln_mods",
    )(cond, mod_w, mod_b.reshape(DEPTH, 1, N_MOD * D_MODEL))


def _rope(x, cos_t, sin_a, sin_b):
    w = x.shape[1]
    reps = w // LANES
    c = jnp.concatenate([cos_t] * reps, axis=1)
    sa = jnp.concatenate([sin_a] * reps, axis=1)
    sb = jnp.concatenate([sin_b] * reps, axis=1)
    quarter = HEAD_DIM // 4
    up = pltpu.roll(x, w - quarter, 1)
    dn = pltpu.roll(x, quarter, 1)
    return x * c + up * sa + dn * sb


def _side_casts(cast_ins, cast_outs):
    for ci, co in zip(cast_ins, cast_outs):
        co[0] = ci[...].astype(BF16)


def _ka_kernel(*refs, mode, ck, seq, n_cast):
    n_in = {"even": 7, "ctx": 8, "lat": 10}[mode]
    x_ref, mods_ref, g1_ref, w_in_ref, w_out_ref, gm_ref, wp_ref = refs[:7]
    outs = refs[n_in + n_cast:]
    _side_casts(refs[n_in:n_in + n_cast], outs[len(outs) - n_cast:])
    mods = mods_ref[0]
    x1 = _ffn_half_step(x_ref[...], mods, 0, g1_ref[...], w_in_ref, w_out_ref, ck)
    outs[0][...] = x1
    h = _modnorm(x1, mods[3:4], mods[4:5], gm_ref[...]).astype(BF16)
    if mode == "even":
        outs[1][...] = _dot(h, wp_ref[...])
        return
    if mode == "lat":
        cos_ref, sa_ref, sb_ref = refs[7:10]
    else:
        wpt_ref = refs[7]
    for p in range(6):
        if mode == "ctx" and p in (1, 2, 4):
            ut = _dot_nt(wpt_ref[p * 512:(p + 1) * 512, :], h)
            for bb in range(ut.shape[1] // seq):
                outs[1 + p][bb] = ut[:, bb * seq:(bb + 1) * seq]
            continue
        u = _dot(h, wp_ref[:, p * 512:(p + 1) * 512])
        if mode == "lat" and p in (3, 4):
            u = _rope(u, cos_ref[...], sa_ref[...], sb_ref[...])
        if p in (0, 3):
            u = u * Q_SCALE
        outs[1 + p][...] = u.astype(outs[1 + p].dtype)


def _layer_spec(wl):
    w, l = wl
    nd = w.ndim - 1
    return pl.BlockSpec((None,) + w.shape[1:], lambda i, _l=l, _nd=nd: (_l,) + (0,) * _nd,
                        pipeline_mode=pl.Buffered(1))


def _cast_specs(casts, nt):
    in_specs, args, out_specs, out_shape = [], [], [], []
    for w, l in casts:
        _, rows, cols = w.shape
        rb = rows // nt
        assert rows % nt == 0 and rb % (2 * SUBLANES) == 0
        in_specs.append(pl.BlockSpec((None, rb, cols), lambda i, _l=l: (_l, i, 0)))
        args.append(w)
        out_specs.append(pl.BlockSpec((1, rb, cols), lambda i: (0, i, 0)))
        out_shape.append(jax.ShapeDtypeStruct((1, rows, cols), BF16))
    return in_specs, args, out_specs, out_shape


def _cast_kernel(*refs):
    n = len(refs) // 2
    _side_casts(refs[:n], refs[n:])


def _cast_call(casts, nt=16):
    c_in, c_args, c_out, c_shape = _cast_specs(casts, nt)
    return pl.pallas_call(
        _cast_kernel, grid=(nt,), in_specs=c_in, out_specs=c_out, out_shape=c_shape,
        compiler_params=pltpu.CompilerParams(dimension_semantics=("arbitrary",), vmem_limit_bytes=VMEM_LIMIT),
        name="cast_weights",
    )(*c_args)


def _kernel_a(x, mods_l, row_fn, g1, w_in, w_out, gm, wp, mode, tm, rope_tabs=None, wpt=None, seq=None,
              casts=(), ck=256):
    t = x.shape[0]
    nt = t // tm
    tok = lambda w: pl.BlockSpec((tm, w), lambda i: (i, 0))
    weights = [g1, w_in, w_out, gm, wp]
    in_specs = [tok(D_MODEL), pl.BlockSpec((1, N_MOD, D_MODEL), lambda i: (row_fn(i), 0, 0))]
    in_specs += [_layer_spec(w) for w in weights]
    args = [x, mods_l] + [w[0] for w in weights]
    n_proj = wp[0].shape[2]
    out_shape = [jax.ShapeDtypeStruct((t, D_MODEL), F32)]
    out_specs = [tok(D_MODEL)]
    if mode == "even":
        out_shape.append(jax.ShapeDtypeStruct((t, n_proj), F32))
        out_specs.append(tok(n_proj))
    elif mode == "lat":
        seq_tiles = rope_tabs[0].shape[0] // tm
        for tab in rope_tabs:
            in_specs.append(pl.BlockSpec((tm, LANES), lambda i, _s=seq_tiles: (i % _s, 0)))
            args.append(tab)
        for _ in range(6):
            out_shape.append(jax.ShapeDtypeStruct((t, 512), BF16))
            out_specs.append(tok(512))
    else:
        in_specs.append(_layer_spec(wpt))
        args.append(wpt[0])
        spt = tm // seq
        for p in range(6):
            if p in (1, 2, 4):
                out_shape.append(jax.ShapeDtypeStruct((t // seq, 512, seq), F32))
                out_specs.append(pl.BlockSpec((spt, 512, seq), lambda i: (i, 0, 0)))
            else:
                out_shape.append(jax.ShapeDtypeStruct((t, 512), BF16 if p in (0, 3) else F32))
                out_specs.append(tok(512))
    c_in, c_args, c_out, c_shape = _cast_specs(casts, nt)
    return pl.pallas_call(
        functools.partial(_ka_kernel, mode=mode, ck=ck, seq=seq, n_cast=len(casts)),
        grid=(nt,),
        in_specs=in_specs + c_in,
        out_specs=out_specs + c_out,
        out_shape=out_shape + c_shape,
        compiler_params=pltpu.CompilerParams(
            dimension_semantics=("arbitrary",), vmem_limit_bytes=VMEM_LIMIT),
        name="ffn1_proj_" + mode,
    )(*args, *c_args)


HALO = 8


def _even_mix(ext_ref, base, n, pos, seq_len, pw_ref, ps_ref, cw_ref):
    def rows(j, lo, hi):
        return ext_ref[base + HALO + j:base + HALO + j + n, lo:hi]

    ya = []
    for g, win in enumerate(POOL_WINDOWS):
        half = win // 2
        lo_c, hi_c = g * POOL_G, (g + 1) * POOL_G
        s = rows(-half, lo_c, hi_c)
        for j in range(-half + 1, half):
            s = s + rows(j, lo_c, hi_c)
        lo = jnp.clip(pos - half, 0, seq_len - 1)
        hi = jnp.clip(pos + half - 1, 0, seq_len - 1)
        cnt = (hi - lo + 1).astype(F32)
        d = (s / cnt - rows(0, lo_c, hi_c)).astype(BF16)
        ya.append(_dot(d, pw_ref[g]))
    ya = jnp.concatenate(ya, axis=1) * ps_ref[...]

    def z(j):
        return rows(j, W_POOL + 2 * W_CONV, W_POOL + 3 * W_CONV) * rows(j, W_POOL, W_POOL + W_CONV)

    y = z(-1) * cw_ref[0:1, :] + z(0) * cw_ref[1:2, :] + z(1) * cw_ref[2:3, :]
    yb = rows(0, W_POOL + W_CONV, W_POOL + 2 * W_CONV) * y
    return jnp.concatenate([ya, yb], axis=1)


def _kb_kernel(*refs, mode, ck, tm, seq_len, final, n_cast):
    x_ref, mods_ref = refs[0], refs[1]
    k = 2
    if mode == "even":
        u_ref, up_ref, un_ref, pw_ref, ps_ref, cw_ref = refs[k:k + 6]
        k += 6
    else:
        n_parts = 2 if mode == "odd2" else 1
        o_refs = refs[k:k + n_parts]
        k += n_parts
    wmo_ref, g2_ref, w_in_ref, w_out_ref = refs[k:k + 4]
    k += 4
    if final:
        gf_ref = refs[k]
        k += 1
    out_ref = refs[k + n_cast]
    _side_casts(refs[k:k + n_cast], refs[k + n_cast + 1:k + 2 * n_cast + 1])
    k += 2 * n_cast + 1
    mods = mods_ref[0]
    if mode == "even":
        ext_ref = refs[k]
        n_seg = min(seq_len, tm)
        stride = n_seg + 2 * HALO
        row0 = pl.program_id(0) * tm
        feats = []
        for si in range(tm // n_seg):
            base = si * stride
            if n_seg == seq_len:
                before = after = jnp.zeros((HALO, ext_ref.shape[1]), F32)
            else:
                first = jnp.bitwise_and(row0, seq_len - 1) == 0
                last = jnp.bitwise_and(row0 + tm, seq_len - 1) == 0
                before = jnp.where(first, 0.0, up_ref[...])
                after = jnp.where(last, 0.0, un_ref[...])
            ext_ref[base:base + HALO, :] = before
            ext_ref[base + HALO:base + HALO + n_seg, :] = u_ref[si * n_seg:(si + 1) * n_seg, :]
            ext_ref[base + HALO + n_seg:base + stride, :] = after
            row = lax.broadcasted_iota(jnp.int32, (n_seg, LANES), 0) + (row0 + si * n_seg)
            pos = jnp.bitwise_and(row, seq_len - 1)
            feats.append(_even_mix(ext_ref, base, n_seg, pos, seq_len, pw_ref, ps_ref, cw_ref))
        feat = jnp.concatenate(feats, axis=0).astype(BF16)
        y = _dot(feat, wmo_ref[...])
    else:
        y = None
        off = 0
        for o_ref in o_refs:
            w = o_ref.shape[1]
            t = _dot(o_ref[...], wmo_ref[off:off + w, :])
            y = t if y is None else y + t
            off += w
    x2 = x_ref[...] + mods[5:6] * y
    x3 = _ffn_half_step(x2, mods, 6, g2_ref[...], w_in_ref, w_out_ref, ck)
    if final:
        x3 = _rms(x3) * gf_ref[...]
    out_ref[...] = x3


def _kernel_b(x, mods_l, row_fn, mix_in, mix_params, wmo, g2, w_in, w_out, mode, tm, seq_len=None,
              final_g=None, casts=(), ck=256):
    t = x.shape[0]
    nt = t // tm
    tok = lambda w: pl.BlockSpec((tm, w), lambda i: (i, 0))
    in_specs = [tok(D_MODEL), pl.BlockSpec((1, N_MOD, D_MODEL), lambda i: (row_fn(i), 0, 0))]
    args = [x, mods_l]
    scratch = []
    if mode == "even":
        u = mix_in[0]
        wu = u.shape[1]
        hb = tm // HALO
        last = t // HALO - 1
        in_specs += [
            tok(wu),
            pl.BlockSpec((HALO, wu), lambda i: (jnp.maximum(i * hb - 1, 0), 0)),
            pl.BlockSpec((HALO, wu), lambda i: (jnp.minimum((i + 1) * hb, last), 0)),
        ]
        args += [u, u, u]
        for w in mix_params:
            in_specs.append(_layer_spec(w))
            args.append(w[0])
        n_seg = min(seq_len, tm)
        assert tm % n_seg == 0 and seq_len % n_seg == 0 and seq_len & (seq_len - 1) == 0
        scratch.append(pltpu.VMEM(((tm // n_seg) * (n_seg + 2 * HALO), wu), F32))
    else:
        for o in mix_in:
            in_specs.append(tok(o.shape[1]))
            args.append(o)
    in_specs += [_layer_spec(w) for w in (wmo, g2, w_in, w_out)]
    args += [w[0] for w in (wmo, g2, w_in, w_out)]
    final = final_g is not None
    if final:
        in_specs.append(_const_spec((1, D_MODEL)))
        args.append(final_g.reshape(1, D_MODEL))
    kmode = mode if mode == "even" else ("odd2" if len(mix_in) == 2 else "odd1")
    c_in, c_args, c_out, c_shape = _cast_specs(casts, nt)
    return pl.pallas_call(
        functools.partial(_kb_kernel, mode=kmode, ck=ck, tm=tm, seq_len=seq_len, final=final, n_cast=len(casts)),
        grid=(nt,),
        in_specs=in_specs + c_in,
        out_specs=[tok(D_MODEL)] + c_out,
        out_shape=[jax.ShapeDtypeStruct((t, D_MODEL), F32)] + c_shape,
        scratch_shapes=scratch,
        compiler_params=pltpu.CompilerParams(
            dimension_semantics=("arbitrary",), vmem_limit_bytes=VMEM_LIMIT),
        name="mix_ffn2_" + kmode,
    )(*args, *c_args)


def _diff_lambda(lam_ref, lam_init):
    lp = lam_ref[...]
    s1 = jnp.sum(lp[0:1] * lp[1:2], axis=-1, keepdims=True)
    s2 = jnp.sum(lp[2:3] * lp[3:4], axis=-1, keepdims=True)
    return jnp.exp(s1) - jnp.exp(s2) + lam_init


def _half_masks():
    lane = lax.broadcasted_iota(jnp.int32, (1, LANES), 1)
    lo = lane < HEAD_DIM
    return lo, jnp.logical_not(lo)


def _diff_out_norm(o, dn_ref, lam_init):
    return (_rms(o) * dn_ref[...]) * (1.0 - lam_init)


def _ctx_attn_kernel(nq_ref, nkt_ref, nvt_ref, dq_ref, dkt_ref, dv_ref, lam_ref, dn_ref, o_ref, *, lam_init, seq):
    lo, hi = _half_masks()
    lam = _diff_lambda(lam_ref, lam_init)
    ones_t = jnp.ones((LANES, seq), BF16)
    ones_c = jnp.ones((seq, LANES), BF16)
    for b in range(nkt_ref.shape[0]):
        rows = slice(b * seq, (b + 1) * seq)
        for hp in range(H_NAT // 2):
            sl = slice(hp * LANES, (hp + 1) * LANES)
            q = nq_ref[rows, sl]
            kt = nkt_ref[b, sl, :].astype(BF16)
            vt = jnp.concatenate([nvt_ref[b, sl, :].astype(BF16), ones_t], axis=0)
            outs = []
            for msk in (lo, hi):
                qm = jnp.where(msk, q, jnp.zeros_like(q))
                s = _dot(qm, kt)
                e = jnp.exp2(s - jnp.max(s, axis=-1, keepdims=True)).astype(BF16)
                ol = _dot_nt(e, vt)
                outs.append(ol[:, :LANES] * (1.0 / ol[:, LANES:]))
            o_ref[rows, sl] = jnp.where(lo, outs[0], outs[1]).astype(o_ref.dtype)
        for h in range(H_DIFF):
            sl = slice(h * LANES, (h + 1) * LANES)
            q = dq_ref[rows, sl]
            kt = dkt_ref[b, sl, :].astype(BF16)
            v = dv_ref[rows, sl].astype(BF16)
            es, rs = [], []
            for msk in (lo, hi):
                qm = jnp.where(msk, q, jnp.zeros_like(q))
                s = _dot(qm, kt)
                e = jnp.exp2(s - jnp.max(s, axis=-1, keepdims=True))
                es.append(e)
                rs.append(_dot(e.astype(BF16), ones_c))
            r0 = 1.0 / rs[0]
            cf = lam * rs[0] * (1.0 / rs[1])
            cfw = jnp.concatenate([cf] * (seq // LANES), axis=1)
            a = (es[0] - cfw * es[1]).astype(BF16)
            o = _diff_out_norm(_dot(a, v) * r0, dn_ref, lam_init)
            o_ref[rows, W_NAT + h * LANES:W_NAT + (h + 1) * LANES] = o.astype(o_ref.dtype)


def _ctx_attention(parts, lam_p, dnorm, seq, lam_init, nb=4):
    nq, nkt, nvt, dq, dkt, dv = parts
    t = nq.shape[0]
    steps = t // (seq * nb)
    tok = pl.BlockSpec((nb * seq, 512), lambda i: (i, 0))
    fmaj = pl.BlockSpec((nb, 512, seq), lambda i: (i, 0, 0))
    return pl.pallas_call(
        functools.partial(_ctx_attn_kernel, lam_init=lam_init, seq=seq),
        grid=(steps,),
        in_specs=[tok, fmaj, fmaj, tok, fmaj, tok, _const_spec((4, HEAD_DIM)), _const_spec((1, 2 * HEAD_DIM))],
        out_specs=pl.BlockSpec((nb * seq, D_MODEL), lambda i: (i, 0)),
        out_shape=jax.ShapeDtypeStruct((t, D_MODEL), BF16),
        compiler_params=pltpu.CompilerParams(
            dimension_semantics=("arbitrary",), vmem_limit_bytes=VMEM_LIMIT),
        name="ctx_attention",
    )(nq, nkt, nvt, dq, dkt, dv, lam_p, dnorm.reshape(1, 2 * HEAD_DIM))


NAT_QROWS = 8
NAT_KROWS = 16
NAT_NDR = 2 * NAT_WIN_R


def _build_bias_tiles(rpb_ref, u_ref):
    c = lax.broadcasted_iota(jnp.int32, (GRID_W, LANES), 0)
    lane = lax.broadcasted_iota(jnp.int32, (GRID_W, LANES), 1)
    kc = jnp.bitwise_and(lane, GRID_W - 1)
    c0 = jnp.clip(c - NAT_WIN_C // 2, 0, GRID_W - NAT_WIN_C)
    col_ok = jnp.logical_and(kc >= c0, kc < c0 + NAT_WIN_C)
    left = lane < GRID_W
    n_dr = 2 * NAT_WIN_R - 1
    for h in range(H_NAT):
        tl, tr = [], []
        for dr in range(n_dr):
            row = jnp.broadcast_to(rpb_ref[h, dr:dr + 1, :] * LOG2E, (GRID_W, LANES))
            tl.append(pltpu.roll(row, LANES - (NAT_WIN_C - 1), 1, stride=1, stride_axis=0))
            tr.append(pltpu.roll(row, LANES - (NAT_WIN_C - 1) + GRID_W, 1, stride=1, stride_axis=0))
        for di in range(NAT_NDR):
            dl = di - NAT_WIN_R + (NAT_WIN_R - 1)
            drr = dl + 1
            lv = tl[dl] if 0 <= dl < n_dr else jnp.full((GRID_W, LANES), NEG, F32)
            rv = tr[drr] if 0 <= drr < n_dr else jnp.full((GRID_W, LANES), NEG, F32)
            u_ref[h, di] = jnp.where(col_ok, jnp.where(left, lv, rv), NEG)


def _nat_lat_kernel(q_ref, k_ref, v_ref, kc_ref, vc_ref, rpb_ref, o_ref, u_ref, s_ref):
    b = pl.program_id(1)

    @pl.when(jnp.logical_and(pl.program_id(0) == 0, b == 0))
    def _():
        _build_bias_tiles(rpb_ref, u_ref)

    rows = k_ref.shape[0] // GRID_W
    kb = jnp.clip(b * NAT_QROWS - NAT_WIN_R // 2, 0, rows - NAT_KROWS)
    koff = pl.multiple_of(kb * GRID_W, GRID_W)
    lo, hi = _half_masks()
    left = lax.broadcasted_iota(jnp.int32, (GRID_W, LANES), 1) < GRID_W
    nq = NAT_QROWS * GRID_W
    for hp in range(H_NAT // 2):
        sl = slice(hp * LANES, (hp + 1) * LANES)
        q = q_ref[:, sl]
        k = k_ref[pl.ds(koff, NAT_KROWS * GRID_W), sl]
        v = v_ref[pl.ds(koff, NAT_KROWS * GRID_W), sl]
        kc = kc_ref[:, sl]
        vc = vc_ref[:, sl]
        v1 = jnp.concatenate([v, jnp.ones_like(v)], axis=1)
        vc1 = jnp.concatenate([vc, jnp.ones_like(vc)], axis=1)
        outs = []
        for e, msk in enumerate((lo, hi)):
            h = 2 * hp + e
            qm = jnp.where(msk, q, jnp.zeros_like(q))
            s_ref[...] = _dot_nt(qm, k)
            for i in range(NAT_QROWS):
                r = b * NAT_QROWS + i
                r0 = jnp.clip(r - NAT_WIN_R // 2, 0, rows - NAT_WIN_R)
                for jp in range(NAT_KROWS // 2):
                    kl = kb + 2 * jp
                    ok_l = jnp.logical_and(kl >= r0, kl < r0 + NAT_WIN_R)
                    ok_r = jnp.logical_and(kl + 1 >= r0, kl + 1 < r0 + NAT_WIN_R)
                    di = jnp.clip(kl - r + NAT_WIN_R, 0, NAT_NDR - 1)
                    ok = jnp.where(left, ok_l.astype(jnp.int32), ok_r.astype(jnp.int32))
                    bias = jnp.where(ok > 0, u_ref[h, di], NEG)
                    tile = (slice(i * GRID_W, (i + 1) * GRID_W), slice(jp * LANES, (jp + 1) * LANES))
                    s_ref[tile] = s_ref[tile] + bias
            s_ctx = _dot_nt(qm, kc)
            s_loc = s_ref[...]
            m = jnp.maximum(jnp.max(s_loc, axis=-1, keepdims=True), jnp.max(s_ctx, axis=-1, keepdims=True))
            ol = (_dot(jnp.exp2(s_loc - m).astype(BF16), v1) + _dot(jnp.exp2(s_ctx - m).astype(BF16), vc1))
            outs.append(ol[:, :LANES] * (1.0 / ol[:, LANES:]))
        o_ref[:, sl] = jnp.where(lo, outs[0], outs[1]).astype(o_ref.dtype)


def _nat_lat_attention(q, k, v, kc, vc, rpb_pad, batch, seq):
    nq = NAT_QROWS * GRID_W
    nblk = seq // nq
    p = kc.shape[1]
    return pl.pallas_call(
        _nat_lat_kernel,
        grid=(batch, nblk),
        in_specs=[
            pl.BlockSpec((nq, W_NAT), lambda bb, i: (bb * nblk + i, 0)),
            pl.BlockSpec((seq, W_NAT), lambda bb, i: (bb, 0)),
            pl.BlockSpec((seq, W_NAT), lambda bb, i: (bb, 0)),
            pl.BlockSpec((None, p, W_NAT), lambda bb, i: (bb, 0, 0)),
            pl.BlockSpec((None, p, W_NAT), lambda bb, i: (bb, 0, 0)),
            pl.BlockSpec(rpb_pad.shape, lambda bb, i: (0, 0, 0)),
        ],
        out_specs=pl.BlockSpec((nq, W_NAT), lambda bb, i: (bb * nblk + i, 0)),
        out_shape=jax.ShapeDtypeStruct((batch * seq, W_NAT), BF16),
        scratch_shapes=[
            pltpu.VMEM((H_NAT, NAT_NDR, GRID_W, LANES), F32),
            pltpu.VMEM((nq, NAT_KROWS * GRID_W), F32),
        ],
        compiler_params=pltpu.CompilerParams(
            dimension_semantics=("arbitrary", "arbitrary"), vmem_limit_bytes=VMEM_LIMIT),
        name="nat_latent_attention",
    )(q, k, v, kc, vc, rpb_pad)


DIFF_KCHUNK = 256


def _lane_fold(x, op):
    out = x[:, :LANES]
    for c in range(1, x.shape[1] // LANES):
        out = op(out, x[:, c * LANES:(c + 1) * LANES])
    return out


def _round_robin(*gens):
    gens = list(gens)
    while gens:
        for g in list(gens):
            try:
                next(g)
            except StopIteration:
                gens.remove(g)


def _diff_lat_kernel(q_ref, k_ref, v_ref, kc_ref, vc_ref, lam_ref, dn_ref, o_ref, sa_ref, sb_ref, *, lam_init):
    lo, hi = _half_masks()
    lam = _diff_lambda(lam_ref, lam_init)
    ck = DIFF_KCHUNK
    pieces = ([(k_ref, v_ref, c * ck) for c in range(k_ref.shape[0] // ck)]
              + [(kc_ref, vc_ref, c * ck) for c in range(kc_ref.shape[0] // ck)])
    row_max, row_sum = {}, {}

    def s_of(h):
        return (sa_ref, sb_ref)[h % 2]

    def scores(h):
        sl = slice(h * LANES, (h + 1) * LANES)
        q = q_ref[:, sl]
        ms = []
        for j, msk in enumerate((lo, hi)):
            qm = jnp.where(msk, q, jnp.zeros_like(q))
            mx = None
            for ci, (kr, _, off) in enumerate(pieces):
                s = _dot_nt(qm, kr[off:off + ck, sl])
                s_of(h)[j, :, ci * ck:(ci + 1) * ck] = s
                part = _lane_fold(s, jnp.maximum)
                mx = part if mx is None else jnp.maximum(mx, part)
                yield
            ms.append(jnp.max(mx, axis=-1, keepdims=True))
        row_max[h] = ms

    def exps(h):
        ls = []
        for j in range(2):
            acc = None
            for ci in range(len(pieces)):
                tile = (j, slice(None), slice(ci * ck, (ci + 1) * ck))
                e = jnp.exp2(s_of(h)[tile] - row_max[h][j])
                s_of(h)[tile] = e
                part = _lane_fold(e, jnp.add)
                acc = part if acc is None else acc + part
                yield
            ls.append(jnp.sum(acc, axis=-1, keepdims=True))
        row_sum[h] = ls

    def values(h):
        sl = slice(h * LANES, (h + 1) * LANES)
        l0, l1 = row_sum[h]
        cf = lam * l0 * (1.0 / l1)
        o = None
        for ci, (_, vr, off) in enumerate(pieces):
            cols = slice(ci * ck, (ci + 1) * ck)
            a = (s_of(h)[0, :, cols] - cf * s_of(h)[1, :, cols]).astype(BF16)
            t = _dot(a, vr[off:off + ck, sl])
            o = t if o is None else o + t
            yield
        o = o * (1.0 / l0)
        o_ref[:, sl] = _diff_out_norm(o, dn_ref, lam_init).astype(o_ref.dtype)

    _round_robin(scores(0))
    for h in range(H_DIFF):
        nxt = [scores(h + 1)] if h + 1 < H_DIFF else []
        _round_robin(exps(h), *nxt)
        _round_robin(values(h))


def _diff_lat_attention(q, k, v, kc, vc, lam_p, dnorm, batch, seq, lam_init, tq=256):
    nqb = seq // tq
    p = kc.shape[1]
    s_scratch = pltpu.VMEM((2, tq, seq + p), F32)
    return pl.pallas_call(
        functools.partial(_diff_lat_kernel, lam_init=lam_init),
        grid=(batch, nqb),
        in_specs=[
            pl.BlockSpec((tq, W_DIFF), lambda bb, i: (bb * nqb + i, 0)),
            pl.BlockSpec((seq, W_DIFF), lambda bb, i: (bb, 0)),
            pl.BlockSpec((seq, W_DIFF), lambda bb, i: (bb, 0)),
            pl.BlockSpec((None, p, W_DIFF), lambda bb, i: (bb, 0, 0)),
            pl.BlockSpec((None, p, W_DIFF), lambda bb, i: (bb, 0, 0)),
            pl.BlockSpec((4, HEAD_DIM), lambda bb, i: (0, 0)),
            pl.BlockSpec((1, 2 * HEAD_DIM), lambda bb, i: (0, 0)),
        ],
        out_specs=pl.BlockSpec((tq, W_DIFF), lambda bb, i: (bb * nqb + i, 0)),
        out_shape=jax.ShapeDtypeStruct((batch * seq, W_DIFF), BF16),
        scratch_shapes=[s_scratch, s_scratch],
        compiler_params=pltpu.CompilerParams(
            dimension_semantics=("arbitrary", "arbitrary"), vmem_limit_bytes=VMEM_LIMIT),
        name="diff_latent_attention",
    )(q, k, v, kc, vc, lam_p, dnorm.reshape(1, 2 * HEAD_DIM))


def _rope_tables(n):
    t = np.arange(n)
    row = (t // GRID_W).astype(np.float64)
    col = (t % GRID_W).astype(np.float64)
    quarter = HEAD_DIM // 4
    inv = 1.0 / (ROPE_THETA ** (np.arange(quarter) / quarter))
    cr, sr = np.cos(row[:, None] * inv[None]), np.sin(row[:, None] * inv[None])
    cc, sc = np.cos(col[:, None] * inv[None]), np.sin(col[:, None] * inv[None])
    zero = np.zeros_like(sr)
    cos_t = np.concatenate([cr, cr, cc, cc], axis=1)
    sin_a = np.concatenate([-sr, zero, -sc, zero], axis=1)
    sin_b = np.concatenate([zero, sr, zero, sc], axis=1)
    tile = lambda a: jnp.asarray(np.concatenate([a, a], axis=1).astype(np.float32))
    return tile(cos_t), tile(sin_a), tile(sin_b)


def kernel(x_prompt, x_sample, cache_nat_k, cache_nat_v, cache_diff_k, cache_diff_v, c, c_ctx, mod_w, mod_b, norm_ffn1, ffn1_w_in, ffn1_w_out, norm_mix, even_w_in, pool_w, pool_scale, conv_w, odd_w_in, nat_rpb, diff_lambda, diff_norm, mix_w_out, norm_ffn2, ffn2_w_in, ffn2_w_out, final_norm):
    batch, seq, _ = x_prompt.shape
    dbatch, dseq, _ = x_sample.shape
    past = cache_nat_k.shape[2]
    tm = 512

    cond = jnp.zeros((SUBLANES, D_MODEL), F32).at[0].set(c_ctx).at[1:1 + dbatch].set(c)
    mods = _adaln_mods(cond, mod_w, mod_b).reshape(DEPTH, SUBLANES, N_MOD, D_MODEL)

    ctx_row = lambda i: 0
    lat_row = lambda i: 1 + (i * tm) // dseq

    cache_bf = lambda a, w: a[:, 0].reshape(dbatch, past, w).astype(BF16)
    gain = lambda g: g.reshape(g.shape[0], 1, g.shape[1])
    g_ffn1, g_mix, g_ffn2 = gain(norm_ffn1), gain(norm_mix), gain(norm_ffn2)
    pool3 = pool_w.reshape(pool_w.shape[0], len(POOL_WINDOWS) * POOL_G, POOL_G)

    f1_in0, f1_out0, w_even0 = _cast_call([(ffn1_w_in, 0), (ffn1_w_out, 0), (even_w_in, 0)])
    w_odd_t = jnp.swapaxes(odd_w_in, 1, 2).astype(BF16)

    ctx = x_prompt.reshape(batch * seq, D_MODEL)
    lat = x_sample.reshape(dbatch * dseq, D_MODEL)

    l = 0
    x1c, uc, f2_in0, f2_out0, wmo0, pool0 = _kernel_a(
        ctx, mods[l], ctx_row, (g_ffn1, l), (f1_in0, 0), (f1_out0, 0), (g_mix, l), (w_even0, 0), "even", tm,
        casts=[(ffn2_w_in, 0), (ffn2_w_out, 0), (mix_w_out, 0), (pool3, 0)])
    even_params = ((pool0.reshape(1, len(POOL_WINDOWS), POOL_G, POOL_G), 0), (gain(pool_scale), 0), (conv_w, 0))
    ctx, f1_in1, f1_out1 = _kernel_b(
        x1c, mods[l], ctx_row, [uc], even_params, (wmo0, 0), (g_ffn2, l), (f2_in0, 0), (f2_out0, 0), "even", tm,
        seq_len=seq, casts=[(ffn1_w_in, 1), (ffn1_w_out, 1)])
    x1l, ul, w_odd1, f2_in1 = _kernel_a(
        lat, mods[l], lat_row, (g_ffn1, l), (f1_in0, 0), (f1_out0, 0), (g_mix, l), (w_even0, 0), "even", tm,
        casts=[(odd_w_in, 0), (ffn2_w_in, 1)])
    lat, f2_out1, wmo1 = _kernel_b(
        x1l, mods[l], lat_row, [ul], even_params, (wmo0, 0), (g_ffn2, l), (f2_in0, 0), (f2_out0, 0), "even", tm,
        seq_len=dseq, casts=[(ffn2_w_out, 1), (mix_w_out, 1)])

    l = 1
    lam_init = 0.8 - 0.6 * math.exp(-0.3 * l)
    ffn1 = ((g_ffn1, l), (f1_in1, 0), (f1_out1, 0), (g_mix, l), (w_odd1, 0))
    ffn2 = ((wmo1, 0), (g_ffn2, l), (f2_in1, 0), (f2_out1, 0))
    ctx1, nq, nkt, nvt, dq, dkt, dv = _kernel_a(ctx, mods[l], ctx_row, *ffn1, "ctx", tm, wpt=(w_odd_t, 0), seq=seq)
    o_ctx = _ctx_attention([nq, nkt, nvt, dq, dkt, dv], diff_lambda[0], diff_norm[0], seq, lam_init)
    y_prompt, = _kernel_b(ctx1, mods[l], ctx_row, [o_ctx], None, *ffn2, "odd", tm, final_g=final_norm)

    lat1, lq, lk, lv, ldq, ldk, ldv = _kernel_a(lat, mods[l], lat_row, *ffn1, "lat", tm,
                                                 rope_tabs=_rope_tables(dseq))
    rpb_pad = jnp.zeros((H_NAT, 2 * NAT_WIN_R, LANES), F32).at[:, :2 * NAT_WIN_R - 1,
                                                                :2 * NAT_WIN_C - 1].set(nat_rpb[0])
    o_nat = _nat_lat_attention(lq, lk, lv, cache_bf(cache_nat_k, W_NAT), cache_bf(cache_nat_v, W_NAT),
                               rpb_pad, dbatch, dseq)
    o_diff = _diff_lat_attention(ldq, ldk, ldv, cache_bf(cache_diff_k, W_DIFF), cache_bf(cache_diff_v, W_DIFF),
                                 diff_lambda[0], diff_norm[0], dbatch, dseq, lam_init)
    y_sample, = _kernel_b(lat1, mods[l], lat_row, [o_nat, o_diff], None, *ffn2, "odd", tm, final_g=final_norm)

    new_nat_k = nkt.reshape(batch, 1, H_NAT, HEAD_DIM, seq).transpose(0, 1, 4, 2, 3)
    new_nat_v = nvt.reshape(batch, 1, H_NAT, HEAD_DIM, seq).transpose(0, 1, 4, 2, 3)
    new_diff_k = dkt.reshape(batch, 1, H_DIFF, 2, HEAD_DIM, seq).transpose(0, 1, 5, 2, 3, 4)
    new_diff_v = dv.reshape(batch, 1, seq, H_DIFF, 2 * HEAD_DIM)
    return (y_prompt.reshape(batch, seq, D_MODEL), y_sample.reshape(dbatch, dseq, D_MODEL),
            new_nat_k, new_nat_v, new_diff_k, new_diff_v)
```

```python
import functools
import math

import numpy as np
import jax
import jax.numpy as jnp
from jax import lax
from jax.experimental import pallas as pl
from jax.experimental.pallas import tpu as pltpu

D_MODEL = 1024
D_FF = 2816
N_MOD = 9
DEPTH = 2
GRID_W = 64
HEAD_DIM = 64
W_POOL = 512
W_CONV = 512
POOL_WINDOWS = (2, 4, 8, 16)
POOL_G = 128
W_NAT = 512
W_DIFF = 512
H_NAT = 8
H_DIFF = 4
NAT_WIN_R = 8
NAT_WIN_C = 16
ROPE_THETA = 10000.0
ATTN_SCALE = HEAD_DIM ** -0.5
LOG2E = math.log2(math.e)
Q_SCALE = ATTN_SCALE * LOG2E
EPS = 1e-6
NEG = -1e30

LANES = 128
SUBLANES = 8
VMEM_LIMIT = 60 * 1024 * 1024

BF16 = jnp.bfloat16
F32 = jnp.float32


def _dot(a, b):
    return jnp.dot(a, b, preferred_element_type=F32)


def _dot_nt(a, b):
    return lax.dot_general(a, b, (((1,), (1,)), ((), ())), preferred_element_type=F32)


def _sigmoid(x):
    return 1.0 / (1.0 + jnp.exp(-x))


def _rms(x):
    return x * lax.rsqrt(jnp.mean(x * x, axis=-1, keepdims=True) + EPS)


def _modnorm(x, shift, scale, g):
    return _rms(x) * (g * (1.0 + scale)) + shift


def _ffn_half_step(x, mods, base, g, w_in_ref, w_out_ref, ck):
    shift, scale, gate = mods[base:base + 1], mods[base + 1:base + 2], mods[base + 2:base + 3]
    h = _modnorm(x, shift, scale, g).astype(BF16)
    acc = None
    for c in range(D_FF // ck):
        a = _dot(h, w_in_ref[:, c * ck:(c + 1) * ck])
        b = _dot(h, w_in_ref[:, D_FF + c * ck:D_FF + (c + 1) * ck])
        gated = ((a * _sigmoid(a)) * b).astype(BF16)
        t = _dot(gated, w_out_ref[c * ck:(c + 1) * ck, :])
        acc = t if acc is None else acc + t
    return x + (0.5 * gate) * acc


def _const_spec(shape):
    nd = len(shape)
    return pl.BlockSpec(shape, lambda i, _nd=nd: (0,) * _nd, pipeline_mode=pl.Buffered(1))


def _mods_kernel(cond_ref, w_ref, b_ref, o_ref):
    cnd = cond_ref[...]
    s = (cnd * _sigmoid(cnd)).astype(BF16)
    o_ref[0] = _dot(s, w_ref[0].astype(BF16)) + b_ref[0]


def _adaln_mods(cond, mod_w, mod_b):
    tn = 2304
    nt = (N_MOD * D_MODEL) // tn
    return pl.pallas_call(
        _mods_kernel,
        grid=(DEPTH, nt),
        in_specs=[
            pl.BlockSpec((SUBLANES, D_MODEL), lambda l, j: (0, 0)),
            pl.BlockSpec((1, D_MODEL, tn), lambda l, j: (l, 0, j)),
            pl.BlockSpec((1, 1, tn), lambda l, j: (l, 0, j)),
        ],
        out_specs=pl.BlockSpec((1, SUBLANES, tn), lambda l, j: (l, 0, j)),
        out_shape=jax.ShapeDtypeStruct((DEPTH, SUBLANES, N_MOD * D_MODEL), F32),
        compiler_params=pltpu.CompilerParams(vmem_limit_bytes=VMEM_LIMIT),
        name="adaln_mods",
    )(cond, mod_w, mod_b.reshape(DEPTH, 1, N_MOD * D_MODEL))


def _rope(x, cos_t, sin_a, sin_b):
    w = x.shape[1]
    reps = w // LANES
    c = jnp.concatenate([cos_t] * reps, axis=1)
    sa = jnp.concatenate([sin_a] * reps, axis=1)
    sb = jnp.concatenate([sin_b] * reps, axis=1)
    quarter = HEAD_DIM // 4
    up = pltpu.roll(x, w - quarter, 1)
    dn = pltpu.roll(x, quarter, 1)
    return x * c + up * sa + dn * sb


def _side_casts(cast_ins, cast_outs):
    for ci, co in zip(cast_ins, cast_outs):
        co[0] = ci[...].astype(BF16)


def _ka_kernel(*refs, mode, ck, seq, n_cast):
    n_in = {"even": 7, "ctx": 8, "lat": 10}[mode]
    x_ref, mods_ref, g1_ref, w_in_ref, w_out_ref, gm_ref, wp_ref = refs[:7]
    outs = refs[n_in + n_cast:]
    _side_casts(refs[n_in:n_in + n_cast], outs[len(outs) - n_cast:])
    mods = mods_ref[0]
    x1 = _ffn_half_step(x_ref[...], mods, 0, g1_ref[...], w_in_ref, w_out_ref, ck)
    outs[0][...] = x1
    h = _modnorm(x1, mods[3:4], mods[4:5], gm_ref[...]).astype(BF16)
    if mode == "even":
        outs[1][...] = _dot(h, wp_ref[...])
        return
    if mode == "lat":
        cos_ref, sa_ref, sb_ref = refs[7:10]
    else:
        wpt_ref = refs[7]
    for p in range(6):
        if mode == "ctx" and p in (1, 2, 4):
            ut = _dot_nt(wpt_ref[p * 512:(p + 1) * 512, :], h)
            for bb in range(ut.shape[1] // seq):
                outs[1 + p][bb] = ut[:, bb * seq:(bb + 1) * seq]
            continue
        u = _dot(h, wp_ref[:, p * 512:(p + 1) * 512])
        if mode == "lat" and p in (3, 4):
            u = _rope(u, cos_ref[...], sa_ref[...], sb_ref[...])
        if p in (0, 3):
            u = u * Q_SCALE
        if mode == "ctx" and p == 5:
            for hh in range(H_DIFF):
                outs[1 + p][:, hh, :] = u[:, hh * LANES:(hh + 1) * LANES]
            outs[2 + p][...] = u.astype(BF16)
            continue
        outs[1 + p][...] = u.astype(outs[1 + p].dtype)


def _layer_spec(wl):
    w, l = wl
    nd = w.ndim - 1
    return pl.BlockSpec((None,) + w.shape[1:], lambda i, _l=l, _nd=nd: (_l,) + (0,) * _nd,
                        pipeline_mode=pl.Buffered(1))


def _cast_specs(casts, nt):
    in_specs, args, out_specs, out_shape = [], [], [], []
    for w, l in casts:
        _, rows, cols = w.shape
        rb = rows // nt
        assert rows % nt == 0 and rb % (2 * SUBLANES) == 0
        in_specs.append(pl.BlockSpec((None, rb, cols), lambda i, _l=l: (_l, i, 0)))
        args.append(w)
        out_specs.append(pl.BlockSpec((1, rb, cols), lambda i: (0, i, 0)))
        out_shape.append(jax.ShapeDtypeStruct((1, rows, cols), BF16))
    return in_specs, args, out_specs, out_shape


def _cast_kernel(*refs):
    n = len(refs) // 2
    _side_casts(refs[:n], refs[n:])


def _cast_call(casts, nt=8):
    c_in, c_args, c_out, c_shape = _cast_specs(casts, nt)
    return pl.pallas_call(
        _cast_kernel, grid=(nt,), in_specs=c_in, out_specs=c_out, out_shape=c_shape,
        compiler_params=pltpu.CompilerParams(dimension_semantics=("arbitrary",), vmem_limit_bytes=VMEM_LIMIT),
        name="cast_weights",
    )(*c_args)


def _kernel_a(x, mods_l, row_fn, g1, w_in, w_out, gm, wp, mode, tm, rope_tabs=None, wpt=None, seq=None,
              casts=(), ck=256):
    t = x.shape[0]
    nt = t // tm
    tok = lambda w: pl.BlockSpec((tm, w), lambda i: (i, 0))
    weights = [g1, w_in, w_out, gm, wp]
    in_specs = [tok(D_MODEL), pl.BlockSpec((1, N_MOD, D_MODEL), lambda i: (row_fn(i), 0, 0))]
    in_specs += [_layer_spec(w) for w in weights]
    args = [x, mods_l] + [w[0] for w in weights]
    n_proj = wp[0].shape[2]
    out_shape = [jax.ShapeDtypeStruct((t, D_MODEL), F32)]
    out_specs = [tok(D_MODEL)]
    if mode == "even":
        out_shape.append(jax.ShapeDtypeStruct((t, n_proj), F32))
        out_specs.append(tok(n_proj))
    elif mode == "lat":
        seq_tiles = rope_tabs[0].shape[0] // tm
        for tab in rope_tabs:
            in_specs.append(pl.BlockSpec((tm, LANES), lambda i, _s=seq_tiles: (i % _s, 0)))
            args.append(tab)
        for _ in range(6):
            out_shape.append(jax.ShapeDtypeStruct((t, 512), BF16))
            out_specs.append(tok(512))
    else:
        in_specs.append(_layer_spec(wpt))
        args.append(wpt[0])
        spt = tm // seq
        for p in range(6):
            if p in (1, 2, 4):
                out_shape.append(jax.ShapeDtypeStruct((t // seq, 512, seq), F32))
                out_specs.append(pl.BlockSpec((spt, 512, seq), lambda i: (i, 0, 0)))
            elif p == 5:
                out_shape.append(jax.ShapeDtypeStruct((t, H_DIFF, 2 * HEAD_DIM), F32))
                out_specs.append(pl.BlockSpec((tm, H_DIFF, 2 * HEAD_DIM), lambda i: (i, 0, 0)))
                out_shape.append(jax.ShapeDtypeStruct((t, 512), BF16))
                out_specs.append(tok(512))
            else:
                out_shape.append(jax.ShapeDtypeStruct((t, 512), BF16))
                out_specs.append(tok(512))
    c_in, c_args, c_out, c_shape = _cast_specs(casts, nt)
    return pl.pallas_call(
        functools.partial(_ka_kernel, mode=mode, ck=ck, seq=seq, n_cast=len(casts)),
        grid=(nt,),
        in_specs=in_specs + c_in,
        out_specs=out_specs + c_out,
        out_shape=out_shape + c_shape,
        compiler_params=pltpu.CompilerParams(
            dimension_semantics=("arbitrary",), vmem_limit_bytes=VMEM_LIMIT),
        name="ffn1_proj_" + mode,
    )(*args, *c_args)


HALO = 8


def _even_mix(ext_ref, base, n, pos, seq_len, pw_ref, ps_ref, cw_ref):
    def rows(j, lo, hi):
        return ext_ref[base + HALO + j:base + HALO + j + n, lo:hi]

    ya = []
    for g, win in enumerate(POOL_WINDOWS):
        half = win // 2
        lo_c, hi_c = g * POOL_G, (g + 1) * POOL_G
        s = rows(-half, lo_c, hi_c)
        for j in range(-half + 1, half):
            s = s + rows(j, lo_c, hi_c)
        lo = jnp.clip(pos - half, 0, seq_len - 1)
        hi = jnp.clip(pos + half - 1, 0, seq_len - 1)
        cnt = (hi - lo + 1).astype(F32)
        d = (s / cnt - rows(0, lo_c, hi_c)).astype(BF16)
        ya.append(_dot(d, pw_ref[g]))
    ya = jnp.concatenate(ya, axis=1) * ps_ref[...]

    def z(j):
        return rows(j, W_POOL + 2 * W_CONV, W_POOL + 3 * W_CONV) * rows(j, W_POOL, W_POOL + W_CONV)

    y = z(-1) * cw_ref[0:1, :] + z(0) * cw_ref[1:2, :] + z(1) * cw_ref[2:3, :]
    yb = rows(0, W_POOL + W_CONV, W_POOL + 2 * W_CONV) * y
    return jnp.concatenate([ya, yb], axis=1)


def _kb_kernel(*refs, mode, ck, tm, seq_len, final, n_cast):
    x_ref, mods_ref = refs[0], refs[1]
    k = 2
    if mode == "even":
        u_ref, up_ref, un_ref, pw_ref, ps_ref, cw_ref = refs[k:k + 6]
        k += 6
    else:
        n_parts = 2 if mode == "odd2" else 1
        o_refs = refs[k:k + n_parts]
        k += n_parts
    wmo_ref, g2_ref, w_in_ref, w_out_ref = refs[k:k + 4]
    k += 4
    if final:
        gf_ref = refs[k]
        k += 1
    out_ref = refs[k + n_cast]
    _side_casts(refs[k:k + n_cast], refs[k + n_cast + 1:k + 2 * n_cast + 1])
    k += 2 * n_cast + 1
    mods = mods_ref[0]
    if mode == "even":
        ext_ref = refs[k]
        n_seg = min(seq_len, tm)
        stride = n_seg + 2 * HALO
        row0 = pl.program_id(0) * tm
        feats = []
        for si in range(tm // n_seg):
            base = si * stride
            if n_seg == seq_len:
                before = after = jnp.zeros((HALO, ext_ref.shape[1]), F32)
            else:
                first = jnp.bitwise_and(row0, seq_len - 1) == 0
                last = jnp.bitwise_and(row0 + tm, seq_len - 1) == 0
                before = jnp.where(first, 0.0, up_ref[...])
                after = jnp.where(last, 0.0, un_ref[...])
            ext_ref[base:base + HALO, :] = before
            ext_ref[base + HALO:base + HALO + n_seg, :] = u_ref[si * n_seg:(si + 1) * n_seg, :]
            ext_ref[base + HALO + n_seg:base + stride, :] = after
            row = lax.broadcasted_iota(jnp.int32, (n_seg, LANES), 0) + (row0 + si * n_seg)
            pos = jnp.bitwise_and(row, seq_len - 1)
            feats.append(_even_mix(ext_ref, base, n_seg, pos, seq_len, pw_ref, ps_ref, cw_ref))
        feat = jnp.concatenate(feats, axis=0).astype(BF16)
        y = _dot(feat, wmo_ref[...])
    else:
        y = None
        off = 0
        for o_ref in o_refs:
            w = o_ref.shape[1]
            t = _dot(o_ref[...], wmo_ref[off:off + w, :])
            y = t if y is None else y + t
            off += w
    x2 = x_ref[...] + mods[5:6] * y
    x3 = _ffn_half_step(x2, mods, 6, g2_ref[...], w_in_ref, w_out_ref, ck)
    if final:
        x3 = _rms(x3) * gf_ref[...]
    out_ref[...] = x3


def _kernel_b(x, mods_l, row_fn, mix_in, mix_params, wmo, g2, w_in, w_out, mode, tm, seq_len=None,
              final_g=None, casts=(), ck=256):
    t = x.shape[0]
    nt = t // tm
    tok = lambda w: pl.BlockSpec((tm, w), lambda i: (i, 0))
    in_specs = [tok(D_MODEL), pl.BlockSpec((1, N_MOD, D_MODEL), lambda i: (row_fn(i), 0, 0))]
    args = [x, mods_l]
    scratch = []
    if mode == "even":
        u = mix_in[0]
        wu = u.shape[1]
        hb = tm // HALO
        last = t // HALO - 1
        in_specs += [
            tok(wu),
            pl.BlockSpec((HALO, wu), lambda i: (jnp.maximum(i * hb - 1, 0), 0)),
            pl.BlockSpec((HALO, wu), lambda i: (jnp.minimum((i + 1) * hb, last), 0)),
        ]
        args += [u, u, u]
        for w in mix_params:
            in_specs.append(_layer_spec(w))
            args.append(w[0])
        n_seg = min(seq_len, tm)
        assert tm % n_seg == 0 and seq_len % n_seg == 0 and seq_len & (seq_len - 1) == 0
        scratch.append(pltpu.VMEM(((tm // n_seg) * (n_seg + 2 * HALO), wu), F32))
    else:
        for o in mix_in:
            in_specs.append(tok(o.shape[1]))
            args.append(o)
    in_specs += [_layer_spec(w) for w in (wmo, g2, w_in, w_out)]
    args += [w[0] for w in (wmo, g2, w_in, w_out)]
    final = final_g is not None
    if final:
        in_specs.append(_const_spec((1, D_MODEL)))
        args.append(final_g.reshape(1, D_MODEL))
    kmode = mode if mode == "even" else ("odd2" if len(mix_in) == 2 else "odd1")
    c_in, c_args, c_out, c_shape = _cast_specs(casts, nt)
    return pl.pallas_call(
        functools.partial(_kb_kernel, mode=kmode, ck=ck, tm=tm, seq_len=seq_len, final=final, n_cast=len(casts)),
        grid=(nt,),
        in_specs=in_specs + c_in,
        out_specs=[tok(D_MODEL)] + c_out,
        out_shape=[jax.ShapeDtypeStruct((t, D_MODEL), F32)] + c_shape,
        scratch_shapes=scratch,
        compiler_params=pltpu.CompilerParams(
            dimension_semantics=("arbitrary",), vmem_limit_bytes=VMEM_LIMIT),
        name="mix_ffn2_" + kmode,
    )(*args, *c_args)


def _diff_lambda(lam_ref, lam_init):
    lp = lam_ref[...]
    s1 = jnp.sum(lp[0:1] * lp[1:2], axis=-1, keepdims=True)
    s2 = jnp.sum(lp[2:3] * lp[3:4], axis=-1, keepdims=True)
    return jnp.exp(s1) - jnp.exp(s2) + lam_init


def _half_masks():
    lane = lax.broadcasted_iota(jnp.int32, (1, LANES), 1)
    lo = lane < HEAD_DIM
    return lo, jnp.logical_not(lo)


def _diff_out_norm(o, dn_ref, lam_init):
    return (_rms(o) * dn_ref[...]) * (1.0 - lam_init)


def _ctx_attn_kernel(nq_ref, nkt_ref, nvt_ref, dq_ref, dkt_ref, dv_ref, lam_ref, dn_ref, o_ref, *, lam_init, seq):
    lo, hi = _half_masks()
    lam = _diff_lambda(lam_ref, lam_init)
    ones_t = jnp.ones((LANES, seq), BF16)
    ones_c = jnp.ones((seq, LANES), BF16)
    for b in range(nkt_ref.shape[0]):
        rows = slice(b * seq, (b + 1) * seq)
        for hp in range(H_NAT // 2):
            sl = slice(hp * LANES, (hp + 1) * LANES)
            q = nq_ref[rows, sl]
            kt = nkt_ref[b, sl, :].astype(BF16)
            vt = jnp.concatenate([nvt_ref[b, sl, :].astype(BF16), ones_t], axis=0)
            outs = []
            for msk in (lo, hi):
                qm = jnp.where(msk, q, jnp.zeros_like(q))
                s = _dot(qm, kt)
                e = jnp.exp2(s - jnp.max(s, axis=-1, keepdims=True)).astype(BF16)
                ol = _dot_nt(e, vt)
                outs.append(ol[:, :LANES] * (1.0 / ol[:, LANES:]))
            o_ref[rows, sl] = jnp.where(lo, outs[0], outs[1]).astype(o_ref.dtype)
        for h in range(H_DIFF):
            sl = slice(h * LANES, (h + 1) * LANES)
            q = dq_ref[rows, sl]
            kt = dkt_ref[b, sl, :].astype(BF16)
            v = dv_ref[rows, sl]
            es, rs = [], []
            for msk in (lo, hi):
                qm = jnp.where(msk, q, jnp.zeros_like(q))
                s = _dot(qm, kt)
                e = jnp.exp2(s - jnp.max(s, axis=-1, keepdims=True))
                es.append(e)
                rs.append(_dot(e.astype(BF16), ones_c))
            r0 = 1.0 / rs[0]
            cf = lam * rs[0] * (1.0 / rs[1])
            cfw = jnp.concatenate([cf] * (seq // LANES), axis=1)
            a = (es[0] - cfw * es[1]).astype(BF16)
            o = _diff_out_norm(_dot(a, v) * r0, dn_ref, lam_init)
            o_ref[rows, W_NAT + h * LANES:W_NAT + (h + 1) * LANES] = o.astype(o_ref.dtype)


def _ctx_attention(parts, lam_p, dnorm, seq, lam_init, nb=4):
    nq, nkt, nvt, dq, dkt, dv = parts
    t = nq.shape[0]
    steps = t // (seq * nb)
    tok = pl.BlockSpec((nb * seq, 512), lambda i: (i, 0))
    fmaj = pl.BlockSpec((nb, 512, seq), lambda i: (i, 0, 0))
    return pl.pallas_call(
        functools.partial(_ctx_attn_kernel, lam_init=lam_init, seq=seq),
        grid=(steps,),
        in_specs=[tok, fmaj, fmaj, tok, fmaj, tok, _const_spec((4, HEAD_DIM)), _const_spec((1, 2 * HEAD_DIM))],
        out_specs=pl.BlockSpec((nb * seq, D_MODEL), lambda i: (i, 0)),
        out_shape=jax.ShapeDtypeStruct((t, D_MODEL), BF16),
        compiler_params=pltpu.CompilerParams(
            dimension_semantics=("arbitrary",), vmem_limit_bytes=VMEM_LIMIT),
        name="ctx_attention",
    )(nq, nkt, nvt, dq, dkt, dv, lam_p, dnorm.reshape(1, 2 * HEAD_DIM))


NAT_QROWS = 8
NAT_KROWS = 16
NAT_NDR = 2 * NAT_WIN_R


def _build_bias_tiles(rpb_ref, u_ref):
    c = lax.broadcasted_iota(jnp.int32, (GRID_W, LANES), 0)
    lane = lax.broadcasted_iota(jnp.int32, (GRID_W, LANES), 1)
    kc = jnp.bitwise_and(lane, GRID_W - 1)
    c0 = jnp.clip(c - NAT_WIN_C // 2, 0, GRID_W - NAT_WIN_C)
    col_ok = jnp.logical_and(kc >= c0, kc < c0 + NAT_WIN_C)
    left = lane < GRID_W
    n_dr = 2 * NAT_WIN_R - 1
    for h in range(H_NAT):
        tl, tr = [], []
        for dr in range(n_dr):
            row = jnp.broadcast_to(rpb_ref[h, dr:dr + 1, :] * LOG2E, (GRID_W, LANES))
            tl.append(pltpu.roll(row, LANES - (NAT_WIN_C - 1), 1, stride=1, stride_axis=0))
            tr.append(pltpu.roll(row, LANES - (NAT_WIN_C - 1) + GRID_W, 1, stride=1, stride_axis=0))
        for di in range(NAT_NDR):
            dl = di - NAT_WIN_R + (NAT_WIN_R - 1)
            drr = dl + 1
            lv = tl[dl] if 0 <= dl < n_dr else jnp.full((GRID_W, LANES), NEG, F32)
            rv = tr[drr] if 0 <= drr < n_dr else jnp.full((GRID_W, LANES), NEG, F32)
            u_ref[h, di] = jnp.where(col_ok, jnp.where(left, lv, rv), NEG)


def _nat_lat_kernel(q_ref, k_ref, v_ref, kc_ref, vc_ref, rpb_ref, o_ref, u_ref, s_ref):
    b = pl.program_id(1)

    @pl.when(jnp.logical_and(pl.program_id(0) == 0, b == 0))
    def _():
        _build_bias_tiles(rpb_ref, u_ref)

    rows = k_ref.shape[0] // GRID_W
    kb = jnp.clip(b * NAT_QROWS - NAT_WIN_R // 2, 0, rows - NAT_KROWS)
    koff = pl.multiple_of(kb * GRID_W, GRID_W)
    lo, hi = _half_masks()
    left = lax.broadcasted_iota(jnp.int32, (GRID_W, LANES), 1) < GRID_W
    nq = NAT_QROWS * GRID_W
    for hp in range(H_NAT // 2):
        sl = slice(hp * LANES, (hp + 1) * LANES)
        q = q_ref[:, sl]
        k = k_ref[pl.ds(koff, NAT_KROWS * GRID_W), sl]
        v = v_ref[pl.ds(koff, NAT_KROWS * GRID_W), sl]
        kc = kc_ref[:, sl]
        vc = vc_ref[:, sl]
        v1 = jnp.concatenate([v, jnp.ones_like(v)], axis=1)
        vc1 = jnp.concatenate([vc, jnp.ones_like(vc)], axis=1)
        outs = []
        for e, msk in enumerate((lo, hi)):
            h = 2 * hp + e
            qm = jnp.where(msk, q, jnp.zeros_like(q))
            s_ref[...] = _dot_nt(qm, k)
            for i in range(NAT_QROWS):
                r = b * NAT_QROWS + i
                r0 = jnp.clip(r - NAT_WIN_R // 2, 0, rows - NAT_WIN_R)
                for jp in range(NAT_KROWS // 2):
                    kl = kb + 2 * jp
                    ok_l = jnp.logical_and(kl >= r0, kl < r0 + NAT_WIN_R)
                    ok_r = jnp.logical_and(kl + 1 >= r0, kl + 1 < r0 + NAT_WIN_R)
                    di = jnp.clip(kl - r + NAT_WIN_R, 0, NAT_NDR - 1)
                    ok = jnp.where(left, ok_l.astype(jnp.int32), ok_r.astype(jnp.int32))
                    bias = jnp.where(ok > 0, u_ref[h, di], NEG)
                    tile = (slice(i * GRID_W, (i + 1) * GRID_W), slice(jp * LANES, (jp + 1) * LANES))
                    s_ref[tile] = s_ref[tile] + bias
            s_ctx = _dot_nt(qm, kc)
            s_loc = s_ref[...]
            m = jnp.maximum(jnp.max(s_loc, axis=-1, keepdims=True), jnp.max(s_ctx, axis=-1, keepdims=True))
            ol = (_dot(jnp.exp2(s_loc - m).astype(BF16), v1) + _dot(jnp.exp2(s_ctx - m).astype(BF16), vc1))
            outs.append(ol[:, :LANES] * (1.0 / ol[:, LANES:]))
        o_ref[:, sl] = jnp.where(lo, outs[0], outs[1]).astype(o_ref.dtype)


def _nat_lat_attention(q, k, v, kc, vc, rpb_pad, batch, seq):
    nq = NAT_QROWS * GRID_W
    nblk = seq // nq
    p = kc.shape[1]
    return pl.pallas_call(
        _nat_lat_kernel,
        grid=(batch, nblk),
        in_specs=[
            pl.BlockSpec((nq, W_NAT), lambda bb, i: (bb * nblk + i, 0)),
            pl.BlockSpec((seq, W_NAT), lambda bb, i: (bb, 0)),
            pl.BlockSpec((seq, W_NAT), lambda bb, i: (bb, 0)),
            pl.BlockSpec((None, p, W_NAT), lambda bb, i: (bb, 0, 0)),
            pl.BlockSpec((None, p, W_NAT), lambda bb, i: (bb, 0, 0)),
            pl.BlockSpec(rpb_pad.shape, lambda bb, i: (0, 0, 0)),
        ],
        out_specs=pl.BlockSpec((nq, W_NAT), lambda bb, i: (bb * nblk + i, 0)),
        out_shape=jax.ShapeDtypeStruct((batch * seq, W_NAT), BF16),
        scratch_shapes=[
            pltpu.VMEM((H_NAT, NAT_NDR, GRID_W, LANES), F32),
            pltpu.VMEM((nq, NAT_KROWS * GRID_W), F32),
        ],
        compiler_params=pltpu.CompilerParams(
            dimension_semantics=("arbitrary", "arbitrary"), vmem_limit_bytes=VMEM_LIMIT),
        name="nat_latent_attention",
    )(q, k, v, kc, vc, rpb_pad)


DIFF_KCHUNK = 256


def _lane_fold(x, op):
    out = x[:, :LANES]
    for c in range(1, x.shape[1] // LANES):
        out = op(out, x[:, c * LANES:(c + 1) * LANES])
    return out


def _round_robin(*gens):
    gens = list(gens)
    while gens:
        for g in list(gens):
            try:
                next(g)
            except StopIteration:
                gens.remove(g)


def _diff_lat_kernel(q_ref, k_ref, v_ref, kc_ref, vc_ref, lam_ref, dn_ref, o_ref, sa_ref, sb_ref, *, lam_init):
    lo, hi = _half_masks()
    lam = _diff_lambda(lam_ref, lam_init)
    ck = DIFF_KCHUNK
    pieces = ([(k_ref, v_ref, c * ck) for c in range(k_ref.shape[0] // ck)]
              + [(kc_ref, vc_ref, c * ck) for c in range(kc_ref.shape[0] // ck)])
    row_max, row_sum = {}, {}

    def s_of(h):
        return (sa_ref, sb_ref)[h % 2]

    def scores(h):
        sl = slice(h * LANES, (h + 1) * LANES)
        q = q_ref[:, sl]
        ms = []
        for j, msk in enumerate((lo, hi)):
            qm = jnp.where(msk, q, jnp.zeros_like(q))
            mx = None
            for ci, (kr, _, off) in enumerate(pieces):
                s = _dot_nt(qm, kr[off:off + ck, sl])
                s_of(h)[j, :, ci * ck:(ci + 1) * ck] = s
                part = _lane_fold(s, jnp.maximum)
                mx = part if mx is None else jnp.maximum(mx, part)
                yield
            ms.append(jnp.max(mx, axis=-1, keepdims=True))
        row_max[h] = ms

    def exps(h):
        ls = []
        for j in range(2):
            acc = None
            for ci in range(len(pieces)):
                tile = (j, slice(None), slice(ci * ck, (ci + 1) * ck))
                e = jnp.exp2(s_of(h)[tile] - row_max[h][j])
                s_of(h)[tile] = e
                part = _lane_fold(e, jnp.add)
                acc = part if acc is None else acc + part
                yield
            ls.append(jnp.sum(acc, axis=-1, keepdims=True))
        row_sum[h] = ls

    def values(h):
        sl = slice(h * LANES, (h + 1) * LANES)
        l0, l1 = row_sum[h]
        cf = lam * l0 * (1.0 / l1)
        o = None
        for ci, (_, vr, off) in enumerate(pieces):
            cols = slice(ci * ck, (ci + 1) * ck)
            a = (s_of(h)[0, :, cols] - cf * s_of(h)[1, :, cols]).astype(BF16)
            t = _dot(a, vr[off:off + ck, sl])
            o = t if o is None else o + t
            yield
        o = o * (1.0 / l0)
        o_ref[:, sl] = _diff_out_norm(o, dn_ref, lam_init).astype(o_ref.dtype)

    _round_robin(scores(0))
    for h in range(H_DIFF):
        nxt = [scores(h + 1)] if h + 1 < H_DIFF else []
        _round_robin(exps(h), *nxt)
        _round_robin(values(h))


def _diff_lat_attention(q, k, v, kc, vc, lam_p, dnorm, batch, seq, lam_init, tq=256):
    nqb = seq // tq
    p = kc.shape[1]
    s_scratch = pltpu.VMEM((2, tq, seq + p), F32)
    return pl.pallas_call(
        functools.partial(_diff_lat_kernel, lam_init=lam_init),
        grid=(batch, nqb),
        in_specs=[
            pl.BlockSpec((tq, W_DIFF), lambda bb, i: (bb * nqb + i, 0)),
            pl.BlockSpec((seq, W_DIFF), lambda bb, i: (bb, 0)),
            pl.BlockSpec((seq, W_DIFF), lambda bb, i: (bb, 0)),
            pl.BlockSpec((None, p, W_DIFF), lambda bb, i: (bb, 0, 0)),
            pl.BlockSpec((None, p, W_DIFF), lambda bb, i: (bb, 0, 0)),
            pl.BlockSpec((4, HEAD_DIM), lambda bb, i: (0, 0)),
            pl.BlockSpec((1, 2 * HEAD_DIM), lambda bb, i: (0, 0)),
        ],
        out_specs=pl.BlockSpec((tq, W_DIFF), lambda bb, i: (bb * nqb + i, 0)),
        out_shape=jax.ShapeDtypeStruct((batch * seq, W_DIFF), BF16),
        scratch_shapes=[s_scratch, s_scratch],
        compiler_params=pltpu.CompilerParams(
            dimension_semantics=("arbitrary", "arbitrary"), vmem_limit_bytes=VMEM_LIMIT),
        name="diff_latent_attention",
    )(q, k, v, kc, vc, lam_p, dnorm.reshape(1, 2 * HEAD_DIM))


def _rope_tables(n):
    t = np.arange(n)
    row = (t // GRID_W).astype(np.float64)
    col = (t % GRID_W).astype(np.float64)
    quarter = HEAD_DIM // 4
    inv = 1.0 / (ROPE_THETA ** (np.arange(quarter) / quarter))
    cr, sr = np.cos(row[:, None] * inv[None]), np.sin(row[:, None] * inv[None])
    cc, sc = np.cos(col[:, None] * inv[None]), np.sin(col[:, None] * inv[None])
    zero = np.zeros_like(sr)
    cos_t = np.concatenate([cr, cr, cc, cc], axis=1)
    sin_a = np.concatenate([-sr, zero, -sc, zero], axis=1)
    sin_b = np.concatenate([zero, sr, zero, sc], axis=1)
    tile = lambda a: jnp.asarray(np.concatenate([a, a], axis=1).astype(np.float32))
    return tile(cos_t), tile(sin_a), tile(sin_b)


def kernel(x_prompt, x_sample, cache_nat_k, cache_nat_v, cache_diff_k, cache_diff_v, c, c_ctx, mod_w, mod_b, norm_ffn1, ffn1_w_in, ffn1_w_out, norm_mix, even_w_in, pool_w, pool_scale, conv_w, odd_w_in, nat_rpb, diff_lambda, diff_norm, mix_w_out, norm_ffn2, ffn2_w_in, ffn2_w_out, final_norm):
    batch, seq, _ = x_prompt.shape
    dbatch, dseq, _ = x_sample.shape
    past = cache_nat_k.shape[2]
    tm = 512

    cond = jnp.zeros((SUBLANES, D_MODEL), F32).at[0].set(c_ctx).at[1:1 + dbatch].set(c)
    mods = _adaln_mods(cond, mod_w, mod_b).reshape(DEPTH, SUBLANES, N_MOD, D_MODEL)

    ctx_row = lambda i: 0
    lat_row = lambda i: 1 + (i * tm) // dseq

    cache_bf = lambda a, w: a[:, 0].reshape(dbatch, past, w).astype(BF16)
    gain = lambda g: g.reshape(g.shape[0], 1, g.shape[1])
    g_ffn1, g_mix, g_ffn2 = gain(norm_ffn1), gain(norm_mix), gain(norm_ffn2)
    pool3 = pool_w.reshape(pool_w.shape[0], len(POOL_WINDOWS) * POOL_G, POOL_G)

    f1_in0, f1_out0, w_even0 = _cast_call([(ffn1_w_in, 0), (ffn1_w_out, 0), (even_w_in, 0)])
    w_odd_t = jnp.swapaxes(odd_w_in, 1, 2).astype(BF16)

    ctx = x_prompt.reshape(batch * seq, D_MODEL)
    lat = x_sample.reshape(dbatch * dseq, D_MODEL)

    l = 0
    x1c, uc, f2_in0, f2_out0, wmo0, pool0 = _kernel_a(
        ctx, mods[l], ctx_row, (g_ffn1, l), (f1_in0, 0), (f1_out0, 0), (g_mix, l), (w_even0, 0), "even", tm,
        casts=[(ffn2_w_in, 0), (ffn2_w_out, 0), (mix_w_out, 0), (pool3, 0)])
    even_params = ((pool0.reshape(1, len(POOL_WINDOWS), POOL_G, POOL_G), 0), (gain(pool_scale), 0), (conv_w, 0))
    ctx, f1_in1, f1_out1 = _kernel_b(
        x1c, mods[l], ctx_row, [uc], even_params, (wmo0, 0), (g_ffn2, l), (f2_in0, 0), (f2_out0, 0), "even", tm,
        seq_len=seq, casts=[(ffn1_w_in, 1), (ffn1_w_out, 1)])
    x1l, ul, w_odd1, f2_in1 = _kernel_a(
        lat, mods[l], lat_row, (g_ffn1, l), (f1_in0, 0), (f1_out0, 0), (g_mix, l), (w_even0, 0), "even", tm,
        casts=[(odd_w_in, 0), (ffn2_w_in, 1)])
    lat, f2_out1, wmo1 = _kernel_b(
        x1l, mods[l], lat_row, [ul], even_params, (wmo0, 0), (g_ffn2, l), (f2_in0, 0), (f2_out0, 0), "even", tm,
        seq_len=dseq, casts=[(ffn2_w_out, 1), (mix_w_out, 1)])

    l = 1
    lam_init = 0.8 - 0.6 * math.exp(-0.3 * l)
    ffn1 = ((g_ffn1, l), (f1_in1, 0), (f1_out1, 0), (g_mix, l), (w_odd1, 0))
    ffn2 = ((wmo1, 0), (g_ffn2, l), (f2_in1, 0), (f2_out1, 0))
    ctx1, nq, nkt, nvt, dq, dkt, dv, dvb = _kernel_a(ctx, mods[l], ctx_row, *ffn1, "ctx", tm, wpt=(w_odd_t, 0),
                                                      seq=seq)
    o_ctx = _ctx_attention([nq, nkt, nvt, dq, dkt, dvb], diff_lambda[0], diff_norm[0], seq, lam_init)
    y_prompt, = _kernel_b(ctx1, mods[l], ctx_row, [o_ctx], None, *ffn2, "odd", tm, final_g=final_norm)

    lat1, lq, lk, lv, ldq, ldk, ldv = _kernel_a(lat, mods[l], lat_row, *ffn1, "lat", tm,
                                                 rope_tabs=_rope_tables(dseq))
    rpb_pad = jnp.zeros((H_NAT, 2 * NAT_WIN_R, LANES), F32).at[:, :2 * NAT_WIN_R - 1,
                                                                :2 * NAT_WIN_C - 1].set(nat_rpb[0])
    o_nat = _nat_lat_attention(lq, lk, lv, cache_bf(cache_nat_k, W_NAT), cache_bf(cache_nat_v, W_NAT),
                               rpb_pad, dbatch, dseq)
    o_diff = _diff_lat_attention(ldq, ldk, ldv, cache_bf(cache_diff_k, W_DIFF), cache_bf(cache_diff_v, W_DIFF),
                                 diff_lambda[0], diff_norm[0], dbatch, dseq, lam_init)
    y_sample, = _kernel_b(lat1, mods[l], lat_row, [o_nat, o_diff], None, *ffn2, "odd", tm, final_g=final_norm)

    new_nat_k = nkt.reshape(batch, 1, H_NAT, HEAD_DIM, seq).transpose(0, 1, 4, 2, 3)
    new_nat_v = nvt.reshape(batch, 1, H_NAT, HEAD_DIM, seq).transpose(0, 1, 4, 2, 3)
    new_diff_k = dkt.reshape(batch, 1, H_DIFF, 2, HEAD_DIM, seq).transpose(0, 1, 5, 2, 3, 4)
    new_diff_v = dv.reshape(batch, 1, seq, H_DIFF, 2 * HEAD_DIM)
    return (y_prompt.reshape(batch, seq, D_MODEL), y_sample.reshape(dbatch, dseq, D_MODEL),
            new_nat_k, new_nat_v, new_diff_k, new_diff_v)
```

```python
import functools
import math

import numpy as np
import jax
import jax.numpy as jnp
from jax import lax
from jax.experimental import pallas as pl
from jax.experimental.pallas import tpu as pltpu

D_MODEL = 1024
D_FF = 2816
N_MOD = 9
DEPTH = 2
GRID_W = 64
HEAD_DIM = 64
W_POOL = 512
W_CONV = 512
POOL_WINDOWS = (2, 4, 8, 16)
POOL_G = 128
W_NAT = 512
W_DIFF = 512
H_NAT = 8
H_DIFF = 4
NAT_WIN_R = 8
NAT_WIN_C = 16
ROPE_THETA = 10000.0
ATTN_SCALE = HEAD_DIM ** -0.5
LOG2E = math.log2(math.e)
Q_SCALE = ATTN_SCALE * LOG2E
EPS = 1e-6
NEG = -1e30

LANES = 128
SUBLANES = 8
VMEM_LIMIT = 60 * 1024 * 1024

BF16 = jnp.bfloat16
F32 = jnp.float32


def _dot(a, b):
    return jnp.dot(a, b, preferred_element_type=F32)


def _dot_nt(a, b):
    return lax.dot_general(a, b, (((1,), (1,)), ((), ())), preferred_element_type=F32)


def _sigmoid(x):
    return 1.0 / (1.0 + jnp.exp(-x))


def _rms(x):
    return x * lax.rsqrt(jnp.mean(x * x, axis=-1, keepdims=True) + EPS)


def _modnorm(x, shift, scale, g):
    return _rms(x) * (g * (1.0 + scale)) + shift


def _ffn_half_step(x, mods, base, g, w_in_ref, w_out_ref, ck):
    shift, scale, gate = mods[base:base + 1], mods[base + 1:base + 2], mods[base + 2:base + 3]
    h = _modnorm(x, shift, scale, g).astype(BF16)
    acc = None
    for c in range(D_FF // ck):
        a = _dot(h, w_in_ref[:, c * ck:(c + 1) * ck])
        b = _dot(h, w_in_ref[:, D_FF + c * ck:D_FF + (c + 1) * ck])
        gated = ((a * _sigmoid(a)) * b).astype(BF16)
        t = _dot(gated, w_out_ref[c * ck:(c + 1) * ck, :])
        acc = t if acc is None else acc + t
    return x + (0.5 * gate) * acc


def _const_spec(shape):
    nd = len(shape)
    return pl.BlockSpec(shape, lambda i, _nd=nd: (0,) * _nd, pipeline_mode=pl.Buffered(1))


def _mods_kernel(cond_ref, w_ref, b_ref, o_ref):
    cnd = cond_ref[...]
    s = (cnd * _sigmoid(cnd)).astype(BF16)
    o_ref[0] = _dot(s, w_ref[0].astype(BF16)) + b_ref[0]


def _adaln_mods(cond, mod_w, mod_b):
    tn = 2304
    nt = (N_MOD * D_MODEL) // tn
    return pl.pallas_call(
        _mods_kernel,
        grid=(DEPTH, nt),
        in_specs=[
            pl.BlockSpec((SUBLANES, D_MODEL), lambda l, j: (0, 0)),
            pl.BlockSpec((1, D_MODEL, tn), lambda l, j: (l, 0, j)),
            pl.BlockSpec((1, 1, tn), lambda l, j: (l, 0, j)),
        ],
        out_specs=pl.BlockSpec((1, SUBLANES, tn), lambda l, j: (l, 0, j)),
        out_shape=jax.ShapeDtypeStruct((DEPTH, SUBLANES, N_MOD * D_MODEL), F32),
        compiler_params=pltpu.CompilerParams(vmem_limit_bytes=VMEM_LIMIT),
        name="adaln_mods",
    )(cond, mod_w, mod_b.reshape(DEPTH, 1, N_MOD * D_MODEL))


def _rope(x, cos_t, sin_a, sin_b):
    w = x.shape[1]
    reps = w // LANES
    c = jnp.concatenate([cos_t] * reps, axis=1)
    sa = jnp.concatenate([sin_a] * reps, axis=1)
    sb = jnp.concatenate([sin_b] * reps, axis=1)
    quarter = HEAD_DIM // 4
    up = pltpu.roll(x, w - quarter, 1)
    dn = pltpu.roll(x, quarter, 1)
    return x * c + up * sa + dn * sb


def _side_casts(cast_ins, cast_outs):
    for ci, co in zip(cast_ins, cast_outs):
        co[0] = ci[...].astype(BF16)


def _ka_kernel(*refs, mode, ck, seq, n_cast):
    n_in = {"even": 7, "ctx": 7, "lat": 10}[mode]
    x_ref, mods_ref, g1_ref, w_in_ref, w_out_ref, gm_ref, wp_ref = refs[:7]
    outs = refs[n_in + n_cast:]
    _side_casts(refs[n_in:n_in + n_cast], outs[len(outs) - n_cast:])
    mods = mods_ref[0]
    x1 = _ffn_half_step(x_ref[...], mods, 0, g1_ref[...], w_in_ref, w_out_ref, ck)
    outs[0][...] = x1
    h = _modnorm(x1, mods[3:4], mods[4:5], gm_ref[...]).astype(BF16)
    if mode == "even":
        outs[1][...] = _dot(h, wp_ref[...])
        return
    if mode == "lat":
        cos_ref, sa_ref, sb_ref = refs[7:10]
    for p in range(6):
        if mode == "ctx" and p in (1, 2, 4):
            ut = lax.dot_general(wp_ref[:, p * 512:(p + 1) * 512], h, (((0,), (1,)), ((), ())),
                                 preferred_element_type=F32)
            for bb in range(ut.shape[1] // seq):
                outs[1 + p][bb] = ut[:, bb * seq:(bb + 1) * seq]
            continue
        u = _dot(h, wp_ref[:, p * 512:(p + 1) * 512])
        if mode == "lat" and p in (3, 4):
            u = _rope(u, cos_ref[...], sa_ref[...], sb_ref[...])
        if p in (0, 3):
            u = u * Q_SCALE
        if mode == "ctx" and p == 5:
            for hh in range(H_DIFF):
                outs[1 + p][:, hh, :] = u[:, hh * LANES:(hh + 1) * LANES]
            outs[2 + p][...] = u.astype(BF16)
            continue
        outs[1 + p][...] = u.astype(outs[1 + p].dtype)


def _layer_spec(wl):
    w, l = wl
    nd = w.ndim - 1
    return pl.BlockSpec((None,) + w.shape[1:], lambda i, _l=l, _nd=nd: (_l,) + (0,) * _nd,
                        pipeline_mode=pl.Buffered(1))


def _cast_specs(casts, nt):
    in_specs, args, out_specs, out_shape = [], [], [], []
    for w, l in casts:
        _, rows, cols = w.shape
        rb = rows // nt
        assert rows % nt == 0 and rb % (2 * SUBLANES) == 0
        in_specs.append(pl.BlockSpec((None, rb, cols), lambda i, _l=l: (_l, i, 0)))
        args.append(w)
        out_specs.append(pl.BlockSpec((1, rb, cols), lambda i: (0, i, 0)))
        out_shape.append(jax.ShapeDtypeStruct((1, rows, cols), BF16))
    return in_specs, args, out_specs, out_shape


def _cast_kernel(*refs):
    n = len(refs) // 2
    _side_casts(refs[:n], refs[n:])


def _cast_call(casts, nt=8):
    c_in, c_args, c_out, c_shape = _cast_specs(casts, nt)
    return pl.pallas_call(
        _cast_kernel, grid=(nt,), in_specs=c_in, out_specs=c_out, out_shape=c_shape,
        compiler_params=pltpu.CompilerParams(dimension_semantics=("arbitrary",), vmem_limit_bytes=VMEM_LIMIT),
        name="cast_weights",
    )(*c_args)


def _kernel_a(x, mods_l, row_fn, g1, w_in, w_out, gm, wp, mode, tm, rope_tabs=None, seq=None, casts=(), ck=256):
    t = x.shape[0]
    nt = t // tm
    tok = lambda w: pl.BlockSpec((tm, w), lambda i: (i, 0))
    weights = [g1, w_in, w_out, gm, wp]
    in_specs = [tok(D_MODEL), pl.BlockSpec((1, N_MOD, D_MODEL), lambda i: (row_fn(i), 0, 0))]
    in_specs += [_layer_spec(w) for w in weights]
    args = [x, mods_l] + [w[0] for w in weights]
    n_proj = wp[0].shape[2]
    out_shape = [jax.ShapeDtypeStruct((t, D_MODEL), F32)]
    out_specs = [tok(D_MODEL)]
    if mode == "even":
        out_shape.append(jax.ShapeDtypeStruct((t, n_proj), F32))
        out_specs.append(tok(n_proj))
    elif mode == "lat":
        seq_tiles = rope_tabs[0].shape[0] // tm
        for tab in rope_tabs:
            in_specs.append(pl.BlockSpec((tm, LANES), lambda i, _s=seq_tiles: (i % _s, 0)))
            args.append(tab)
        for _ in range(6):
            out_shape.append(jax.ShapeDtypeStruct((t, 512), BF16))
            out_specs.append(tok(512))
    else:
        spt = tm // seq
        for p in range(6):
            if p in (1, 2, 4):
                out_shape.append(jax.ShapeDtypeStruct((t // seq, 512, seq), F32))
                out_specs.append(pl.BlockSpec((spt, 512, seq), lambda i: (i, 0, 0)))
            elif p == 5:
                out_shape.append(jax.ShapeDtypeStruct((t, H_DIFF, 2 * HEAD_DIM), F32))
                out_specs.append(pl.BlockSpec((tm, H_DIFF, 2 * HEAD_DIM), lambda i: (i, 0, 0)))
                out_shape.append(jax.ShapeDtypeStruct((t, 512), BF16))
                out_specs.append(tok(512))
            else:
                out_shape.append(jax.ShapeDtypeStruct((t, 512), BF16))
                out_specs.append(tok(512))
    c_in, c_args, c_out, c_shape = _cast_specs(casts, nt)
    return pl.pallas_call(
        functools.partial(_ka_kernel, mode=mode, ck=ck, seq=seq, n_cast=len(casts)),
        grid=(nt,),
        in_specs=in_specs + c_in,
        out_specs=out_specs + c_out,
        out_shape=out_shape + c_shape,
        compiler_params=pltpu.CompilerParams(
            dimension_semantics=("arbitrary",), vmem_limit_bytes=VMEM_LIMIT),
        name="ffn1_proj_" + mode,
    )(*args, *c_args)


HALO = 8


def _even_mix(ext_ref, base, n, pos, seq_len, pw_ref, ps_ref, cw_ref):
    def rows(j, lo, hi):
        return ext_ref[base + HALO + j:base + HALO + j + n, lo:hi]

    ya = []
    for g, win in enumerate(POOL_WINDOWS):
        half = win // 2
        lo_c, hi_c = g * POOL_G, (g + 1) * POOL_G
        s = rows(-half, lo_c, hi_c)
        for j in range(-half + 1, half):
            s = s + rows(j, lo_c, hi_c)
        lo = jnp.clip(pos - half, 0, seq_len - 1)
        hi = jnp.clip(pos + half - 1, 0, seq_len - 1)
        cnt = (hi - lo + 1).astype(F32)
        d = (s / cnt - rows(0, lo_c, hi_c)).astype(BF16)
        ya.append(_dot(d, pw_ref[g]))
    ya = jnp.concatenate(ya, axis=1) * ps_ref[...]

    def z(j):
        return rows(j, W_POOL + 2 * W_CONV, W_POOL + 3 * W_CONV) * rows(j, W_POOL, W_POOL + W_CONV)

    y = z(-1) * cw_ref[0:1, :] + z(0) * cw_ref[1:2, :] + z(1) * cw_ref[2:3, :]
    yb = rows(0, W_POOL + W_CONV, W_POOL + 2 * W_CONV) * y
    return jnp.concatenate([ya, yb], axis=1)


def _kb_kernel(*refs, mode, ck, tm, seq_len, final, n_cast):
    x_ref, mods_ref = refs[0], refs[1]
    k = 2
    if mode == "even":
        u_ref, up_ref, un_ref, pw_ref, ps_ref, cw_ref = refs[k:k + 6]
        k += 6
    else:
        n_parts = 2 if mode == "odd2" else 1
        o_refs = refs[k:k + n_parts]
        k += n_parts
    wmo_ref, g2_ref, w_in_ref, w_out_ref = refs[k:k + 4]
    k += 4
    if final:
        gf_ref = refs[k]
        k += 1
    out_ref = refs[k + n_cast]
    _side_casts(refs[k:k + n_cast], refs[k + n_cast + 1:k + 2 * n_cast + 1])
    k += 2 * n_cast + 1
    mods = mods_ref[0]
    if mode == "even":
        ext_ref = refs[k]
        n_seg = min(seq_len, tm)
        stride = n_seg + 2 * HALO
        row0 = pl.program_id(0) * tm
        feats = []
        for si in range(tm // n_seg):
            base = si * stride
            if n_seg == seq_len:
                before = after = jnp.zeros((HALO, ext_ref.shape[1]), F32)
            else:
                first = jnp.bitwise_and(row0, seq_len - 1) == 0
                last = jnp.bitwise_and(row0 + tm, seq_len - 1) == 0
                before = jnp.where(first, 0.0, up_ref[...])
                after = jnp.where(last, 0.0, un_ref[...])
            ext_ref[base:base + HALO, :] = before
            ext_ref[base + HALO:base + HALO + n_seg, :] = u_ref[si * n_seg:(si + 1) * n_seg, :]
            ext_ref[base + HALO + n_seg:base + stride, :] = after
            row = lax.broadcasted_iota(jnp.int32, (n_seg, LANES), 0) + (row0 + si * n_seg)
            pos = jnp.bitwise_and(row, seq_len - 1)
            feats.append(_even_mix(ext_ref, base, n_seg, pos, seq_len, pw_ref, ps_ref, cw_ref))
        feat = jnp.concatenate(feats, axis=0).astype(BF16)
        y = _dot(feat, wmo_ref[...])
    else:
        y = None
        off = 0
        for o_ref in o_refs:
            w = o_ref.shape[1]
            t = _dot(o_ref[...], wmo_ref[off:off + w, :])
            y = t if y is None else y + t
            off += w
    x2 = x_ref[...] + mods[5:6] * y
    x3 = _ffn_half_step(x2, mods, 6, g2_ref[...], w_in_ref, w_out_ref, ck)
    if final:
        x3 = _rms(x3) * gf_ref[...]
    out_ref[...] = x3


def _kernel_b(x, mods_l, row_fn, mix_in, mix_params, wmo, g2, w_in, w_out, mode, tm, seq_len=None,
              final_g=None, casts=(), ck=256):
    t = x.shape[0]
    nt = t // tm
    tok = lambda w: pl.BlockSpec((tm, w), lambda i: (i, 0))
    in_specs = [tok(D_MODEL), pl.BlockSpec((1, N_MOD, D_MODEL), lambda i: (row_fn(i), 0, 0))]
    args = [x, mods_l]
    scratch = []
    if mode == "even":
        u = mix_in[0]
        wu = u.shape[1]
        hb = tm // HALO
        last = t // HALO - 1
        in_specs += [
            tok(wu),
            pl.BlockSpec((HALO, wu), lambda i: (jnp.maximum(i * hb - 1, 0), 0)),
            pl.BlockSpec((HALO, wu), lambda i: (jnp.minimum((i + 1) * hb, last), 0)),
        ]
        args += [u, u, u]
        for w in mix_params:
            in_specs.append(_layer_spec(w))
            args.append(w[0])
        n_seg = min(seq_len, tm)
        assert tm % n_seg == 0 and seq_len % n_seg == 0 and seq_len & (seq_len - 1) == 0
        scratch.append(pltpu.VMEM(((tm // n_seg) * (n_seg + 2 * HALO), wu), F32))
    else:
        for o in mix_in:
            in_specs.append(tok(o.shape[1]))
            args.append(o)
    in_specs += [_layer_spec(w) for w in (wmo, g2, w_in, w_out)]
    args += [w[0] for w in (wmo, g2, w_in, w_out)]
    final = final_g is not None
    if final:
        in_specs.append(_const_spec((1, D_MODEL)))
        args.append(final_g.reshape(1, D_MODEL))
    kmode = mode if mode == "even" else ("odd2" if len(mix_in) == 2 else "odd1")
    c_in, c_args, c_out, c_shape = _cast_specs(casts, nt)
    return pl.pallas_call(
        functools.partial(_kb_kernel, mode=kmode, ck=ck, tm=tm, seq_len=seq_len, final=final, n_cast=len(casts)),
        grid=(nt,),
        in_specs=in_specs + c_in,
        out_specs=[tok(D_MODEL)] + c_out,
        out_shape=[jax.ShapeDtypeStruct((t, D_MODEL), F32)] + c_shape,
        scratch_shapes=scratch,
        compiler_params=pltpu.CompilerParams(
            dimension_semantics=("arbitrary",), vmem_limit_bytes=VMEM_LIMIT),
        name="mix_ffn2_" + kmode,
    )(*args, *c_args)


def _diff_lambda(lam_ref, lam_init):
    lp = lam_ref[...]
    s1 = jnp.sum(lp[0:1] * lp[1:2], axis=-1, keepdims=True)
    s2 = jnp.sum(lp[2:3] * lp[3:4], axis=-1, keepdims=True)
    return jnp.exp(s1) - jnp.exp(s2) + lam_init


def _half_masks():
    lane = lax.broadcasted_iota(jnp.int32, (1, LANES), 1)
    lo = lane < HEAD_DIM
    return lo, jnp.logical_not(lo)


def _diff_out_norm(o, dn_ref, lam_init):
    return (_rms(o) * dn_ref[...]) * (1.0 - lam_init)


def _ctx_attn_kernel(nq_ref, nkt_ref, nvt_ref, dq_ref, dkt_ref, dv_ref, lam_ref, dn_ref, o_ref, *, lam_init, seq):
    lo, hi = _half_masks()
    lam = _diff_lambda(lam_ref, lam_init)
    ones_t = jnp.ones((LANES, seq), BF16)
    ones_c = jnp.ones((seq, LANES), BF16)
    for b in range(nkt_ref.shape[0]):
        rows = slice(b * seq, (b + 1) * seq)
        for hp in range(H_NAT // 2):
            sl = slice(hp * LANES, (hp + 1) * LANES)
            q = nq_ref[rows, sl]
            kt = nkt_ref[b, sl, :].astype(BF16)
            vt = jnp.concatenate([nvt_ref[b, sl, :].astype(BF16), ones_t], axis=0)
            outs = []
            for msk in (lo, hi):
                qm = jnp.where(msk, q, jnp.zeros_like(q))
                s = _dot(qm, kt)
                e = jnp.exp2(s - jnp.max(s, axis=-1, keepdims=True)).astype(BF16)
                ol = _dot_nt(e, vt)
                outs.append(ol[:, :LANES] * (1.0 / ol[:, LANES:]))
            o_ref[rows, sl] = jnp.where(lo, outs[0], outs[1]).astype(o_ref.dtype)
        for h in range(H_DIFF):
            sl = slice(h * LANES, (h + 1) * LANES)
            q = dq_ref[rows, sl]
            kt = dkt_ref[b, sl, :].astype(BF16)
            v = dv_ref[rows, sl]
            es, rs = [], []
            for msk in (lo, hi):
                qm = jnp.where(msk, q, jnp.zeros_like(q))
                s = _dot(qm, kt)
                e = jnp.exp2(s - jnp.max(s, axis=-1, keepdims=True))
                es.append(e)
                rs.append(_dot(e.astype(BF16), ones_c))
            r0 = 1.0 / rs[0]
            cf = lam * rs[0] * (1.0 / rs[1])
            cfw = jnp.concatenate([cf] * (seq // LANES), axis=1)
            a = (es[0] - cfw * es[1]).astype(BF16)
            o = _diff_out_norm(_dot(a, v) * r0, dn_ref, lam_init)
            o_ref[rows, W_NAT + h * LANES:W_NAT + (h + 1) * LANES] = o.astype(o_ref.dtype)


def _ctx_attention(parts, lam_p, dnorm, seq, lam_init, nb=4):
    nq, nkt, nvt, dq, dkt, dv = parts
    t = nq.shape[0]
    steps = t // (seq * nb)
    tok = pl.BlockSpec((nb * seq, 512), lambda i: (i, 0))
    fmaj = pl.BlockSpec((nb, 512, seq), lambda i: (i, 0, 0))
    return pl.pallas_call(
        functools.partial(_ctx_attn_kernel, lam_init=lam_init, seq=seq),
        grid=(steps,),
        in_specs=[tok, fmaj, fmaj, tok, fmaj, tok, _const_spec((4, HEAD_DIM)), _const_spec((1, 2 * HEAD_DIM))],
        out_specs=pl.BlockSpec((nb * seq, D_MODEL), lambda i: (i, 0)),
        out_shape=jax.ShapeDtypeStruct((t, D_MODEL), BF16),
        compiler_params=pltpu.CompilerParams(
            dimension_semantics=("arbitrary",), vmem_limit_bytes=VMEM_LIMIT),
        name="ctx_attention",
    )(nq, nkt, nvt, dq, dkt, dv, lam_p, dnorm.reshape(1, 2 * HEAD_DIM))


NAT_QROWS = 8
NAT_KROWS = 16
NAT_NDR = 2 * NAT_WIN_R


def _build_bias_tiles(rpb_ref, u_ref):
    c = lax.broadcasted_iota(jnp.int32, (GRID_W, LANES), 0)
    lane = lax.broadcasted_iota(jnp.int32, (GRID_W, LANES), 1)
    kc = jnp.bitwise_and(lane, GRID_W - 1)
    c0 = jnp.clip(c - NAT_WIN_C // 2, 0, GRID_W - NAT_WIN_C)
    col_ok = jnp.logical_and(kc >= c0, kc < c0 + NAT_WIN_C)
    left = lane < GRID_W
    n_dr = 2 * NAT_WIN_R - 1
    for h in range(H_NAT):
        tl, tr = [], []
        for dr in range(n_dr):
            row = jnp.broadcast_to(rpb_ref[h, dr:dr + 1, :] * LOG2E, (GRID_W, LANES))
            tl.append(pltpu.roll(row, LANES - (NAT_WIN_C - 1), 1, stride=1, stride_axis=0))
            tr.append(pltpu.roll(row, LANES - (NAT_WIN_C - 1) + GRID_W, 1, stride=1, stride_axis=0))
        for di in range(NAT_NDR):
            dl = di - NAT_WIN_R + (NAT_WIN_R - 1)
            drr = dl + 1
            lv = tl[dl] if 0 <= dl < n_dr else jnp.full((GRID_W, LANES), NEG, F32)
            rv = tr[drr] if 0 <= drr < n_dr else jnp.full((GRID_W, LANES), NEG, F32)
            u_ref[h, di] = jnp.where(col_ok, jnp.where(left, lv, rv), NEG)


def _nat_lat_kernel(q_ref, k_ref, v_ref, kc_ref, vc_ref, rpb_ref, o_ref, u_ref, s_ref):
    b = pl.program_id(1)

    @pl.when(jnp.logical_and(pl.program_id(0) == 0, b == 0))
    def _():
        _build_bias_tiles(rpb_ref, u_ref)

    rows = k_ref.shape[0] // GRID_W
    kb = jnp.clip(b * NAT_QROWS - NAT_WIN_R // 2, 0, rows - NAT_KROWS)
    koff = pl.multiple_of(kb * GRID_W, GRID_W)
    lo, hi = _half_masks()
    left = lax.broadcasted_iota(jnp.int32, (GRID_W, LANES), 1) < GRID_W
    nq = NAT_QROWS * GRID_W
    for hp in range(H_NAT // 2):
        sl = slice(hp * LANES, (hp + 1) * LANES)
        q = q_ref[:, sl]
        k = k_ref[pl.ds(koff, NAT_KROWS * GRID_W), sl]
        v = v_ref[pl.ds(koff, NAT_KROWS * GRID_W), sl]
        kc = kc_ref[:, sl]
        vc = vc_ref[:, sl]
        v1 = jnp.concatenate([v, jnp.ones_like(v)], axis=1)
        vc1 = jnp.concatenate([vc, jnp.ones_like(vc)], axis=1)
        outs = []
        for e, msk in enumerate((lo, hi)):
            h = 2 * hp + e
            qm = jnp.where(msk, q, jnp.zeros_like(q))
            s_ref[...] = _dot_nt(qm, k)
            for i in range(NAT_QROWS):
                r = b * NAT_QROWS + i
                r0 = jnp.clip(r - NAT_WIN_R // 2, 0, rows - NAT_WIN_R)
                for jp in range(NAT_KROWS // 2):
                    kl = kb + 2 * jp
                    ok_l = jnp.logical_and(kl >= r0, kl < r0 + NAT_WIN_R)
                    ok_r = jnp.logical_and(kl + 1 >= r0, kl + 1 < r0 + NAT_WIN_R)
                    di = jnp.clip(kl - r + NAT_WIN_R, 0, NAT_NDR - 1)
                    ok = jnp.where(left, ok_l.astype(jnp.int32), ok_r.astype(jnp.int32))
                    bias = jnp.where(ok > 0, u_ref[h, di], NEG)
                    tile = (slice(i * GRID_W, (i + 1) * GRID_W), slice(jp * LANES, (jp + 1) * LANES))
                    s_ref[tile] = s_ref[tile] + bias
            s_ctx = _dot_nt(qm, kc)
            s_loc = s_ref[...]
            m = jnp.maximum(jnp.max(s_loc, axis=-1, keepdims=True), jnp.max(s_ctx, axis=-1, keepdims=True))
            ol = (_dot(jnp.exp2(s_loc - m).astype(BF16), v1) + _dot(jnp.exp2(s_ctx - m).astype(BF16), vc1))
            outs.append(ol[:, :LANES] * (1.0 / ol[:, LANES:]))
        o_ref[:, sl] = jnp.where(lo, outs[0], outs[1]).astype(o_ref.dtype)


def _nat_lat_attention(q, k, v, kc, vc, rpb_pad, batch, seq):
    nq = NAT_QROWS * GRID_W
    nblk = seq // nq
    p = kc.shape[1]
    return pl.pallas_call(
        _nat_lat_kernel,
        grid=(batch, nblk),
        in_specs=[
            pl.BlockSpec((nq, W_NAT), lambda bb, i: (bb * nblk + i, 0)),
            pl.BlockSpec((seq, W_NAT), lambda bb, i: (bb, 0)),
            pl.BlockSpec((seq, W_NAT), lambda bb, i: (bb, 0)),
            pl.BlockSpec((None, p, W_NAT), lambda bb, i: (bb, 0, 0)),
            pl.BlockSpec((None, p, W_NAT), lambda bb, i: (bb, 0, 0)),
            pl.BlockSpec(rpb_pad.shape, lambda bb, i: (0, 0, 0)),
        ],
        out_specs=pl.BlockSpec((nq, W_NAT), lambda bb, i: (bb * nblk + i, 0)),
        out_shape=jax.ShapeDtypeStruct((batch * seq, W_NAT), BF16),
        scratch_shapes=[
            pltpu.VMEM((H_NAT, NAT_NDR, GRID_W, LANES), F32),
            pltpu.VMEM((nq, NAT_KROWS * GRID_W), F32),
        ],
        compiler_params=pltpu.CompilerParams(
            dimension_semantics=("arbitrary", "arbitrary"), vmem_limit_bytes=VMEM_LIMIT),
        name="nat_latent_attention",
    )(q, k, v, kc, vc, rpb_pad)


DIFF_KCHUNK = 256


def _lane_fold(x, op):
    out = x[:, :LANES]
    for c in range(1, x.shape[1] // LANES):
        out = op(out, x[:, c * LANES:(c + 1) * LANES])
    return out


def _round_robin(*gens):
    gens = list(gens)
    while gens:
        for g in list(gens):
            try:
                next(g)
            except StopIteration:
                gens.remove(g)


def _diff_lat_kernel(q_ref, k_ref, v_ref, kc_ref, vc_ref, lam_ref, dn_ref, o_ref, sa_ref, sb_ref, *, lam_init):
    lo, hi = _half_masks()
    lam = _diff_lambda(lam_ref, lam_init)
    ck = DIFF_KCHUNK
    pieces = ([(k_ref, v_ref, c * ck) for c in range(k_ref.shape[0] // ck)]
              + [(kc_ref, vc_ref, c * ck) for c in range(kc_ref.shape[0] // ck)])
    row_max, row_sum = {}, {}

    def s_of(h):
        return (sa_ref, sb_ref)[h % 2]

    def scores(h):
        sl = slice(h * LANES, (h + 1) * LANES)
        q = q_ref[:, sl]
        ms = []
        for j, msk in enumerate((lo, hi)):
            qm = jnp.where(msk, q, jnp.zeros_like(q))
            mx = None
            for ci, (kr, _, off) in enumerate(pieces):
                s = _dot_nt(qm, kr[off:off + ck, sl])
                s_of(h)[j, :, ci * ck:(ci + 1) * ck] = s
                part = _lane_fold(s, jnp.maximum)
                mx = part if mx is None else jnp.maximum(mx, part)
                yield
            ms.append(jnp.max(mx, axis=-1, keepdims=True))
        row_max[h] = ms

    def exps(h):
        ls = []
        for j in range(2):
            acc = None
            for ci in range(len(pieces)):
                tile = (j, slice(None), slice(ci * ck, (ci + 1) * ck))
                e = jnp.exp2(s_of(h)[tile] - row_max[h][j])
                s_of(h)[tile] = e
                part = _lane_fold(e, jnp.add)
                acc = part if acc is None else acc + part
                yield
            ls.append(jnp.sum(acc, axis=-1, keepdims=True))
        row_sum[h] = ls

    def values(h):
        sl = slice(h * LANES, (h + 1) * LANES)
        l0, l1 = row_sum[h]
        cf = lam * l0 * (1.0 / l1)
        o = None
        for ci, (_, vr, off) in enumerate(pieces):
            cols = slice(ci * ck, (ci + 1) * ck)
            a = (s_of(h)[0, :, cols] - cf * s_of(h)[1, :, cols]).astype(BF16)
            t = _dot(a, vr[off:off + ck, sl])
            o = t if o is None else o + t
            yield
        o = o * (1.0 / l0)
        o_ref[:, sl] = _diff_out_norm(o, dn_ref, lam_init).astype(o_ref.dtype)

    _round_robin(scores(0))
    for h in range(H_DIFF):
        nxt = [scores(h + 1)] if h + 1 < H_DIFF else []
        _round_robin(exps(h), *nxt)
        _round_robin(values(h))


def _diff_lat_attention(q, k, v, kc, vc, lam_p, dnorm, batch, seq, lam_init, tq=256):
    nqb = seq // tq
    p = kc.shape[1]
    s_scratch = pltpu.VMEM((2, tq, seq + p), F32)
    return pl.pallas_call(
        functools.partial(_diff_lat_kernel, lam_init=lam_init),
        grid=(batch, nqb),
        in_specs=[
            pl.BlockSpec((tq, W_DIFF), lambda bb, i: (bb * nqb + i, 0)),
            pl.BlockSpec((seq, W_DIFF), lambda bb, i: (bb, 0)),
            pl.BlockSpec((seq, W_DIFF), lambda bb, i: (bb, 0)),
            pl.BlockSpec((None, p, W_DIFF), lambda bb, i: (bb, 0, 0)),
            pl.BlockSpec((None, p, W_DIFF), lambda bb, i: (bb, 0, 0)),
            pl.BlockSpec((4, HEAD_DIM), lambda bb, i: (0, 0)),
            pl.BlockSpec((1, 2 * HEAD_DIM), lambda bb, i: (0, 0)),
        ],
        out_specs=pl.BlockSpec((tq, W_DIFF), lambda bb, i: (bb * nqb + i, 0)),
        out_shape=jax.ShapeDtypeStruct((batch * seq, W_DIFF), BF16),
        scratch_shapes=[s_scratch, s_scratch],
        compiler_params=pltpu.CompilerParams(
            dimension_semantics=("arbitrary", "arbitrary"), vmem_limit_bytes=VMEM_LIMIT),
        name="diff_latent_attention",
    )(q, k, v, kc, vc, lam_p, dnorm.reshape(1, 2 * HEAD_DIM))


def _rope_tables(n):
    t = np.arange(n)
    row = (t // GRID_W).astype(np.float64)
    col = (t % GRID_W).astype(np.float64)
    quarter = HEAD_DIM // 4
    inv = 1.0 / (ROPE_THETA ** (np.arange(quarter) / quarter))
    cr, sr = np.cos(row[:, None] * inv[None]), np.sin(row[:, None] * inv[None])
    cc, sc = np.cos(col[:, None] * inv[None]), np.sin(col[:, None] * inv[None])
    zero = np.zeros_like(sr)
    cos_t = np.concatenate([cr, cr, cc, cc], axis=1)
    sin_a = np.concatenate([-sr, zero, -sc, zero], axis=1)
    sin_b = np.concatenate([zero, sr, zero, sc], axis=1)
    tile = lambda a: jnp.asarray(np.concatenate([a, a], axis=1).astype(np.float32))
    return tile(cos_t), tile(sin_a), tile(sin_b)


def kernel(x_prompt, x_sample, cache_nat_k, cache_nat_v, cache_diff_k, cache_diff_v, c, c_ctx, mod_w, mod_b, norm_ffn1, ffn1_w_in, ffn1_w_out, norm_mix, even_w_in, pool_w, pool_scale, conv_w, odd_w_in, nat_rpb, diff_lambda, diff_norm, mix_w_out, norm_ffn2, ffn2_w_in, ffn2_w_out, final_norm):
    batch, seq, _ = x_prompt.shape
    dbatch, dseq, _ = x_sample.shape
    past = cache_nat_k.shape[2]
    tm = 512

    cond = jnp.zeros((SUBLANES, D_MODEL), F32).at[0].set(c_ctx).at[1:1 + dbatch].set(c)
    mods = _adaln_mods(cond, mod_w, mod_b).reshape(DEPTH, SUBLANES, N_MOD, D_MODEL)

    ctx_row = lambda i: 0
    lat_row = lambda i: 1 + (i * tm) // dseq

    cache_bf = lambda a, w: a[:, 0].reshape(dbatch, past, w).astype(BF16)
    gain = lambda g: g.reshape(g.shape[0], 1, g.shape[1])
    g_ffn1, g_mix, g_ffn2 = gain(norm_ffn1), gain(norm_mix), gain(norm_ffn2)
    pool3 = pool_w.reshape(pool_w.shape[0], len(POOL_WINDOWS) * POOL_G, POOL_G)

    f1_in0, f1_out0, w_even0 = _cast_call([(ffn1_w_in, 0), (ffn1_w_out, 0), (even_w_in, 0)])

    ctx = x_prompt.reshape(batch * seq, D_MODEL)
    lat = x_sample.reshape(dbatch * dseq, D_MODEL)

    l = 0
    x1c, uc, f2_in0, f2_out0, wmo0, pool0 = _kernel_a(
        ctx, mods[l], ctx_row, (g_ffn1, l), (f1_in0, 0), (f1_out0, 0), (g_mix, l), (w_even0, 0), "even", tm,
        casts=[(ffn2_w_in, 0), (ffn2_w_out, 0), (mix_w_out, 0), (pool3, 0)])
    even_params = ((pool0.reshape(1, len(POOL_WINDOWS), POOL_G, POOL_G), 0), (gain(pool_scale), 0), (conv_w, 0))
    ctx, f1_in1, f1_out1 = _kernel_b(
        x1c, mods[l], ctx_row, [uc], even_params, (wmo0, 0), (g_ffn2, l), (f2_in0, 0), (f2_out0, 0), "even", tm,
        seq_len=seq, casts=[(ffn1_w_in, 1), (ffn1_w_out, 1)])
    x1l, ul, w_odd1, f2_in1 = _kernel_a(
        lat, mods[l], lat_row, (g_ffn1, l), (f1_in0, 0), (f1_out0, 0), (g_mix, l), (w_even0, 0), "even", tm,
        casts=[(odd_w_in, 0), (ffn2_w_in, 1)])
    lat, f2_out1, wmo1 = _kernel_b(
        x1l, mods[l], lat_row, [ul], even_params, (wmo0, 0), (g_ffn2, l), (f2_in0, 0), (f2_out0, 0), "even", tm,
        seq_len=dseq, casts=[(ffn2_w_out, 1), (mix_w_out, 1)])

    l = 1
    lam_init = 0.8 - 0.6 * math.exp(-0.3 * l)
    ffn1 = ((g_ffn1, l), (f1_in1, 0), (f1_out1, 0), (g_mix, l), (w_odd1, 0))
    ffn2 = ((wmo1, 0), (g_ffn2, l), (f2_in1, 0), (f2_out1, 0))
    ctx1, nq, nkt, nvt, dq, dkt, dv, dvb = _kernel_a(ctx, mods[l], ctx_row, *ffn1, "ctx", tm, seq=seq)
    o_ctx = _ctx_attention([nq, nkt, nvt, dq, dkt, dvb], diff_lambda[0], diff_norm[0], seq, lam_init)
    y_prompt, = _kernel_b(ctx1, mods[l], ctx_row, [o_ctx], None, *ffn2, "odd", tm, final_g=final_norm)

    lat1, lq, lk, lv, ldq, ldk, ldv = _kernel_a(lat, mods[l], lat_row, *ffn1, "lat", tm,
                                                 rope_tabs=_rope_tables(dseq))
    rpb_pad = jnp.zeros((H_NAT, 2 * NAT_WIN_R, LANES), F32).at[:, :2 * NAT_WIN_R - 1,
                                                                :2 * NAT_WIN_C - 1].set(nat_rpb[0])
    o_nat = _nat_lat_attention(lq, lk, lv, cache_bf(cache_nat_k, W_NAT), cache_bf(cache_nat_v, W_NAT),
                               rpb_pad, dbatch, dseq)
    o_diff = _diff_lat_attention(ldq, ldk, ldv, cache_bf(cache_diff_k, W_DIFF), cache_bf(cache_diff_v, W_DIFF),
                                 diff_lambda[0], diff_norm[0], dbatch, dseq, lam_init)
    y_sample, = _kernel_b(lat1, mods[l], lat_row, [o_nat, o_diff], None, *ffn2, "odd", tm, final_g=final_norm)

    new_nat_k = nkt.reshape(batch, 1, H_NAT, HEAD_DIM, seq).transpose(0, 1, 4, 2, 3)
    new_nat_v = nvt.reshape(batch, 1, H_NAT, HEAD_DIM, seq).transpose(0, 1, 4, 2, 3)
    new_diff_k = dkt.reshape(batch, 1, H_DIFF, 2, HEAD_DIM, seq).transpose(0, 1, 5, 2, 3, 4)
    new_diff_v = dv.reshape(batch, 1, seq, H_DIFF, 2 * HEAD_DIM)
    return (y_prompt.reshape(batch, seq, D_MODEL), y_sample.reshape(dbatch, dseq, D_MODEL),
            new_nat_k, new_nat_v, new_diff_k, new_diff_v)
```

```python
import functools
import math

import numpy as np
import jax
import jax.numpy as jnp
from jax import lax
from jax.experimental import pallas as pl
from jax.experimental.pallas import tpu as pltpu

D_MODEL = 1024
D_FF = 2816
N_MOD = 9
DEPTH = 2
GRID_W = 64
HEAD_DIM = 64
W_POOL = 512
W_CONV = 512
POOL_WINDOWS = (2, 4, 8, 16)
POOL_G = 128
W_NAT = 512
W_DIFF = 512
H_NAT = 8
H_DIFF = 4
NAT_WIN_R = 8
NAT_WIN_C = 16
ROPE_THETA = 10000.0
ATTN_SCALE = HEAD_DIM ** -0.5
LOG2E = math.log2(math.e)
Q_SCALE = ATTN_SCALE * LOG2E
EPS = 1e-6
NEG = -1e30

LANES = 128
SUBLANES = 8
VMEM_LIMIT = 60 * 1024 * 1024

BF16 = jnp.bfloat16
F32 = jnp.float32


def _dot(a, b):
    return jnp.dot(a, b, preferred_element_type=F32)


def _dot_nt(a, b):
    return lax.dot_general(a, b, (((1,), (1,)), ((), ())), preferred_element_type=F32)


def _sigmoid(x):
    return 1.0 / (1.0 + jnp.exp(-x))


def _rms(x):
    return x * lax.rsqrt(jnp.mean(x * x, axis=-1, keepdims=True) + EPS)


def _modnorm(x, shift, scale, g):
    return _rms(x) * (g * (1.0 + scale)) + shift


def _ffn_half_step(x, mods, base, g, w_in_ref, w_out_ref, ck):
    shift, scale, gate = mods[base:base + 1], mods[base + 1:base + 2], mods[base + 2:base + 3]
    h = _modnorm(x, shift, scale, g).astype(BF16)
    acc = None
    for c in range(D_FF // ck):
        a = _dot(h, w_in_ref[:, c * ck:(c + 1) * ck])
        b = _dot(h, w_in_ref[:, D_FF + c * ck:D_FF + (c + 1) * ck])
        gated = ((a * _sigmoid(a)) * b).astype(BF16)
        t = _dot(gated, w_out_ref[c * ck:(c + 1) * ck, :])
        acc = t if acc is None else acc + t
    return x + (0.5 * gate) * acc


def _const_spec(shape):
    nd = len(shape)
    return pl.BlockSpec(shape, lambda i, _nd=nd: (0,) * _nd, pipeline_mode=pl.Buffered(1))


def _mods_kernel(cond_ref, w_ref, b_ref, o_ref):
    cnd = cond_ref[...]
    s = (cnd * _sigmoid(cnd)).astype(BF16)
    o_ref[0] = _dot(s, w_ref[0].astype(BF16)) + b_ref[0]


def _adaln_mods(cond, mod_w, mod_b):
    tn = 2304
    nt = (N_MOD * D_MODEL) // tn
    return pl.pallas_call(
        _mods_kernel,
        grid=(DEPTH, nt),
        in_specs=[
            pl.BlockSpec((SUBLANES, D_MODEL), lambda l, j: (0, 0)),
            pl.BlockSpec((1, D_MODEL, tn), lambda l, j: (l, 0, j)),
            pl.BlockSpec((1, 1, tn), lambda l, j: (l, 0, j)),
        ],
        out_specs=pl.BlockSpec((1, SUBLANES, tn), lambda l, j: (l, 0, j)),
        out_shape=jax.ShapeDtypeStruct((DEPTH, SUBLANES, N_MOD * D_MODEL), F32),
        compiler_params=pltpu.CompilerParams(vmem_limit_bytes=VMEM_LIMIT),
        name="adaln_mods",
    )(cond, mod_w, mod_b.reshape(DEPTH, 1, N_MOD * D_MODEL))


def _rope(x, cos_t, sin_a, sin_b):
    w = x.shape[1]
    reps = w // LANES
    c = jnp.concatenate([cos_t] * reps, axis=1)
    sa = jnp.concatenate([sin_a] * reps, axis=1)
    sb = jnp.concatenate([sin_b] * reps, axis=1)
    quarter = HEAD_DIM // 4
    up = pltpu.roll(x, w - quarter, 1)
    dn = pltpu.roll(x, quarter, 1)
    return x * c + up * sa + dn * sb


def _side_casts(cast_ins, cast_outs):
    for ci, co in zip(cast_ins, cast_outs):
        co[0] = ci[...].astype(BF16)


def _ka_kernel(*refs, mode, ck, seq, n_cast):
    n_in = {"even": 7, "ctx": 7, "lat": 10}[mode]
    x_ref, mods_ref, g1_ref, w_in_ref, w_out_ref, gm_ref, wp_ref = refs[:7]
    outs = refs[n_in + n_cast:]
    _side_casts(refs[n_in:n_in + n_cast], outs[len(outs) - n_cast:])
    mods = mods_ref[0]
    x1 = _ffn_half_step(x_ref[...], mods, 0, g1_ref[...], w_in_ref, w_out_ref, ck)
    outs[0][...] = x1
    h = _modnorm(x1, mods[3:4], mods[4:5], gm_ref[...]).astype(BF16)
    if mode == "even":
        outs[1][...] = _dot(h, wp_ref[...])
        return
    if mode == "lat":
        cos_ref, sa_ref, sb_ref = refs[7:10]
    for p in range(6):
        if mode == "ctx" and p in (1, 2, 4):
            ut = lax.dot_general(wp_ref[:, p * 512:(p + 1) * 512], h, (((0,), (1,)), ((), ())),
                                 preferred_element_type=F32)
            for bb in range(ut.shape[1] // seq):
                outs[1 + p][bb] = ut[:, bb * seq:(bb + 1) * seq]
            continue
        u = _dot(h, wp_ref[:, p * 512:(p + 1) * 512])
        if mode == "lat" and p in (3, 4):
            u = _rope(u, cos_ref[...], sa_ref[...], sb_ref[...])
        if p in (0, 3):
            u = u * Q_SCALE
        if mode == "ctx" and p == 5:
            for hh in range(H_DIFF):
                outs[1 + p][:, hh, :] = u[:, hh * LANES:(hh + 1) * LANES]
            outs[2 + p][...] = u.astype(BF16)
            continue
        outs[1 + p][...] = u.astype(outs[1 + p].dtype)


def _layer_spec(wl):
    w, l = wl
    nd = w.ndim - 1
    return pl.BlockSpec((None,) + w.shape[1:], lambda i, _l=l, _nd=nd: (_l,) + (0,) * _nd,
                        pipeline_mode=pl.Buffered(1))


def _cast_specs(casts, nt):
    in_specs, args, out_specs, out_shape = [], [], [], []
    for w, l in casts:
        _, rows, cols = w.shape
        rb = rows // nt
        assert rows % nt == 0 and rb % (2 * SUBLANES) == 0
        in_specs.append(pl.BlockSpec((None, rb, cols), lambda i, _l=l: (_l, i, 0)))
        args.append(w)
        out_specs.append(pl.BlockSpec((1, rb, cols), lambda i: (0, i, 0)))
        out_shape.append(jax.ShapeDtypeStruct((1, rows, cols), BF16))
    return in_specs, args, out_specs, out_shape


def _cast_kernel(*refs):
    n = len(refs) // 2
    _side_casts(refs[:n], refs[n:])


def _cast_call(casts, nt=8):
    c_in, c_args, c_out, c_shape = _cast_specs(casts, nt)
    return pl.pallas_call(
        _cast_kernel, grid=(nt,), in_specs=c_in, out_specs=c_out, out_shape=c_shape,
        compiler_params=pltpu.CompilerParams(dimension_semantics=("arbitrary",), vmem_limit_bytes=VMEM_LIMIT),
        name="cast_weights",
    )(*c_args)


def _kernel_a(x, mods_l, row_fn, g1, w_in, w_out, gm, wp, mode, tm, rope_tabs=None, seq=None, casts=(), ck=256):
    t = x.shape[0]
    nt = t // tm
    tok = lambda w: pl.BlockSpec((tm, w), lambda i: (i, 0))
    weights = [g1, w_in, w_out, gm, wp]
    in_specs = [tok(D_MODEL), pl.BlockSpec((1, N_MOD, D_MODEL), lambda i: (row_fn(i), 0, 0))]
    in_specs += [_layer_spec(w) for w in weights]
    args = [x, mods_l] + [w[0] for w in weights]
    n_proj = wp[0].shape[2]
    out_shape = [jax.ShapeDtypeStruct((t, D_MODEL), F32)]
    out_specs = [tok(D_MODEL)]
    if mode == "even":
        out_shape.append(jax.ShapeDtypeStruct((t, n_proj), F32))
        out_specs.append(tok(n_proj))
    elif mode == "lat":
        seq_tiles = rope_tabs[0].shape[0] // tm
        for tab in rope_tabs:
            in_specs.append(pl.BlockSpec((tm, LANES), lambda i, _s=seq_tiles: (i % _s, 0)))
            args.append(tab)
        for _ in range(6):
            out_shape.append(jax.ShapeDtypeStruct((t, 512), BF16))
            out_specs.append(tok(512))
    else:
        spt = tm // seq
        for p in range(6):
            if p in (1, 2, 4):
                out_shape.append(jax.ShapeDtypeStruct((t // seq, 512, seq), F32))
                out_specs.append(pl.BlockSpec((spt, 512, seq), lambda i: (i, 0, 0)))
            elif p == 5:
                out_shape.append(jax.ShapeDtypeStruct((t, H_DIFF, 2 * HEAD_DIM), F32))
                out_specs.append(pl.BlockSpec((tm, H_DIFF, 2 * HEAD_DIM), lambda i: (i, 0, 0)))
                out_shape.append(jax.ShapeDtypeStruct((t, 512), BF16))
                out_specs.append(tok(512))
            else:
                out_shape.append(jax.ShapeDtypeStruct((t, 512), BF16))
                out_specs.append(tok(512))
    c_in, c_args, c_out, c_shape = _cast_specs(casts, nt)
    return pl.pallas_call(
        functools.partial(_ka_kernel, mode=mode, ck=ck, seq=seq, n_cast=len(casts)),
        grid=(nt,),
        in_specs=in_specs + c_in,
        out_specs=out_specs + c_out,
        out_shape=out_shape + c_shape,
        compiler_params=pltpu.CompilerParams(
            dimension_semantics=("arbitrary",), vmem_limit_bytes=VMEM_LIMIT),
        name="ffn1_proj_" + mode,
    )(*args, *c_args)


HALO = 8


def _even_mix(ext_ref, base, n, pos, seq_len, pw_ref, ps_ref, cw_ref):
    def rows(j, lo, hi):
        return ext_ref[base + HALO + j:base + HALO + j + n, lo:hi]

    ya = []
    for g, win in enumerate(POOL_WINDOWS):
        half = win // 2
        lo_c, hi_c = g * POOL_G, (g + 1) * POOL_G
        s = rows(-half, lo_c, hi_c)
        for j in range(-half + 1, half):
            s = s + rows(j, lo_c, hi_c)
        lo = jnp.clip(pos - half, 0, seq_len - 1)
        hi = jnp.clip(pos + half - 1, 0, seq_len - 1)
        cnt = (hi - lo + 1).astype(F32)
        d = (s / cnt - rows(0, lo_c, hi_c)).astype(BF16)
        ya.append(_dot(d, pw_ref[g]))
    ya = jnp.concatenate(ya, axis=1) * ps_ref[...]

    def z(j):
        return rows(j, W_POOL + 2 * W_CONV, W_POOL + 3 * W_CONV) * rows(j, W_POOL, W_POOL + W_CONV)

    y = z(-1) * cw_ref[0:1, :] + z(0) * cw_ref[1:2, :] + z(1) * cw_ref[2:3, :]
    yb = rows(0, W_POOL + W_CONV, W_POOL + 2 * W_CONV) * y
    return jnp.concatenate([ya, yb], axis=1)


def _kb_kernel(*refs, mode, ck, tm, seq_len, final, n_cast):
    x_ref, mods_ref = refs[0], refs[1]
    k = 2
    if mode == "even":
        u_ref, up_ref, un_ref, pw_ref, ps_ref, cw_ref = refs[k:k + 6]
        k += 6
    else:
        n_parts = 2 if mode == "odd2" else 1
        o_refs = refs[k:k + n_parts]
        k += n_parts
    wmo_ref, g2_ref, w_in_ref, w_out_ref = refs[k:k + 4]
    k += 4
    if final:
        gf_ref = refs[k]
        k += 1
    out_ref = refs[k + n_cast]
    _side_casts(refs[k:k + n_cast], refs[k + n_cast + 1:k + 2 * n_cast + 1])
    k += 2 * n_cast + 1
    mods = mods_ref[0]
    if mode == "even":
        ext_ref = refs[k]
        n_seg = min(seq_len, tm)
        stride = n_seg + 2 * HALO
        row0 = pl.program_id(0) * tm
        feats = []
        for si in range(tm // n_seg):
            base = si * stride
            if n_seg == seq_len:
                before = after = jnp.zeros((HALO, ext_ref.shape[1]), F32)
            else:
                first = jnp.bitwise_and(row0, seq_len - 1) == 0
                last = jnp.bitwise_and(row0 + tm, seq_len - 1) == 0
                before = jnp.where(first, 0.0, up_ref[...])
                after = jnp.where(last, 0.0, un_ref[...])
            ext_ref[base:base + HALO, :] = before
            ext_ref[base + HALO:base + HALO + n_seg, :] = u_ref[si * n_seg:(si + 1) * n_seg, :]
            ext_ref[base + HALO + n_seg:base + stride, :] = after
            row = lax.broadcasted_iota(jnp.int32, (n_seg, LANES), 0) + (row0 + si * n_seg)
            pos = jnp.bitwise_and(row, seq_len - 1)
            feats.append(_even_mix(ext_ref, base, n_seg, pos, seq_len, pw_ref, ps_ref, cw_ref))
        feat = jnp.concatenate(feats, axis=0).astype(BF16)
        y = _dot(feat, wmo_ref[...])
    else:
        y = None
        off = 0
        for o_ref in o_refs:
            w = o_ref.shape[1]
            t = _dot(o_ref[...], wmo_ref[off:off + w, :])
            y = t if y is None else y + t
            off += w
    x2 = x_ref[...] + mods[5:6] * y
    x3 = _ffn_half_step(x2, mods, 6, g2_ref[...], w_in_ref, w_out_ref, ck)
    if final:
        x3 = _rms(x3) * gf_ref[...]
    out_ref[...] = x3


def _kernel_b(x, mods_l, row_fn, mix_in, mix_params, wmo, g2, w_in, w_out, mode, tm, seq_len=None,
              final_g=None, casts=(), ck=256):
    t = x.shape[0]
    nt = t // tm
    tok = lambda w: pl.BlockSpec((tm, w), lambda i: (i, 0))
    in_specs = [tok(D_MODEL), pl.BlockSpec((1, N_MOD, D_MODEL), lambda i: (row_fn(i), 0, 0))]
    args = [x, mods_l]
    scratch = []
    if mode == "even":
        u = mix_in[0]
        wu = u.shape[1]
        hb = tm // HALO
        last = t // HALO - 1
        in_specs += [
            tok(wu),
            pl.BlockSpec((HALO, wu), lambda i: (jnp.maximum(i * hb - 1, 0), 0)),
            pl.BlockSpec((HALO, wu), lambda i: (jnp.minimum((i + 1) * hb, last), 0)),
        ]
        args += [u, u, u]
        for w in mix_params:
            in_specs.append(_layer_spec(w))
            args.append(w[0])
        n_seg = min(seq_len, tm)
        assert tm % n_seg == 0 and seq_len % n_seg == 0 and seq_len & (seq_len - 1) == 0
        scratch.append(pltpu.VMEM(((tm // n_seg) * (n_seg + 2 * HALO), wu), F32))
    else:
        for o in mix_in:
            in_specs.append(tok(o.shape[1]))
            args.append(o)
    in_specs += [_layer_spec(w) for w in (wmo, g2, w_in, w_out)]
    args += [w[0] for w in (wmo, g2, w_in, w_out)]
    final = final_g is not None
    if final:
        in_specs.append(_const_spec((1, D_MODEL)))
        args.append(final_g.reshape(1, D_MODEL))
    kmode = mode if mode == "even" else ("odd2" if len(mix_in) == 2 else "odd1")
    c_in, c_args, c_out, c_shape = _cast_specs(casts, nt)
    return pl.pallas_call(
        functools.partial(_kb_kernel, mode=kmode, ck=ck, tm=tm, seq_len=seq_len, final=final, n_cast=len(casts)),
        grid=(nt,),
        in_specs=in_specs + c_in,
        out_specs=[tok(D_MODEL)] + c_out,
        out_shape=[jax.ShapeDtypeStruct((t, D_MODEL), F32)] + c_shape,
        scratch_shapes=scratch,
        compiler_params=pltpu.CompilerParams(
            dimension_semantics=("arbitrary",), vmem_limit_bytes=VMEM_LIMIT),
        name="mix_ffn2_" + kmode,
    )(*args, *c_args)


def _diff_lambda(lam_ref, lam_init):
    lp = lam_ref[...]
    s1 = jnp.sum(lp[0:1] * lp[1:2], axis=-1, keepdims=True)
    s2 = jnp.sum(lp[2:3] * lp[3:4], axis=-1, keepdims=True)
    return jnp.exp(s1) - jnp.exp(s2) + lam_init


def _half_masks():
    lane = lax.broadcasted_iota(jnp.int32, (1, LANES), 1)
    lo = lane < HEAD_DIM
    return lo, jnp.logical_not(lo)


def _diff_out_norm(o, dn_ref, lam_init):
    return (_rms(o) * dn_ref[...]) * (1.0 - lam_init)


def _ctx_attn_kernel(nq_ref, nkt_ref, nvt_ref, dq_ref, dkt_ref, dv_ref, lam_ref, dn_ref, o_ref, *, lam_init, seq):
    lo, hi = _half_masks()
    lam = _diff_lambda(lam_ref, lam_init)
    ones_t = jnp.ones((LANES, seq), BF16)
    ones_c = jnp.ones((seq, LANES), BF16)
    for b in range(nkt_ref.shape[0]):
        rows = slice(b * seq, (b + 1) * seq)
        for hp in range(H_NAT // 2):
            sl = slice(hp * LANES, (hp + 1) * LANES)
            q = nq_ref[rows, sl]
            kt = nkt_ref[b, sl, :].astype(BF16)
            vt = jnp.concatenate([nvt_ref[b, sl, :].astype(BF16), ones_t], axis=0)
            outs = []
            for msk in (lo, hi):
                qm = jnp.where(msk, q, jnp.zeros_like(q))
                s = _dot(qm, kt)
                e = jnp.exp2(s - jnp.max(s, axis=-1, keepdims=True)).astype(BF16)
                ol = _dot_nt(e, vt)
                outs.append(ol[:, :LANES] * (1.0 / ol[:, LANES:]))
            o_ref[rows, sl] = jnp.where(lo, outs[0], outs[1]).astype(o_ref.dtype)
        for h in range(H_DIFF):
            sl = slice(h * LANES, (h + 1) * LANES)
            q = dq_ref[rows, sl]
            kt = dkt_ref[b, sl, :].astype(BF16)
            v1 = jnp.concatenate([dv_ref[rows, sl], ones_c], axis=1)
            ols = []
            for msk in (lo, hi):
                qm = jnp.where(msk, q, jnp.zeros_like(q))
                s = _dot(qm, kt)
                e = jnp.exp2(s - jnp.max(s, axis=-1, keepdims=True)).astype(BF16)
                ol = _dot(e, v1)
                ols.append(ol[:, :LANES] * (1.0 / ol[:, LANES:]))
            o = _diff_out_norm(ols[0] - lam * ols[1], dn_ref, lam_init)
            o_ref[rows, W_NAT + h * LANES:W_NAT + (h + 1) * LANES] = o.astype(o_ref.dtype)


def _ctx_attention(parts, lam_p, dnorm, seq, lam_init, nb=4):
    nq, nkt, nvt, dq, dkt, dv = parts
    t = nq.shape[0]
    steps = t // (seq * nb)
    tok = pl.BlockSpec((nb * seq, 512), lambda i: (i, 0))
    fmaj = pl.BlockSpec((nb, 512, seq), lambda i: (i, 0, 0))
    return pl.pallas_call(
        functools.partial(_ctx_attn_kernel, lam_init=lam_init, seq=seq),
        grid=(steps,),
        in_specs=[tok, fmaj, fmaj, tok, fmaj, tok, _const_spec((4, HEAD_DIM)), _const_spec((1, 2 * HEAD_DIM))],
        out_specs=pl.BlockSpec((nb * seq, D_MODEL), lambda i: (i, 0)),
        out_shape=jax.ShapeDtypeStruct((t, D_MODEL), BF16),
        compiler_params=pltpu.CompilerParams(
            dimension_semantics=("arbitrary",), vmem_limit_bytes=VMEM_LIMIT),
        name="ctx_attention",
    )(nq, nkt, nvt, dq, dkt, dv, lam_p, dnorm.reshape(1, 2 * HEAD_DIM))


NAT_QROWS = 8
NAT_KROWS = 16
NAT_NDR = 2 * NAT_WIN_R


def _build_bias_tiles(rpb_ref, u_ref):
    c = lax.broadcasted_iota(jnp.int32, (GRID_W, LANES), 0)
    lane = lax.broadcasted_iota(jnp.int32, (GRID_W, LANES), 1)
    kc = jnp.bitwise_and(lane, GRID_W - 1)
    c0 = jnp.clip(c - NAT_WIN_C // 2, 0, GRID_W - NAT_WIN_C)
    col_ok = jnp.logical_and(kc >= c0, kc < c0 + NAT_WIN_C)
    left = lane < GRID_W
    n_dr = 2 * NAT_WIN_R - 1
    for h in range(H_NAT):
        tl, tr = [], []
        for dr in range(n_dr):
            row = jnp.broadcast_to(rpb_ref[h, dr:dr + 1, :] * LOG2E, (GRID_W, LANES))
            tl.append(pltpu.roll(row, LANES - (NAT_WIN_C - 1), 1, stride=1, stride_axis=0))
            tr.append(pltpu.roll(row, LANES - (NAT_WIN_C - 1) + GRID_W, 1, stride=1, stride_axis=0))
        for di in range(NAT_NDR):
            dl = di - NAT_WIN_R + (NAT_WIN_R - 1)
            drr = dl + 1
            lv = tl[dl] if 0 <= dl < n_dr else jnp.full((GRID_W, LANES), NEG, F32)
            rv = tr[drr] if 0 <= drr < n_dr else jnp.full((GRID_W, LANES), NEG, F32)
            u_ref[h, di] = jnp.where(col_ok, jnp.where(left, lv, rv), NEG)


def _nat_lat_kernel(q_ref, k_ref, v_ref, kc_ref, vc_ref, rpb_ref, o_ref, u_ref, s_ref):
    b = pl.program_id(1)

    @pl.when(jnp.logical_and(pl.program_id(0) == 0, b == 0))
    def _():
        _build_bias_tiles(rpb_ref, u_ref)

    rows = k_ref.shape[0] // GRID_W
    kb = jnp.clip(b * NAT_QROWS - NAT_WIN_R // 2, 0, rows - NAT_KROWS)
    koff = pl.multiple_of(kb * GRID_W, GRID_W)
    lo, hi = _half_masks()
    left = lax.broadcasted_iota(jnp.int32, (GRID_W, LANES), 1) < GRID_W
    nq = NAT_QROWS * GRID_W
    for hp in range(H_NAT // 2):
        sl = slice(hp * LANES, (hp + 1) * LANES)
        q = q_ref[:, sl]
        k = k_ref[pl.ds(koff, NAT_KROWS * GRID_W), sl]
        v = v_ref[pl.ds(koff, NAT_KROWS * GRID_W), sl]
        kc = kc_ref[:, sl]
        vc = vc_ref[:, sl]
        v1 = jnp.concatenate([v, jnp.ones_like(v)], axis=1)
        vc1 = jnp.concatenate([vc, jnp.ones_like(vc)], axis=1)
        outs = []
        for e, msk in enumerate((lo, hi)):
            h = 2 * hp + e
            qm = jnp.where(msk, q, jnp.zeros_like(q))
            s_ref[...] = _dot_nt(qm, k)
            for i in range(NAT_QROWS):
                r = b * NAT_QROWS + i
                r0 = jnp.clip(r - NAT_WIN_R // 2, 0, rows - NAT_WIN_R)
                for jp in range(NAT_KROWS // 2):
                    kl = kb + 2 * jp
                    ok_l = jnp.logical_and(kl >= r0, kl < r0 + NAT_WIN_R)
                    ok_r = jnp.logical_and(kl + 1 >= r0, kl + 1 < r0 + NAT_WIN_R)
                    di = jnp.clip(kl - r + NAT_WIN_R, 0, NAT_NDR - 1)
                    ok = jnp.where(left, ok_l.astype(jnp.int32), ok_r.astype(jnp.int32))
                    bias = jnp.where(ok > 0, u_ref[h, di], NEG)
                    tile = (slice(i * GRID_W, (i + 1) * GRID_W), slice(jp * LANES, (jp + 1) * LANES))
                    s_ref[tile] = s_ref[tile] + bias
            s_ctx = _dot_nt(qm, kc)
            s_loc = s_ref[...]
            m = jnp.maximum(jnp.max(s_loc, axis=-1, keepdims=True), jnp.max(s_ctx, axis=-1, keepdims=True))
            ol = (_dot(jnp.exp2(s_loc - m).astype(BF16), v1) + _dot(jnp.exp2(s_ctx - m).astype(BF16), vc1))
            outs.append(ol[:, :LANES] * (1.0 / ol[:, LANES:]))
        o_ref[:, sl] = jnp.where(lo, outs[0], outs[1]).astype(o_ref.dtype)


def _nat_lat_attention(q, k, v, kc, vc, rpb_pad, batch, seq):
    nq = NAT_QROWS * GRID_W
    nblk = seq // nq
    p = kc.shape[1]
    return pl.pallas_call(
        _nat_lat_kernel,
        grid=(batch, nblk),
        in_specs=[
            pl.BlockSpec((nq, W_NAT), lambda bb, i: (bb * nblk + i, 0)),
            pl.BlockSpec((seq, W_NAT), lambda bb, i: (bb, 0)),
            pl.BlockSpec((seq, W_NAT), lambda bb, i: (bb, 0)),
            pl.BlockSpec((None, p, W_NAT), lambda bb, i: (bb, 0, 0)),
            pl.BlockSpec((None, p, W_NAT), lambda bb, i: (bb, 0, 0)),
            pl.BlockSpec(rpb_pad.shape, lambda bb, i: (0, 0, 0)),
        ],
        out_specs=pl.BlockSpec((nq, W_NAT), lambda bb, i: (bb * nblk + i, 0)),
        out_shape=jax.ShapeDtypeStruct((batch * seq, W_NAT), BF16),
        scratch_shapes=[
            pltpu.VMEM((H_NAT, NAT_NDR, GRID_W, LANES), F32),
            pltpu.VMEM((nq, NAT_KROWS * GRID_W), F32),
        ],
        compiler_params=pltpu.CompilerParams(
            dimension_semantics=("arbitrary", "arbitrary"), vmem_limit_bytes=VMEM_LIMIT),
        name="nat_latent_attention",
    )(q, k, v, kc, vc, rpb_pad)


DIFF_KCHUNK = 256


def _lane_fold(x, op):
    out = x[:, :LANES]
    for c in range(1, x.shape[1] // LANES):
        out = op(out, x[:, c * LANES:(c + 1) * LANES])
    return out


def _round_robin(*gens):
    gens = list(gens)
    while gens:
        for g in list(gens):
            try:
                next(g)
            except StopIteration:
                gens.remove(g)


def _diff_lat_kernel(q_ref, k_ref, v_ref, kc_ref, vc_ref, lam_ref, dn_ref, o_ref, sa_ref, sb_ref, *, lam_init):
    lo, hi = _half_masks()
    lam = _diff_lambda(lam_ref, lam_init)
    ck = DIFF_KCHUNK
    pieces = ([(k_ref, v_ref, c * ck) for c in range(k_ref.shape[0] // ck)]
              + [(kc_ref, vc_ref, c * ck) for c in range(kc_ref.shape[0] // ck)])
    row_max = {}

    def s_of(h):
        return (sa_ref, sb_ref)[h % 2]

    def scores(h):
        sl = slice(h * LANES, (h + 1) * LANES)
        q = q_ref[:, sl]
        ms = []
        for j, msk in enumerate((lo, hi)):
            qm = jnp.where(msk, q, jnp.zeros_like(q))
            mx = None
            for ci, (kr, _, off) in enumerate(pieces):
                s = _dot_nt(qm, kr[off:off + ck, sl])
                s_of(h)[j, :, ci * ck:(ci + 1) * ck] = s
                part = _lane_fold(s, jnp.maximum)
                mx = part if mx is None else jnp.maximum(mx, part)
                yield
            ms.append(jnp.max(mx, axis=-1, keepdims=True))
        row_max[h] = ms

    def weighted_values(h):
        sl = slice(h * LANES, (h + 1) * LANES)
        tq = q_ref.shape[0]
        acc = None
        for ci, (_, vr, off) in enumerate(pieces):
            cols = slice(ci * ck, (ci + 1) * ck)
            e = jnp.concatenate([jnp.exp2(s_of(h)[j, :, cols] - row_max[h][j]).astype(BF16) for j in range(2)], axis=0)
            vv = vr[off:off + ck, sl]
            t = _dot(e, jnp.concatenate([vv, jnp.ones_like(vv)], axis=1))
            acc = t if acc is None else acc + t
            yield
        on = acc[:, :LANES] * (1.0 / acc[:, LANES:])
        o = on[:tq] - lam * on[tq:]
        o_ref[:, sl] = _diff_out_norm(o, dn_ref, lam_init).astype(o_ref.dtype)

    _round_robin(scores(0))
    for h in range(H_DIFF):
        nxt = [scores(h + 1)] if h + 1 < H_DIFF else []
        _round_robin(weighted_values(h), *nxt)


def _diff_lat_attention(q, k, v, kc, vc, lam_p, dnorm, batch, seq, lam_init, tq=256):
    nqb = seq // tq
    p = kc.shape[1]
    s_scratch = pltpu.VMEM((2, tq, seq + p), F32)
    return pl.pallas_call(
        functools.partial(_diff_lat_kernel, lam_init=lam_init),
        grid=(batch, nqb),
        in_specs=[
            pl.BlockSpec((tq, W_DIFF), lambda bb, i: (bb * nqb + i, 0)),
            pl.BlockSpec((seq, W_DIFF), lambda bb, i: (bb, 0)),
            pl.BlockSpec((seq, W_DIFF), lambda bb, i: (bb, 0)),
            pl.BlockSpec((None, p, W_DIFF), lambda bb, i: (bb, 0, 0)),
            pl.BlockSpec((None, p, W_DIFF), lambda bb, i: (bb, 0, 0)),
            pl.BlockSpec((4, HEAD_DIM), lambda bb, i: (0, 0)),
            pl.BlockSpec((1, 2 * HEAD_DIM), lambda bb, i: (0, 0)),
        ],
        out_specs=pl.BlockSpec((tq, W_DIFF), lambda bb, i: (bb * nqb + i, 0)),
        out_shape=jax.ShapeDtypeStruct((batch * seq, W_DIFF), BF16),
        scratch_shapes=[s_scratch, s_scratch],
        compiler_params=pltpu.CompilerParams(
            dimension_semantics=("arbitrary", "arbitrary"), vmem_limit_bytes=VMEM_LIMIT),
        name="diff_latent_attention",
    )(q, k, v, kc, vc, lam_p, dnorm.reshape(1, 2 * HEAD_DIM))


def _rope_tables(n):
    t = np.arange(n)
    row = (t // GRID_W).astype(np.float64)
    col = (t % GRID_W).astype(np.float64)
    quarter = HEAD_DIM // 4
    inv = 1.0 / (ROPE_THETA ** (np.arange(quarter) / quarter))
    cr, sr = np.cos(row[:, None] * inv[None]), np.sin(row[:, None] * inv[None])
    cc, sc = np.cos(col[:, None] * inv[None]), np.sin(col[:, None] * inv[None])
    zero = np.zeros_like(sr)
    cos_t = np.concatenate([cr, cr, cc, cc], axis=1)
    sin_a = np.concatenate([-sr, zero, -sc, zero], axis=1)
    sin_b = np.concatenate([zero, sr, zero, sc], axis=1)
    tile = lambda a: jnp.asarray(np.concatenate([a, a], axis=1).astype(np.float32))
    return tile(cos_t), tile(sin_a), tile(sin_b)


def kernel(x_prompt, x_sample, cache_nat_k, cache_nat_v, cache_diff_k, cache_diff_v, c, c_ctx, mod_w, mod_b, norm_ffn1, ffn1_w_in, ffn1_w_out, norm_mix, even_w_in, pool_w, pool_scale, conv_w, odd_w_in, nat_rpb, diff_lambda, diff_norm, mix_w_out, norm_ffn2, ffn2_w_in, ffn2_w_out, final_norm):
    batch, seq, _ = x_prompt.shape
    dbatch, dseq, _ = x_sample.shape
    past = cache_nat_k.shape[2]
    tm = 512

    cond = jnp.zeros((SUBLANES, D_MODEL), F32).at[0].set(c_ctx).at[1:1 + dbatch].set(c)
    mods = _adaln_mods(cond, mod_w, mod_b).reshape(DEPTH, SUBLANES, N_MOD, D_MODEL)

    ctx_row = lambda i: 0
    lat_row = lambda i: 1 + (i * tm) // dseq

    cache_bf = lambda a, w: a[:, 0].reshape(dbatch, past, w).astype(BF16)
    gain = lambda g: g.reshape(g.shape[0], 1, g.shape[1])
    g_ffn1, g_mix, g_ffn2 = gain(norm_ffn1), gain(norm_mix), gain(norm_ffn2)
    pool3 = pool_w.reshape(pool_w.shape[0], len(POOL_WINDOWS) * POOL_G, POOL_G)

    f1_in0, f1_out0, w_even0 = _cast_call([(ffn1_w_in, 0), (ffn1_w_out, 0), (even_w_in, 0)])

    ctx = x_prompt.reshape(batch * seq, D_MODEL)
    lat = x_sample.reshape(dbatch * dseq, D_MODEL)

    l = 0
    x1c, uc, f2_in0, f2_out0, wmo0, pool0 = _kernel_a(
        ctx, mods[l], ctx_row, (g_ffn1, l), (f1_in0, 0), (f1_out0, 0), (g_mix, l), (w_even0, 0), "even", tm,
        casts=[(ffn2_w_in, 0), (ffn2_w_out, 0), (mix_w_out, 0), (pool3, 0)])
    even_params = ((pool0.reshape(1, len(POOL_WINDOWS), POOL_G, POOL_G), 0), (gain(pool_scale), 0), (conv_w, 0))
    ctx, f1_in1, f1_out1 = _kernel_b(
        x1c, mods[l], ctx_row, [uc], even_params, (wmo0, 0), (g_ffn2, l), (f2_in0, 0), (f2_out0, 0), "even", tm,
        seq_len=seq, casts=[(ffn1_w_in, 1), (ffn1_w_out, 1)])
    x1l, ul, w_odd1, f2_in1 = _kernel_a(
        lat, mods[l], lat_row, (g_ffn1, l), (f1_in0, 0), (f1_out0, 0), (g_mix, l), (w_even0, 0), "even", tm,
        casts=[(odd_w_in, 0), (ffn2_w_in, 1)])
    lat, f2_out1, wmo1 = _kernel_b(
        x1l, mods[l], lat_row, [ul], even_params, (wmo0, 0), (g_ffn2, l), (f2_in0, 0), (f2_out0, 0), "even", tm,
        seq_len=dseq, casts=[(ffn2_w_out, 1), (mix_w_out, 1)])

    l = 1
    lam_init = 0.8 - 0.6 * math.exp(-0.3 * l)
    ffn1 = ((g_ffn1, l), (f1_in1, 0), (f1_out1, 0), (g_mix, l), (w_odd1, 0))
    ffn2 = ((wmo1, 0), (g_ffn2, l), (f2_in1, 0), (f2_out1, 0))
    ctx1, nq, nkt, nvt, dq, dkt, dv, dvb = _kernel_a(ctx, mods[l], ctx_row, *ffn1, "ctx", tm, seq=seq)
    o_ctx = _ctx_attention([nq, nkt, nvt, dq, dkt, dvb], diff_lambda[0], diff_norm[0], seq, lam_init)
    y_prompt, = _kernel_b(ctx1, mods[l], ctx_row, [o_ctx], None, *ffn2, "odd", tm, final_g=final_norm)

    lat1, lq, lk, lv, ldq, ldk, ldv = _kernel_a(lat, mods[l], lat_row, *ffn1, "lat", tm,
                                                 rope_tabs=_rope_tables(dseq))
    rpb_pad = jnp.zeros((H_NAT, 2 * NAT_WIN_R, LANES), F32).at[:, :2 * NAT_WIN_R - 1,
                                                                :2 * NAT_WIN_C - 1].set(nat_rpb[0])
    o_nat = _nat_lat_attention(lq, lk, lv, cache_bf(cache_nat_k, W_NAT), cache_bf(cache_nat_v, W_NAT),
                               rpb_pad, dbatch, dseq)
    o_diff = _diff_lat_attention(ldq, ldk, ldv, cache_bf(cache_diff_k, W_DIFF), cache_bf(cache_diff_v, W_DIFF),
                                 diff_lambda[0], diff_norm[0], dbatch, dseq, lam_init)
    y_sample, = _kernel_b(lat1, mods[l], lat_row, [o_nat, o_diff], None, *ffn2, "odd", tm, final_g=final_norm)

    new_nat_k = nkt.reshape(batch, 1, H_NAT, HEAD_DIM, seq).transpose(0, 1, 4, 2, 3)
    new_nat_v = nvt.reshape(batch, 1, H_NAT, HEAD_DIM, seq).transpose(0, 1, 4, 2, 3)
    new_diff_k = dkt.reshape(batch, 1, H_DIFF, 2, HEAD_DIM, seq).transpose(0, 1, 5, 2, 3, 4)
    new_diff_v = dv.reshape(batch, 1, seq, H_DIFF, 2 * HEAD_DIM)
    return (y_prompt.reshape(batch, seq, D_MODEL), y_sample.reshape(dbatch, dseq, D_MODEL),
            new_nat_k, new_nat_v, new_diff_k, new_diff_v)
```

```python
import functools
import math

import numpy as np
import jax
import jax.numpy as jnp
from jax import lax
from jax.experimental import pallas as pl
from jax.experimental.pallas import tpu as pltpu

D_MODEL = 1024
D_FF = 2816
N_MOD = 9
DEPTH = 2
GRID_W = 64
HEAD_DIM = 64
W_POOL = 512
W_CONV = 512
POOL_WINDOWS = (2, 4, 8, 16)
POOL_G = 128
W_NAT = 512
W_DIFF = 512
H_NAT = 8
H_DIFF = 4
NAT_WIN_R = 8
NAT_WIN_C = 16
ROPE_THETA = 10000.0
ATTN_SCALE = HEAD_DIM ** -0.5
LOG2E = math.log2(math.e)
Q_SCALE = ATTN_SCALE * LOG2E
EPS = 1e-6
NEG = -1e30

LANES = 128
SUBLANES = 8
VMEM_LIMIT = 60 * 1024 * 1024

BF16 = jnp.bfloat16
F32 = jnp.float32


def _dot(a, b):
    return jnp.dot(a, b, preferred_element_type=F32)


def _dot_nt(a, b):
    return lax.dot_general(a, b, (((1,), (1,)), ((), ())), preferred_element_type=F32)


def _sigmoid(x):
    return 1.0 / (1.0 + jnp.exp(-x))


def _rms(x):
    return x * lax.rsqrt(jnp.mean(x * x, axis=-1, keepdims=True) + EPS)


def _modnorm(x, shift, scale, g):
    return _rms(x) * (g * (1.0 + scale)) + shift


def _ffn_half_step(x, mods, base, g, w_in_ref, w_out_ref, ck):
    shift, scale, gate = mods[base:base + 1], mods[base + 1:base + 2], mods[base + 2:base + 3]
    h = _modnorm(x, shift, scale, g).astype(BF16)
    acc = None
    for c in range(D_FF // ck):
        a = _dot(h, w_in_ref[:, c * ck:(c + 1) * ck])
        b = _dot(h, w_in_ref[:, D_FF + c * ck:D_FF + (c + 1) * ck])
        gated = ((a * _sigmoid(a)) * b).astype(BF16)
        t = _dot(gated, w_out_ref[c * ck:(c + 1) * ck, :])
        acc = t if acc is None else acc + t
    return x + (0.5 * gate) * acc


def _const_spec(shape):
    nd = len(shape)
    return pl.BlockSpec(shape, lambda i, _nd=nd: (0,) * _nd, pipeline_mode=pl.Buffered(1))


def _mods_kernel(cond_ref, w_ref, b_ref, o_ref):
    cnd = cond_ref[...]
    s = (cnd * _sigmoid(cnd)).astype(BF16)
    o_ref[0] = _dot(s, w_ref[0].astype(BF16)) + b_ref[0]


def _adaln_mods(cond, mod_w, mod_b):
    tn = 2304
    nt = (N_MOD * D_MODEL) // tn
    return pl.pallas_call(
        _mods_kernel,
        grid=(DEPTH, nt),
        in_specs=[
            pl.BlockSpec((SUBLANES, D_MODEL), lambda l, j: (0, 0)),
            pl.BlockSpec((1, D_MODEL, tn), lambda l, j: (l, 0, j)),
            pl.BlockSpec((1, 1, tn), lambda l, j: (l, 0, j)),
        ],
        out_specs=pl.BlockSpec((1, SUBLANES, tn), lambda l, j: (l, 0, j)),
        out_shape=jax.ShapeDtypeStruct((DEPTH, SUBLANES, N_MOD * D_MODEL), F32),
        compiler_params=pltpu.CompilerParams(vmem_limit_bytes=VMEM_LIMIT),
        name="adaln_mods",
    )(cond, mod_w, mod_b.reshape(DEPTH, 1, N_MOD * D_MODEL))


def _rope(x, cos_t, sin_a, sin_b):
    w = x.shape[1]
    reps = w // LANES
    c = jnp.concatenate([cos_t] * reps, axis=1)
    sa = jnp.concatenate([sin_a] * reps, axis=1)
    sb = jnp.concatenate([sin_b] * reps, axis=1)
    quarter = HEAD_DIM // 4
    up = pltpu.roll(x, w - quarter, 1)
    dn = pltpu.roll(x, quarter, 1)
    return x * c + up * sa + dn * sb


def _side_casts(cast_ins, cast_outs):
    for ci, co in zip(cast_ins, cast_outs):
        co[0] = ci[...].astype(BF16)


def _ka_kernel(*refs, mode, ck, seq, n_cast):
    n_in = {"even": 7, "ctx": 7, "lat": 10}[mode]
    x_ref, mods_ref, g1_ref, w_in_ref, w_out_ref, gm_ref, wp_ref = refs[:7]
    outs = refs[n_in + n_cast:]
    _side_casts(refs[n_in:n_in + n_cast], outs[len(outs) - n_cast:])
    mods = mods_ref[0]
    x1 = _ffn_half_step(x_ref[...], mods, 0, g1_ref[...], w_in_ref, w_out_ref, ck)
    outs[0][...] = x1
    h = _modnorm(x1, mods[3:4], mods[4:5], gm_ref[...]).astype(BF16)
    if mode == "even":
        outs[1][...] = _dot(h, wp_ref[...])
        return
    if mode == "lat":
        cos_ref, sa_ref, sb_ref = refs[7:10]
    for p in range(6):
        if mode == "ctx" and p in (1, 2, 4):
            ut = lax.dot_general(wp_ref[:, p * 512:(p + 1) * 512], h, (((0,), (1,)), ((), ())),
                                 preferred_element_type=F32)
            for bb in range(ut.shape[1] // seq):
                outs[1 + p][bb] = ut[:, bb * seq:(bb + 1) * seq]
            continue
        u = _dot(h, wp_ref[:, p * 512:(p + 1) * 512])
        if mode == "lat" and p in (3, 4):
            u = _rope(u, cos_ref[...], sa_ref[...], sb_ref[...])
        if p in (0, 3):
            u = u * Q_SCALE
        if mode == "ctx" and p == 5:
            for hh in range(H_DIFF):
                outs[1 + p][:, hh, :] = u[:, hh * LANES:(hh + 1) * LANES]
            outs[2 + p][...] = u.astype(BF16)
            continue
        outs[1 + p][...] = u.astype(outs[1 + p].dtype)


def _layer_spec(wl):
    w, l = wl
    nd = w.ndim - 1
    return pl.BlockSpec((None,) + w.shape[1:], lambda i, _l=l, _nd=nd: (_l,) + (0,) * _nd,
                        pipeline_mode=pl.Buffered(1))


def _cast_specs(casts, nt):
    in_specs, args, out_specs, out_shape = [], [], [], []
    for w, l in casts:
        _, rows, cols = w.shape
        rb = rows // nt
        assert rows % nt == 0 and rb % (2 * SUBLANES) == 0
        in_specs.append(pl.BlockSpec((None, rb, cols), lambda i, _l=l: (_l, i, 0)))
        args.append(w)
        out_specs.append(pl.BlockSpec((1, rb, cols), lambda i: (0, i, 0)))
        out_shape.append(jax.ShapeDtypeStruct((1, rows, cols), BF16))
    return in_specs, args, out_specs, out_shape


def _cast_kernel(*refs):
    n = len(refs) // 2
    _side_casts(refs[:n], refs[n:])


def _cast_call(casts, nt=8):
    c_in, c_args, c_out, c_shape = _cast_specs(casts, nt)
    return pl.pallas_call(
        _cast_kernel, grid=(nt,), in_specs=c_in, out_specs=c_out, out_shape=c_shape,
        compiler_params=pltpu.CompilerParams(dimension_semantics=("arbitrary",), vmem_limit_bytes=VMEM_LIMIT),
        name="cast_weights",
    )(*c_args)


def _kernel_a(x, mods_l, row_fn, g1, w_in, w_out, gm, wp, mode, tm, rope_tabs=None, seq=None, casts=(), ck=256):
    t = x.shape[0]
    nt = t // tm
    tok = lambda w: pl.BlockSpec((tm, w), lambda i: (i, 0))
    weights = [g1, w_in, w_out, gm, wp]
    in_specs = [tok(D_MODEL), pl.BlockSpec((1, N_MOD, D_MODEL), lambda i: (row_fn(i), 0, 0))]
    in_specs += [_layer_spec(w) for w in weights]
    args = [x, mods_l] + [w[0] for w in weights]
    n_proj = wp[0].shape[2]
    out_shape = [jax.ShapeDtypeStruct((t, D_MODEL), F32)]
    out_specs = [tok(D_MODEL)]
    if mode == "even":
        out_shape.append(jax.ShapeDtypeStruct((t, n_proj), F32))
        out_specs.append(tok(n_proj))
    elif mode == "lat":
        seq_tiles = rope_tabs[0].shape[0] // tm
        for tab in rope_tabs:
            in_specs.append(pl.BlockSpec((tm, LANES), lambda i, _s=seq_tiles: (i % _s, 0)))
            args.append(tab)
        for _ in range(6):
            out_shape.append(jax.ShapeDtypeStruct((t, 512), BF16))
            out_specs.append(tok(512))
    else:
        spt = tm // seq
        for p in range(6):
            if p in (1, 2, 4):
                out_shape.append(jax.ShapeDtypeStruct((t // seq, 512, seq), F32))
                out_specs.append(pl.BlockSpec((spt, 512, seq), lambda i: (i, 0, 0)))
            elif p == 5:
                out_shape.append(jax.ShapeDtypeStruct((t, H_DIFF, 2 * HEAD_DIM), F32))
                out_specs.append(pl.BlockSpec((tm, H_DIFF, 2 * HEAD_DIM), lambda i: (i, 0, 0)))
                out_shape.append(jax.ShapeDtypeStruct((t, 512), BF16))
                out_specs.append(tok(512))
            else:
                out_shape.append(jax.ShapeDtypeStruct((t, 512), BF16))
                out_specs.append(tok(512))
    c_in, c_args, c_out, c_shape = _cast_specs(casts, nt)
    return pl.pallas_call(
        functools.partial(_ka_kernel, mode=mode, ck=ck, seq=seq, n_cast=len(casts)),
        grid=(nt,),
        in_specs=in_specs + c_in,
        out_specs=out_specs + c_out,
        out_shape=out_shape + c_shape,
        compiler_params=pltpu.CompilerParams(
            dimension_semantics=("arbitrary",), vmem_limit_bytes=VMEM_LIMIT),
        name="ffn1_proj_" + mode,
    )(*args, *c_args)


HALO = 8


def _even_mix(ext_ref, base, n, pos, seq_len, pw_ref, ps_ref, cw_ref):
    def rows(j, lo, hi):
        return ext_ref[base + HALO + j:base + HALO + j + n, lo:hi]

    ya = []
    for g, win in enumerate(POOL_WINDOWS):
        half = win // 2
        lo_c, hi_c = g * POOL_G, (g + 1) * POOL_G
        s = rows(-half, lo_c, hi_c)
        for j in range(-half + 1, half):
            s = s + rows(j, lo_c, hi_c)
        lo = jnp.clip(pos - half, 0, seq_len - 1)
        hi = jnp.clip(pos + half - 1, 0, seq_len - 1)
        cnt = (hi - lo + 1).astype(F32)
        d = (s / cnt - rows(0, lo_c, hi_c)).astype(BF16)
        ya.append(_dot(d, pw_ref[g]))
    ya = jnp.concatenate(ya, axis=1) * ps_ref[...]

    def z(j):
        return rows(j, W_POOL + 2 * W_CONV, W_POOL + 3 * W_CONV) * rows(j, W_POOL, W_POOL + W_CONV)

    y = z(-1) * cw_ref[0:1, :] + z(0) * cw_ref[1:2, :] + z(1) * cw_ref[2:3, :]
    yb = rows(0, W_POOL + W_CONV, W_POOL + 2 * W_CONV) * y
    return jnp.concatenate([ya, yb], axis=1)


def _kb_kernel(*refs, mode, ck, tm, seq_len, final, n_cast):
    x_ref, mods_ref = refs[0], refs[1]
    k = 2
    if mode == "even":
        u_ref, up_ref, un_ref, pw_ref, ps_ref, cw_ref = refs[k:k + 6]
        k += 6
    else:
        n_parts = 2 if mode == "odd2" else 1
        o_refs = refs[k:k + n_parts]
        k += n_parts
    wmo_ref, g2_ref, w_in_ref, w_out_ref = refs[k:k + 4]
    k += 4
    if final:
        gf_ref = refs[k]
        k += 1
    out_ref = refs[k + n_cast]
    _side_casts(refs[k:k + n_cast], refs[k + n_cast + 1:k + 2 * n_cast + 1])
    k += 2 * n_cast + 1
    mods = mods_ref[0]
    if mode == "even":
        ext_ref = refs[k]
        n_seg = min(seq_len, tm)
        stride = n_seg + 2 * HALO
        row0 = pl.program_id(0) * tm
        feats = []
        for si in range(tm // n_seg):
            base = si * stride
            if n_seg == seq_len:
                before = after = jnp.zeros((HALO, ext_ref.shape[1]), F32)
            else:
                first = jnp.bitwise_and(row0, seq_len - 1) == 0
                last = jnp.bitwise_and(row0 + tm, seq_len - 1) == 0
                before = jnp.where(first, 0.0, up_ref[...])
                after = jnp.where(last, 0.0, un_ref[...])
            ext_ref[base:base + HALO, :] = before
            ext_ref[base + HALO:base + HALO + n_seg, :] = u_ref[si * n_seg:(si + 1) * n_seg, :]
            ext_ref[base + HALO + n_seg:base + stride, :] = after
            row = lax.broadcasted_iota(jnp.int32, (n_seg, LANES), 0) + (row0 + si * n_seg)
            pos = jnp.bitwise_and(row, seq_len - 1)
            feats.append(_even_mix(ext_ref, base, n_seg, pos, seq_len, pw_ref, ps_ref, cw_ref))
        feat = jnp.concatenate(feats, axis=0).astype(BF16)
        y = _dot(feat, wmo_ref[...])
    else:
        y = None
        off = 0
        for o_ref in o_refs:
            w = o_ref.shape[1]
            t = _dot(o_ref[...], wmo_ref[off:off + w, :])
            y = t if y is None else y + t
            off += w
    x2 = x_ref[...] + mods[5:6] * y
    x3 = _ffn_half_step(x2, mods, 6, g2_ref[...], w_in_ref, w_out_ref, ck)
    if final:
        x3 = _rms(x3) * gf_ref[...]
    out_ref[...] = x3


def _kernel_b(x, mods_l, row_fn, mix_in, mix_params, wmo, g2, w_in, w_out, mode, tm, seq_len=None,
              final_g=None, casts=(), ck=256):
    t = x.shape[0]
    nt = t // tm
    tok = lambda w: pl.BlockSpec((tm, w), lambda i: (i, 0))
    in_specs = [tok(D_MODEL), pl.BlockSpec((1, N_MOD, D_MODEL), lambda i: (row_fn(i), 0, 0))]
    args = [x, mods_l]
    scratch = []
    if mode == "even":
        u = mix_in[0]
        wu = u.shape[1]
        hb = tm // HALO
        last = t // HALO - 1
        in_specs += [
            tok(wu),
            pl.BlockSpec((HALO, wu), lambda i: (jnp.maximum(i * hb - 1, 0), 0)),
            pl.BlockSpec((HALO, wu), lambda i: (jnp.minimum((i + 1) * hb, last), 0)),
        ]
        args += [u, u, u]
        for w in mix_params:
            in_specs.append(_layer_spec(w))
            args.append(w[0])
        n_seg = min(seq_len, tm)
        assert tm % n_seg == 0 and seq_len % n_seg == 0 and seq_len & (seq_len - 1) == 0
        scratch.append(pltpu.VMEM(((tm // n_seg) * (n_seg + 2 * HALO), wu), F32))
    else:
        for o in mix_in:
            in_specs.append(tok(o.shape[1]))
            args.append(o)
    in_specs += [_layer_spec(w) for w in (wmo, g2, w_in, w_out)]
    args += [w[0] for w in (wmo, g2, w_in, w_out)]
    final = final_g is not None
    if final:
        in_specs.append(_const_spec((1, D_MODEL)))
        args.append(final_g.reshape(1, D_MODEL))
    kmode = mode if mode == "even" else ("odd2" if len(mix_in) == 2 else "odd1")
    c_in, c_args, c_out, c_shape = _cast_specs(casts, nt)
    return pl.pallas_call(
        functools.partial(_kb_kernel, mode=kmode, ck=ck, tm=tm, seq_len=seq_len, final=final, n_cast=len(casts)),
        grid=(nt,),
        in_specs=in_specs + c_in,
        out_specs=[tok(D_MODEL)] + c_out,
        out_shape=[jax.ShapeDtypeStruct((t, D_MODEL), F32)] + c_shape,
        scratch_shapes=scratch,
        compiler_params=pltpu.CompilerParams(
            dimension_semantics=("arbitrary",), vmem_limit_bytes=VMEM_LIMIT),
        name="mix_ffn2_" + kmode,
    )(*args, *c_args)


def _diff_lambda(lam_ref, lam_init):
    lp = lam_ref[...]
    s1 = jnp.sum(lp[0:1] * lp[1:2], axis=-1, keepdims=True)
    s2 = jnp.sum(lp[2:3] * lp[3:4], axis=-1, keepdims=True)
    return jnp.exp(s1) - jnp.exp(s2) + lam_init


def _half_masks():
    lane = lax.broadcasted_iota(jnp.int32, (1, LANES), 1)
    lo = lane < HEAD_DIM
    return lo, jnp.logical_not(lo)


def _diff_out_norm(o, dn_ref, lam_init):
    return (_rms(o) * dn_ref[...]) * (1.0 - lam_init)


def _ctx_attn_kernel(nq_ref, nkt_ref, nvt_ref, dq_ref, dkt_ref, dv_ref, lam_ref, dn_ref, o_ref, *, lam_init, seq):
    lo, hi = _half_masks()
    lam = _diff_lambda(lam_ref, lam_init)
    ones_t = jnp.ones((LANES, seq), BF16)
    ones_c = jnp.ones((seq, LANES), BF16)
    for b in range(nkt_ref.shape[0]):
        rows = slice(b * seq, (b + 1) * seq)
        for hp in range(H_NAT // 2):
            sl = slice(hp * LANES, (hp + 1) * LANES)
            q = nq_ref[rows, sl]
            kt = nkt_ref[b, sl, :].astype(BF16)
            vt = jnp.concatenate([nvt_ref[b, sl, :].astype(BF16), ones_t], axis=0)
            outs = []
            for msk in (lo, hi):
                qm = jnp.where(msk, q, jnp.zeros_like(q))
                s = _dot(qm, kt)
                e = jnp.exp2(s - jnp.max(s, axis=-1, keepdims=True)).astype(BF16)
                ol = _dot_nt(e, vt)
                outs.append(ol[:, :LANES] * (1.0 / ol[:, LANES:]))
            o_ref[rows, sl] = jnp.where(lo, outs[0], outs[1]).astype(o_ref.dtype)
        for h in range(H_DIFF):
            sl = slice(h * LANES, (h + 1) * LANES)
            q = dq_ref[rows, sl]
            kt = dkt_ref[b, sl, :].astype(BF16)
            v1 = jnp.concatenate([dv_ref[rows, sl], ones_c], axis=1)
            ols = []
            for msk in (lo, hi):
                qm = jnp.where(msk, q, jnp.zeros_like(q))
                s = _dot(qm, kt)
                e = jnp.exp2(s - jnp.max(s, axis=-1, keepdims=True)).astype(BF16)
                ol = _dot(e, v1)
                ols.append(ol[:, :LANES] * (1.0 / ol[:, LANES:]))
            o = _diff_out_norm(ols[0] - lam * ols[1], dn_ref, lam_init)
            o_ref[rows, W_NAT + h * LANES:W_NAT + (h + 1) * LANES] = o.astype(o_ref.dtype)


def _ctx_attention(parts, lam_p, dnorm, seq, lam_init, nb=4):
    nq, nkt, nvt, dq, dkt, dv = parts
    t = nq.shape[0]
    steps = t // (seq * nb)
    tok = pl.BlockSpec((nb * seq, 512), lambda i: (i, 0))
    fmaj = pl.BlockSpec((nb, 512, seq), lambda i: (i, 0, 0))
    return pl.pallas_call(
        functools.partial(_ctx_attn_kernel, lam_init=lam_init, seq=seq),
        grid=(steps,),
        in_specs=[tok, fmaj, fmaj, tok, fmaj, tok, _const_spec((4, HEAD_DIM)), _const_spec((1, 2 * HEAD_DIM))],
        out_specs=pl.BlockSpec((nb * seq, D_MODEL), lambda i: (i, 0)),
        out_shape=jax.ShapeDtypeStruct((t, D_MODEL), BF16),
        compiler_params=pltpu.CompilerParams(
            dimension_semantics=("arbitrary",), vmem_limit_bytes=VMEM_LIMIT),
        name="ctx_attention",
    )(nq, nkt, nvt, dq, dkt, dv, lam_p, dnorm.reshape(1, 2 * HEAD_DIM))


NAT_QROWS = 4
NAT_KROWS = 12
NAT_NDR = 2 * NAT_WIN_R


def _build_bias_tiles(rpb_ref, u_ref):
    c = lax.broadcasted_iota(jnp.int32, (GRID_W, LANES), 0)
    lane = lax.broadcasted_iota(jnp.int32, (GRID_W, LANES), 1)
    kc = jnp.bitwise_and(lane, GRID_W - 1)
    c0 = jnp.clip(c - NAT_WIN_C // 2, 0, GRID_W - NAT_WIN_C)
    col_ok = jnp.logical_and(kc >= c0, kc < c0 + NAT_WIN_C)
    left = lane < GRID_W
    n_dr = 2 * NAT_WIN_R - 1
    for h in range(H_NAT):
        tl, tr = [], []
        for dr in range(n_dr):
            row = jnp.broadcast_to(rpb_ref[h, dr:dr + 1, :] * LOG2E, (GRID_W, LANES))
            tl.append(pltpu.roll(row, LANES - (NAT_WIN_C - 1), 1, stride=1, stride_axis=0))
            tr.append(pltpu.roll(row, LANES - (NAT_WIN_C - 1) + GRID_W, 1, stride=1, stride_axis=0))
        for di in range(NAT_NDR):
            dl = di - NAT_WIN_R + (NAT_WIN_R - 1)
            drr = dl + 1
            lv = tl[dl] if 0 <= dl < n_dr else jnp.full((GRID_W, LANES), NEG, F32)
            rv = tr[drr] if 0 <= drr < n_dr else jnp.full((GRID_W, LANES), NEG, F32)
            u_ref[h, di] = jnp.where(col_ok, jnp.where(left, lv, rv), NEG)


def _nat_lat_kernel(q_ref, k_ref, v_ref, kc_ref, vc_ref, rpb_ref, o_ref, u_ref, s_ref):
    b = pl.program_id(1)

    @pl.when(jnp.logical_and(pl.program_id(0) == 0, b == 0))
    def _():
        _build_bias_tiles(rpb_ref, u_ref)

    rows = k_ref.shape[0] // GRID_W
    kb = jnp.clip(b * NAT_QROWS - NAT_WIN_R // 2, 0, rows - NAT_KROWS)
    koff = pl.multiple_of(kb * GRID_W, GRID_W)
    lo, hi = _half_masks()
    left = lax.broadcasted_iota(jnp.int32, (GRID_W, LANES), 1) < GRID_W
    nq = NAT_QROWS * GRID_W
    for hp in range(H_NAT // 2):
        sl = slice(hp * LANES, (hp + 1) * LANES)
        q = q_ref[:, sl]
        k = k_ref[pl.ds(koff, NAT_KROWS * GRID_W), sl]
        v = v_ref[pl.ds(koff, NAT_KROWS * GRID_W), sl]
        kc = kc_ref[:, sl]
        vc = vc_ref[:, sl]
        v1 = jnp.concatenate([v, jnp.ones_like(v)], axis=1)
        vc1 = jnp.concatenate([vc, jnp.ones_like(vc)], axis=1)
        outs = []
        for e, msk in enumerate((lo, hi)):
            h = 2 * hp + e
            qm = jnp.where(msk, q, jnp.zeros_like(q))
            s_ref[...] = _dot_nt(qm, k)
            for i in range(NAT_QROWS):
                r = b * NAT_QROWS + i
                r0 = jnp.clip(r - NAT_WIN_R // 2, 0, rows - NAT_WIN_R)
                for jp in range(NAT_KROWS // 2):
                    kl = kb + 2 * jp
                    ok_l = jnp.logical_and(kl >= r0, kl < r0 + NAT_WIN_R)
                    ok_r = jnp.logical_and(kl + 1 >= r0, kl + 1 < r0 + NAT_WIN_R)
                    di = jnp.clip(kl - r + NAT_WIN_R, 0, NAT_NDR - 1)
                    ok = jnp.where(left, ok_l.astype(jnp.int32), ok_r.astype(jnp.int32))
                    bias = jnp.where(ok > 0, u_ref[h, di], NEG)
                    tile = (slice(i * GRID_W, (i + 1) * GRID_W), slice(jp * LANES, (jp + 1) * LANES))
                    s_ref[tile] = s_ref[tile] + bias
            s_ctx = _dot_nt(qm, kc)
            s_loc = s_ref[...]
            m = jnp.maximum(jnp.max(s_loc, axis=-1, keepdims=True), jnp.max(s_ctx, axis=-1, keepdims=True))
            ol = (_dot(jnp.exp2(s_loc - m).astype(BF16), v1) + _dot(jnp.exp2(s_ctx - m).astype(BF16), vc1))
            outs.append(ol[:, :LANES] * (1.0 / ol[:, LANES:]))
        o_ref[:, sl] = jnp.where(lo, outs[0], outs[1]).astype(o_ref.dtype)


def _nat_lat_attention(q, k, v, kc, vc, rpb_pad, batch, seq):
    nq = NAT_QROWS * GRID_W
    nblk = seq // nq
    p = kc.shape[1]
    return pl.pallas_call(
        _nat_lat_kernel,
        grid=(batch, nblk),
        in_specs=[
            pl.BlockSpec((nq, W_NAT), lambda bb, i: (bb * nblk + i, 0)),
            pl.BlockSpec((seq, W_NAT), lambda bb, i: (bb, 0)),
            pl.BlockSpec((seq, W_NAT), lambda bb, i: (bb, 0)),
            pl.BlockSpec((None, p, W_NAT), lambda bb, i: (bb, 0, 0)),
            pl.BlockSpec((None, p, W_NAT), lambda bb, i: (bb, 0, 0)),
            pl.BlockSpec(rpb_pad.shape, lambda bb, i: (0, 0, 0)),
        ],
        out_specs=pl.BlockSpec((nq, W_NAT), lambda bb, i: (bb * nblk + i, 0)),
        out_shape=jax.ShapeDtypeStruct((batch * seq, W_NAT), BF16),
        scratch_shapes=[
            pltpu.VMEM((H_NAT, NAT_NDR, GRID_W, LANES), F32),
            pltpu.VMEM((nq, NAT_KROWS * GRID_W), F32),
        ],
        compiler_params=pltpu.CompilerParams(
            dimension_semantics=("arbitrary", "arbitrary"), vmem_limit_bytes=VMEM_LIMIT),
        name="nat_latent_attention",
    )(q, k, v, kc, vc, rpb_pad)


DIFF_KCHUNK = 256


def _lane_fold(x, op):
    out = x[:, :LANES]
    for c in range(1, x.shape[1] // LANES):
        out = op(out, x[:, c * LANES:(c + 1) * LANES])
    return out


def _round_robin(*gens):
    gens = list(gens)
    while gens:
        for g in list(gens):
            try:
                next(g)
            except StopIteration:
                gens.remove(g)


def _diff_lat_kernel(q_ref, k_ref, v_ref, kc_ref, vc_ref, lam_ref, dn_ref, o_ref, sa_ref, sb_ref, *, lam_init):
    lo, hi = _half_masks()
    lam = _diff_lambda(lam_ref, lam_init)
    ck = DIFF_KCHUNK
    pieces = ([(k_ref, v_ref, c * ck) for c in range(k_ref.shape[0] // ck)]
              + [(kc_ref, vc_ref, c * ck) for c in range(kc_ref.shape[0] // ck)])
    row_max = {}

    def s_of(h):
        return (sa_ref, sb_ref)[h % 2]

    def scores(h):
        sl = slice(h * LANES, (h + 1) * LANES)
        q = q_ref[:, sl]
        ms = []
        for j, msk in enumerate((lo, hi)):
            qm = jnp.where(msk, q, jnp.zeros_like(q))
            mx = None
            for ci, (kr, _, off) in enumerate(pieces):
                s = _dot_nt(qm, kr[off:off + ck, sl])
                s_of(h)[j, :, ci * ck:(ci + 1) * ck] = s
                part = _lane_fold(s, jnp.maximum)
                mx = part if mx is None else jnp.maximum(mx, part)
                yield
            ms.append(jnp.max(mx, axis=-1, keepdims=True))
        row_max[h] = ms

    def weighted_values(h):
        sl = slice(h * LANES, (h + 1) * LANES)
        tq = q_ref.shape[0]
        acc = None
        for ci, (_, vr, off) in enumerate(pieces):
            cols = slice(ci * ck, (ci + 1) * ck)
            e = jnp.concatenate([jnp.exp2(s_of(h)[j, :, cols] - row_max[h][j]).astype(BF16) for j in range(2)], axis=0)
            vv = vr[off:off + ck, sl]
            t = _dot(e, jnp.concatenate([vv, jnp.ones_like(vv)], axis=1))
            acc = t if acc is None else acc + t
            yield
        on = acc[:, :LANES] * (1.0 / acc[:, LANES:])
        o = on[:tq] - lam * on[tq:]
        o_ref[:, sl] = _diff_out_norm(o, dn_ref, lam_init).astype(o_ref.dtype)

    _round_robin(scores(0))
    for h in range(H_DIFF):
        nxt = [scores(h + 1)] if h + 1 < H_DIFF else []
        _round_robin(weighted_values(h), *nxt)


def _diff_lat_attention(q, k, v, kc, vc, lam_p, dnorm, batch, seq, lam_init, tq=256):
    nqb = seq // tq
    p = kc.shape[1]
    s_scratch = pltpu.VMEM((2, tq, seq + p), F32)
    return pl.pallas_call(
        functools.partial(_diff_lat_kernel, lam_init=lam_init),
        grid=(batch, nqb),
        in_specs=[
            pl.BlockSpec((tq, W_DIFF), lambda bb, i: (bb * nqb + i, 0)),
            pl.BlockSpec((seq, W_DIFF), lambda bb, i: (bb, 0)),
            pl.BlockSpec((seq, W_DIFF), lambda bb, i: (bb, 0)),
            pl.BlockSpec((None, p, W_DIFF), lambda bb, i: (bb, 0, 0)),
            pl.BlockSpec((None, p, W_DIFF), lambda bb, i: (bb, 0, 0)),
            pl.BlockSpec((4, HEAD_DIM), lambda bb, i: (0, 0)),
            pl.BlockSpec((1, 2 * HEAD_DIM), lambda bb, i: (0, 0)),
        ],
        out_specs=pl.BlockSpec((tq, W_DIFF), lambda bb, i: (bb * nqb + i, 0)),
        out_shape=jax.ShapeDtypeStruct((batch * seq, W_DIFF), BF16),
        scratch_shapes=[s_scratch, s_scratch],
        compiler_params=pltpu.CompilerParams(
            dimension_semantics=("arbitrary", "arbitrary"), vmem_limit_bytes=VMEM_LIMIT),
        name="diff_latent_attention",
    )(q, k, v, kc, vc, lam_p, dnorm.reshape(1, 2 * HEAD_DIM))


def _rope_tables(n):
    t = np.arange(n)
    row = (t // GRID_W).astype(np.float64)
    col = (t % GRID_W).astype(np.float64)
    quarter = HEAD_DIM // 4
    inv = 1.0 / (ROPE_THETA ** (np.arange(quarter) / quarter))
    cr, sr = np.cos(row[:, None] * inv[None]), np.sin(row[:, None] * inv[None])
    cc, sc = np.cos(col[:, None] * inv[None]), np.sin(col[:, None] * inv[None])
    zero = np.zeros_like(sr)
    cos_t = np.concatenate([cr, cr, cc, cc], axis=1)
    sin_a = np.concatenate([-sr, zero, -sc, zero], axis=1)
    sin_b = np.concatenate([zero, sr, zero, sc], axis=1)
    tile = lambda a: jnp.asarray(np.concatenate([a, a], axis=1).astype(np.float32))
    return tile(cos_t), tile(sin_a), tile(sin_b)


def kernel(x_prompt, x_sample, cache_nat_k, cache_nat_v, cache_diff_k, cache_diff_v, c, c_ctx, mod_w, mod_b, norm_ffn1, ffn1_w_in, ffn1_w_out, norm_mix, even_w_in, pool_w, pool_scale, conv_w, odd_w_in, nat_rpb, diff_lambda, diff_norm, mix_w_out, norm_ffn2, ffn2_w_in, ffn2_w_out, final_norm):
    batch, seq, _ = x_prompt.shape
    dbatch, dseq, _ = x_sample.shape
    past = cache_nat_k.shape[2]
    tm = 512

    cond = jnp.zeros((SUBLANES, D_MODEL), F32).at[0].set(c_ctx).at[1:1 + dbatch].set(c)
    mods = _adaln_mods(cond, mod_w, mod_b).reshape(DEPTH, SUBLANES, N_MOD, D_MODEL)

    ctx_row = lambda i: 0
    lat_row = lambda i: 1 + (i * tm) // dseq

    cache_bf = lambda a, w: a[:, 0].reshape(dbatch, past, w).astype(BF16)
    gain = lambda g: g.reshape(g.shape[0], 1, g.shape[1])
    g_ffn1, g_mix, g_ffn2 = gain(norm_ffn1), gain(norm_mix), gain(norm_ffn2)
    pool3 = pool_w.reshape(pool_w.shape[0], len(POOL_WINDOWS) * POOL_G, POOL_G)

    f1_in0, f1_out0, w_even0 = _cast_call([(ffn1_w_in, 0), (ffn1_w_out, 0), (even_w_in, 0)])

    ctx = x_prompt.reshape(batch * seq, D_MODEL)
    lat = x_sample.reshape(dbatch * dseq, D_MODEL)

    l = 0
    x1c, uc, f2_in0, f2_out0, wmo0, pool0 = _kernel_a(
        ctx, mods[l], ctx_row, (g_ffn1, l), (f1_in0, 0), (f1_out0, 0), (g_mix, l), (w_even0, 0), "even", tm,
        casts=[(ffn2_w_in, 0), (ffn2_w_out, 0), (mix_w_out, 0), (pool3, 0)])
    even_params = ((pool0.reshape(1, len(POOL_WINDOWS), POOL_G, POOL_G), 0), (gain(pool_scale), 0), (conv_w, 0))
    ctx, f1_in1, f1_out1 = _kernel_b(
        x1c, mods[l], ctx_row, [uc], even_params, (wmo0, 0), (g_ffn2, l), (f2_in0, 0), (f2_out0, 0), "even", tm,
        seq_len=seq, casts=[(ffn1_w_in, 1), (ffn1_w_out, 1)])
    x1l, ul, w_odd1, f2_in1 = _kernel_a(
        lat, mods[l], lat_row, (g_ffn1, l), (f1_in0, 0), (f1_out0, 0), (g_mix, l), (w_even0, 0), "even", tm,
        casts=[(odd_w_in, 0), (ffn2_w_in, 1)])
    lat, f2_out1, wmo1 = _kernel_b(
        x1l, mods[l], lat_row, [ul], even_params, (wmo0, 0), (g_ffn2, l), (f2_in0, 0), (f2_out0, 0), "even", tm,
        seq_len=dseq, casts=[(ffn2_w_out, 1), (mix_w_out, 1)])

    l = 1
    lam_init = 0.8 - 0.6 * math.exp(-0.3 * l)
    ffn1 = ((g_ffn1, l), (f1_in1, 0), (f1_out1, 0), (g_mix, l), (w_odd1, 0))
    ffn2 = ((wmo1, 0), (g_ffn2, l), (f2_in1, 0), (f2_out1, 0))
    ctx1, nq, nkt, nvt, dq, dkt, dv, dvb = _kernel_a(ctx, mods[l], ctx_row, *ffn1, "ctx", tm, seq=seq)
    o_ctx = _ctx_attention([nq, nkt, nvt, dq, dkt, dvb], diff_lambda[0], diff_norm[0], seq, lam_init)
    y_prompt, = _kernel_b(ctx1, mods[l], ctx_row, [o_ctx], None, *ffn2, "odd", tm, final_g=final_norm)

    lat1, lq, lk, lv, ldq, ldk, ldv = _kernel_a(lat, mods[l], lat_row, *ffn1, "lat", tm,
                                                 rope_tabs=_rope_tables(dseq))
    rpb_pad = jnp.zeros((H_NAT, 2 * NAT_WIN_R, LANES), F32).at[:, :2 * NAT_WIN_R - 1,
                                                                :2 * NAT_WIN_C - 1].set(nat_rpb[0])
    o_nat = _nat_lat_attention(lq, lk, lv, cache_bf(cache_nat_k, W_NAT), cache_bf(cache_nat_v, W_NAT),
                               rpb_pad, dbatch, dseq)
    o_diff = _diff_lat_attention(ldq, ldk, ldv, cache_bf(cache_diff_k, W_DIFF), cache_bf(cache_diff_v, W_DIFF),
                                 diff_lambda[0], diff_norm[0], dbatch, dseq, lam_init)
    y_sample, = _kernel_b(lat1, mods[l], lat_row, [o_nat, o_diff], None, *ffn2, "odd", tm, final_g=final_norm)

    new_nat_k = nkt.reshape(batch, 1, H_NAT, HEAD_DIM, seq).transpose(0, 1, 4, 2, 3)
    new_nat_v = nvt.reshape(batch, 1, H_NAT, HEAD_DIM, seq).transpose(0, 1, 4, 2, 3)
    new_diff_k = dkt.reshape(batch, 1, H_DIFF, 2, HEAD_DIM, seq).transpose(0, 1, 5, 2, 3, 4)
    new_diff_v = dv.reshape(batch, 1, seq, H_DIFF, 2 * HEAD_DIM)
    return (y_prompt.reshape(batch, seq, D_MODEL), y_sample.reshape(dbatch, dseq, D_MODEL),
            new_nat_k, new_nat_v, new_diff_k, new_diff_v)
```

```python
import functools
import math

import numpy as np
import jax
import jax.numpy as jnp
from jax import lax
from jax.experimental import pallas as pl
from jax.experimental.pallas import tpu as pltpu

D_MODEL = 1024
D_FF = 2816
N_MOD = 9
DEPTH = 2
GRID_W = 64
HEAD_DIM = 64
W_POOL = 512
W_CONV = 512
POOL_WINDOWS = (2, 4, 8, 16)
POOL_G = 128
W_NAT = 512
W_DIFF = 512
H_NAT = 8
H_DIFF = 4
NAT_WIN_R = 8
NAT_WIN_C = 16
ROPE_THETA = 10000.0
ATTN_SCALE = HEAD_DIM ** -0.5
LOG2E = math.log2(math.e)
Q_SCALE = ATTN_SCALE * LOG2E
EPS = 1e-6
NEG = -1e30

LANES = 128
SUBLANES = 8
VMEM_LIMIT = 60 * 1024 * 1024

BF16 = jnp.bfloat16
F32 = jnp.float32


def _dot(a, b):
    return jnp.dot(a, b, preferred_element_type=F32)


def _dot_nt(a, b):
    return lax.dot_general(a, b, (((1,), (1,)), ((), ())), preferred_element_type=F32)


def _sigmoid(x):
    return 1.0 / (1.0 + jnp.exp(-x))


def _rms(x):
    return x * lax.rsqrt(jnp.mean(x * x, axis=-1, keepdims=True) + EPS)


def _modnorm(x, shift, scale, g):
    return _rms(x) * (g * (1.0 + scale)) + shift


def _ffn_half_step(x, mods, base, g, w_in_ref, w_out_ref, ck):
    shift, scale, gate = mods[base:base + 1], mods[base + 1:base + 2], mods[base + 2:base + 3]
    h = _modnorm(x, shift, scale, g).astype(BF16)
    acc = None
    for c in range(D_FF // ck):
        a = _dot(h, w_in_ref[:, c * ck:(c + 1) * ck])
        b = _dot(h, w_in_ref[:, D_FF + c * ck:D_FF + (c + 1) * ck])
        gated = ((a * _sigmoid(a)) * b).astype(BF16)
        t = _dot(gated, w_out_ref[c * ck:(c + 1) * ck, :])
        acc = t if acc is None else acc + t
    return x + (0.5 * gate) * acc


def _const_spec(shape):
    nd = len(shape)
    return pl.BlockSpec(shape, lambda i, _nd=nd: (0,) * _nd, pipeline_mode=pl.Buffered(1))


def _mods_kernel(cond_ref, w_ref, b_ref, o_ref):
    cnd = cond_ref[...]
    s = (cnd * _sigmoid(cnd)).astype(BF16)
    o_ref[0] = _dot(s, w_ref[0].astype(BF16)) + b_ref[0]


def _adaln_mods(cond, mod_w, mod_b):
    tn = 2304
    nt = (N_MOD * D_MODEL) // tn
    return pl.pallas_call(
        _mods_kernel,
        grid=(DEPTH, nt),
        in_specs=[
            pl.BlockSpec((SUBLANES, D_MODEL), lambda l, j: (0, 0)),
            pl.BlockSpec((1, D_MODEL, tn), lambda l, j: (l, 0, j)),
            pl.BlockSpec((1, 1, tn), lambda l, j: (l, 0, j)),
        ],
        out_specs=pl.BlockSpec((1, SUBLANES, tn), lambda l, j: (l, 0, j)),
        out_shape=jax.ShapeDtypeStruct((DEPTH, SUBLANES, N_MOD * D_MODEL), F32),
        compiler_params=pltpu.CompilerParams(vmem_limit_bytes=VMEM_LIMIT),
        name="adaln_mods",
    )(cond, mod_w, mod_b.reshape(DEPTH, 1, N_MOD * D_MODEL))


def _rope(x, cos_t, sin_a, sin_b):
    w = x.shape[1]
    reps = w // LANES
    c = jnp.concatenate([cos_t] * reps, axis=1)
    sa = jnp.concatenate([sin_a] * reps, axis=1)
    sb = jnp.concatenate([sin_b] * reps, axis=1)
    quarter = HEAD_DIM // 4
    up = pltpu.roll(x, w - quarter, 1)
    dn = pltpu.roll(x, quarter, 1)
    return x * c + up * sa + dn * sb


def _side_casts(cast_ins, cast_outs):
    for ci, co in zip(cast_ins, cast_outs):
        co[0] = ci[...].astype(BF16)


def _ka_kernel(*refs, mode, ck, seq, n_cast):
    n_in = {"even": 7, "ctx": 7, "lat": 10}[mode]
    x_ref, mods_ref, g1_ref, w_in_ref, w_out_ref, gm_ref, wp_ref = refs[:7]
    outs = refs[n_in + n_cast:]
    _side_casts(refs[n_in:n_in + n_cast], outs[len(outs) - n_cast:])
    mods = mods_ref[0]
    x1 = _ffn_half_step(x_ref[...], mods, 0, g1_ref[...], w_in_ref, w_out_ref, ck)
    outs[0][...] = x1
    h = _modnorm(x1, mods[3:4], mods[4:5], gm_ref[...]).astype(BF16)
    if mode == "even":
        outs[1][...] = _dot(h, wp_ref[...])
        return
    if mode == "lat":
        cos_ref, sa_ref, sb_ref = refs[7:10]
    for p in range(6):
        if mode == "ctx" and p in (1, 2, 4):
            ut = lax.dot_general(wp_ref[:, p * 512:(p + 1) * 512], h, (((0,), (1,)), ((), ())),
                                 preferred_element_type=F32)
            for bb in range(ut.shape[1] // seq):
                outs[1 + p][bb] = ut[:, bb * seq:(bb + 1) * seq]
            continue
        u = _dot(h, wp_ref[:, p * 512:(p + 1) * 512])
        if mode == "lat" and p in (3, 4):
            u = _rope(u, cos_ref[...], sa_ref[...], sb_ref[...])
        if p in (0, 3):
            u = u * Q_SCALE
        if mode == "ctx" and p == 5:
            outs[1 + p][...] = u.reshape(u.shape[0], H_DIFF, 2 * HEAD_DIM)
            outs[2 + p][...] = u.astype(BF16)
            continue
        outs[1 + p][...] = u.astype(outs[1 + p].dtype)


def _layer_spec(wl):
    w, l = wl
    nd = w.ndim - 1
    return pl.BlockSpec((None,) + w.shape[1:], lambda i, _l=l, _nd=nd: (_l,) + (0,) * _nd,
                        pipeline_mode=pl.Buffered(1))


def _cast_specs(casts, nt):
    in_specs, args, out_specs, out_shape = [], [], [], []
    for w, l in casts:
        _, rows, cols = w.shape
        rb = rows // nt
        assert rows % nt == 0 and rb % (2 * SUBLANES) == 0
        in_specs.append(pl.BlockSpec((None, rb, cols), lambda i, _l=l: (_l, i, 0)))
        args.append(w)
        out_specs.append(pl.BlockSpec((1, rb, cols), lambda i: (0, i, 0)))
        out_shape.append(jax.ShapeDtypeStruct((1, rows, cols), BF16))
    return in_specs, args, out_specs, out_shape


def _cast_kernel(*refs):
    n = len(refs) // 2
    _side_casts(refs[:n], refs[n:])


def _cast_call(casts, nt=8):
    c_in, c_args, c_out, c_shape = _cast_specs(casts, nt)
    return pl.pallas_call(
        _cast_kernel, grid=(nt,), in_specs=c_in, out_specs=c_out, out_shape=c_shape,
        compiler_params=pltpu.CompilerParams(dimension_semantics=("arbitrary",), vmem_limit_bytes=VMEM_LIMIT),
        name="cast_weights",
    )(*c_args)


def _kernel_a(x, mods_l, row_fn, g1, w_in, w_out, gm, wp, mode, tm, rope_tabs=None, seq=None, casts=(), ck=256):
    t = x.shape[0]
    nt = t // tm
    tok = lambda w: pl.BlockSpec((tm, w), lambda i: (i, 0))
    weights = [g1, w_in, w_out, gm, wp]
    in_specs = [tok(D_MODEL), pl.BlockSpec((1, N_MOD, D_MODEL), lambda i: (row_fn(i), 0, 0))]
    in_specs += [_layer_spec(w) for w in weights]
    args = [x, mods_l] + [w[0] for w in weights]
    n_proj = wp[0].shape[2]
    out_shape = [jax.ShapeDtypeStruct((t, D_MODEL), F32)]
    out_specs = [tok(D_MODEL)]
    if mode == "even":
        out_shape.append(jax.ShapeDtypeStruct((t, n_proj), F32))
        out_specs.append(tok(n_proj))
    elif mode == "lat":
        seq_tiles = rope_tabs[0].shape[0] // tm
        for tab in rope_tabs:
            in_specs.append(pl.BlockSpec((tm, LANES), lambda i, _s=seq_tiles: (i % _s, 0)))
            args.append(tab)
        for _ in range(6):
            out_shape.append(jax.ShapeDtypeStruct((t, 512), BF16))
            out_specs.append(tok(512))
    else:
        spt = tm // seq
        for p in range(6):
            if p in (1, 2, 4):
                out_shape.append(jax.ShapeDtypeStruct((t // seq, 512, seq), F32))
                out_specs.append(pl.BlockSpec((spt, 512, seq), lambda i: (i, 0, 0)))
            elif p == 5:
                out_shape.append(jax.ShapeDtypeStruct((t, H_DIFF, 2 * HEAD_DIM), F32))
                out_specs.append(pl.BlockSpec((tm, H_DIFF, 2 * HEAD_DIM), lambda i: (i, 0, 0)))
                out_shape.append(jax.ShapeDtypeStruct((t, 512), BF16))
                out_specs.append(tok(512))
            else:
                out_shape.append(jax.ShapeDtypeStruct((t, 512), BF16))
                out_specs.append(tok(512))
    c_in, c_args, c_out, c_shape = _cast_specs(casts, nt)
    return pl.pallas_call(
        functools.partial(_ka_kernel, mode=mode, ck=ck, seq=seq, n_cast=len(casts)),
        grid=(nt,),
        in_specs=in_specs + c_in,
        out_specs=out_specs + c_out,
        out_shape=out_shape + c_shape,
        compiler_params=pltpu.CompilerParams(
            dimension_semantics=("arbitrary",), vmem_limit_bytes=VMEM_LIMIT),
        name="ffn1_proj_" + mode,
    )(*args, *c_args)


HALO = 8


def _even_mix(ext_ref, base, n, pos, seq_len, pw_ref, ps_ref, cw_ref):
    def rows(j, lo, hi):
        return ext_ref[base + HALO + j:base + HALO + j + n, lo:hi]

    ya = []
    for g, win in enumerate(POOL_WINDOWS):
        half = win // 2
        lo_c, hi_c = g * POOL_G, (g + 1) * POOL_G
        s = rows(-half, lo_c, hi_c)
        for j in range(-half + 1, half):
            s = s + rows(j, lo_c, hi_c)
        lo = jnp.clip(pos - half, 0, seq_len - 1)
        hi = jnp.clip(pos + half - 1, 0, seq_len - 1)
        cnt = (hi - lo + 1).astype(F32)
        d = (s / cnt - rows(0, lo_c, hi_c)).astype(BF16)
        ya.append(_dot(d, pw_ref[g]))
    ya = jnp.concatenate(ya, axis=1) * ps_ref[...]

    def z(j):
        return rows(j, W_POOL + 2 * W_CONV, W_POOL + 3 * W_CONV) * rows(j, W_POOL, W_POOL + W_CONV)

    y = z(-1) * cw_ref[0:1, :] + z(0) * cw_ref[1:2, :] + z(1) * cw_ref[2:3, :]
    yb = rows(0, W_POOL + W_CONV, W_POOL + 2 * W_CONV) * y
    return jnp.concatenate([ya, yb], axis=1)


def _kb_kernel(*refs, mode, ck, tm, seq_len, final, n_cast):
    x_ref, mods_ref = refs[0], refs[1]
    k = 2
    if mode == "even":
        u_ref, up_ref, un_ref, pw_ref, ps_ref, cw_ref = refs[k:k + 6]
        k += 6
    else:
        n_parts = 2 if mode == "odd2" else 1
        o_refs = refs[k:k + n_parts]
        k += n_parts
    wmo_ref, g2_ref, w_in_ref, w_out_ref = refs[k:k + 4]
    k += 4
    if final:
        gf_ref = refs[k]
        k += 1
    out_ref = refs[k + n_cast]
    _side_casts(refs[k:k + n_cast], refs[k + n_cast + 1:k + 2 * n_cast + 1])
    k += 2 * n_cast + 1
    mods = mods_ref[0]
    if mode == "even":
        ext_ref = refs[k]
        n_seg = min(seq_len, tm)
        stride = n_seg + 2 * HALO
        row0 = pl.program_id(0) * tm
        feats = []
        for si in range(tm // n_seg):
            base = si * stride
            if n_seg == seq_len:
                before = after = jnp.zeros((HALO, ext_ref.shape[1]), F32)
            else:
                first = jnp.bitwise_and(row0, seq_len - 1) == 0
                last = jnp.bitwise_and(row0 + tm, seq_len - 1) == 0
                before = jnp.where(first, 0.0, up_ref[...])
                after = jnp.where(last, 0.0, un_ref[...])
            ext_ref[base:base + HALO, :] = before
            ext_ref[base + HALO:base + HALO + n_seg, :] = u_ref[si * n_seg:(si + 1) * n_seg, :]
            ext_ref[base + HALO + n_seg:base + stride, :] = after
            row = lax.broadcasted_iota(jnp.int32, (n_seg, LANES), 0) + (row0 + si * n_seg)
            pos = jnp.bitwise_and(row, seq_len - 1)
            feats.append(_even_mix(ext_ref, base, n_seg, pos, seq_len, pw_ref, ps_ref, cw_ref))
        feat = jnp.concatenate(feats, axis=0).astype(BF16)
        y = _dot(feat, wmo_ref[...])
    else:
        y = None
        off = 0
        for o_ref in o_refs:
            w = o_ref.shape[1]
            t = _dot(o_ref[...], wmo_ref[off:off + w, :])
            y = t if y is None else y + t
            off += w
    x2 = x_ref[...] + mods[5:6] * y
    x3 = _ffn_half_step(x2, mods, 6, g2_ref[...], w_in_ref, w_out_ref, ck)
    if final:
        x3 = _rms(x3) * gf_ref[...]
    out_ref[...] = x3


def _kernel_b(x, mods_l, row_fn, mix_in, mix_params, wmo, g2, w_in, w_out, mode, tm, seq_len=None,
              final_g=None, casts=(), ck=256):
    t = x.shape[0]
    nt = t // tm
    tok = lambda w: pl.BlockSpec((tm, w), lambda i: (i, 0))
    in_specs = [tok(D_MODEL), pl.BlockSpec((1, N_MOD, D_MODEL), lambda i: (row_fn(i), 0, 0))]
    args = [x, mods_l]
    scratch = []
    if mode == "even":
        u = mix_in[0]
        wu = u.shape[1]
        hb = tm // HALO
        last = t // HALO - 1
        in_specs += [
            tok(wu),
            pl.BlockSpec((HALO, wu), lambda i: (jnp.maximum(i * hb - 1, 0), 0)),
            pl.BlockSpec((HALO, wu), lambda i: (jnp.minimum((i + 1) * hb, last), 0)),
        ]
        args += [u, u, u]
        for w in mix_params:
            in_specs.append(_layer_spec(w))
            args.append(w[0])
        n_seg = min(seq_len, tm)
        assert tm % n_seg == 0 and seq_len % n_seg == 0 and seq_len & (seq_len - 1) == 0
        scratch.append(pltpu.VMEM(((tm // n_seg) * (n_seg + 2 * HALO), wu), F32))
    else:
        for o in mix_in:
            in_specs.append(tok(o.shape[1]))
            args.append(o)
    in_specs += [_layer_spec(w) for w in (wmo, g2, w_in, w_out)]
    args += [w[0] for w in (wmo, g2, w_in, w_out)]
    final = final_g is not None
    if final:
        in_specs.append(_const_spec((1, D_MODEL)))
        args.append(final_g.reshape(1, D_MODEL))
    kmode = mode if mode == "even" else ("odd2" if len(mix_in) == 2 else "odd1")
    c_in, c_args, c_out, c_shape = _cast_specs(casts, nt)
    return pl.pallas_call(
        functools.partial(_kb_kernel, mode=kmode, ck=ck, tm=tm, seq_len=seq_len, final=final, n_cast=len(casts)),
        grid=(nt,),
        in_specs=in_specs + c_in,
        out_specs=[tok(D_MODEL)] + c_out,
        out_shape=[jax.ShapeDtypeStruct((t, D_MODEL), F32)] + c_shape,
        scratch_shapes=scratch,
        compiler_params=pltpu.CompilerParams(
            dimension_semantics=("arbitrary",), vmem_limit_bytes=VMEM_LIMIT),
        name="mix_ffn2_" + kmode,
    )(*args, *c_args)


def _diff_lambda(lam_ref, lam_init):
    lp = lam_ref[...]
    s1 = jnp.sum(lp[0:1] * lp[1:2], axis=-1, keepdims=True)
    s2 = jnp.sum(lp[2:3] * lp[3:4], axis=-1, keepdims=True)
    return jnp.exp(s1) - jnp.exp(s2) + lam_init


def _half_masks():
    lane = lax.broadcasted_iota(jnp.int32, (1, LANES), 1)
    lo = lane < HEAD_DIM
    return lo, jnp.logical_not(lo)


def _diff_out_norm(o, dn_ref, lam_init):
    return (_rms(o) * dn_ref[...]) * (1.0 - lam_init)


def _ctx_attn_kernel(nq_ref, nkt_ref, nvt_ref, dq_ref, dkt_ref, dv_ref, lam_ref, dn_ref, o_ref, *, lam_init, seq):
    lo, hi = _half_masks()
    lam = _diff_lambda(lam_ref, lam_init)
    ones_t = jnp.ones((LANES, seq), BF16)
    ones_c = jnp.ones((seq, LANES), BF16)
    for b in range(nkt_ref.shape[0]):
        rows = slice(b * seq, (b + 1) * seq)
        for hp in range(H_NAT // 2):
            sl = slice(hp * LANES, (hp + 1) * LANES)
            q = nq_ref[rows, sl]
            kt = nkt_ref[b, sl, :].astype(BF16)
            vt = jnp.concatenate([nvt_ref[b, sl, :].astype(BF16), ones_t], axis=0)
            outs = []
            for msk in (lo, hi):
                qm = jnp.where(msk, q, jnp.zeros_like(q))
                s = _dot(qm, kt)
                e = jnp.exp2(s - jnp.max(s, axis=-1, keepdims=True)).astype(BF16)
                ol = _dot_nt(e, vt)
                outs.append(ol[:, :LANES] * (1.0 / ol[:, LANES:]))
            o_ref[rows, sl] = jnp.where(lo, outs[0], outs[1]).astype(o_ref.dtype)
        for h in range(H_DIFF):
            sl = slice(h * LANES, (h + 1) * LANES)
            q = dq_ref[rows, sl]
            kt = dkt_ref[b, sl, :].astype(BF16)
            v1 = jnp.concatenate([dv_ref[rows, sl], ones_c], axis=1)
            ols = []
            for msk in (lo, hi):
                qm = jnp.where(msk, q, jnp.zeros_like(q))
                s = _dot(qm, kt)
                e = jnp.exp2(s - jnp.max(s, axis=-1, keepdims=True)).astype(BF16)
                ol = _dot(e, v1)
                ols.append(ol[:, :LANES] * (1.0 / ol[:, LANES:]))
            o = _diff_out_norm(ols[0] - lam * ols[1], dn_ref, lam_init)
            o_ref[rows, W_NAT + h * LANES:W_NAT + (h + 1) * LANES] = o.astype(o_ref.dtype)


def _ctx_attention(parts, lam_p, dnorm, seq, lam_init, nb=8):
    nq, nkt, nvt, dq, dkt, dv = parts
    t = nq.shape[0]
    steps = t // (seq * nb)
    tok = pl.BlockSpec((nb * seq, 512), lambda i: (i, 0))
    fmaj = pl.BlockSpec((nb, 512, seq), lambda i: (i, 0, 0))
    return pl.pallas_call(
        functools.partial(_ctx_attn_kernel, lam_init=lam_init, seq=seq),
        grid=(steps,),
        in_specs=[tok, fmaj, fmaj, tok, fmaj, tok, _const_spec((4, HEAD_DIM)), _const_spec((1, 2 * HEAD_DIM))],
        out_specs=pl.BlockSpec((nb * seq, D_MODEL), lambda i: (i, 0)),
        out_shape=jax.ShapeDtypeStruct((t, D_MODEL), BF16),
        compiler_params=pltpu.CompilerParams(
            dimension_semantics=("arbitrary",), vmem_limit_bytes=VMEM_LIMIT),
        name="ctx_attention",
    )(nq, nkt, nvt, dq, dkt, dv, lam_p, dnorm.reshape(1, 2 * HEAD_DIM))


NAT_QROWS = 4
NAT_KROWS = 12
NAT_NDR = 2 * NAT_WIN_R


def _build_bias_tiles(rpb_ref, u_ref):
    c = lax.broadcasted_iota(jnp.int32, (GRID_W, LANES), 0)
    lane = lax.broadcasted_iota(jnp.int32, (GRID_W, LANES), 1)
    kc = jnp.bitwise_and(lane, GRID_W - 1)
    c0 = jnp.clip(c - NAT_WIN_C // 2, 0, GRID_W - NAT_WIN_C)
    col_ok = jnp.logical_and(kc >= c0, kc < c0 + NAT_WIN_C)
    left = lane < GRID_W
    n_dr = 2 * NAT_WIN_R - 1
    for h in range(H_NAT):
        tl, tr = [], []
        for dr in range(n_dr):
            row = jnp.broadcast_to(rpb_ref[h, dr:dr + 1, :] * LOG2E, (GRID_W, LANES))
            tl.append(pltpu.roll(row, LANES - (NAT_WIN_C - 1), 1, stride=1, stride_axis=0))
            tr.append(pltpu.roll(row, LANES - (NAT_WIN_C - 1) + GRID_W, 1, stride=1, stride_axis=0))
        for di in range(NAT_NDR):
            dl = di - NAT_WIN_R + (NAT_WIN_R - 1)
            drr = dl + 1
            lv = tl[dl] if 0 <= dl < n_dr else jnp.full((GRID_W, LANES), NEG, F32)
            rv = tr[drr] if 0 <= drr < n_dr else jnp.full((GRID_W, LANES), NEG, F32)
            u_ref[h, di] = jnp.where(col_ok, jnp.where(left, lv, rv), NEG)


def _nat_lat_kernel(q_ref, k_ref, v_ref, kc_ref, vc_ref, rpb_ref, o_ref, u_ref, s_ref):
    b = pl.program_id(1)

    @pl.when(jnp.logical_and(pl.program_id(0) == 0, b == 0))
    def _():
        _build_bias_tiles(rpb_ref, u_ref)

    rows = k_ref.shape[0] // GRID_W
    kb = jnp.clip(b * NAT_QROWS - NAT_WIN_R // 2, 0, rows - NAT_KROWS)
    koff = pl.multiple_of(kb * GRID_W, GRID_W)
    lo, hi = _half_masks()
    left = lax.broadcasted_iota(jnp.int32, (GRID_W, LANES), 1) < GRID_W
    nq = NAT_QROWS * GRID_W
    for hp in range(H_NAT // 2):
        sl = slice(hp * LANES, (hp + 1) * LANES)
        q = q_ref[:, sl]
        k = k_ref[pl.ds(koff, NAT_KROWS * GRID_W), sl]
        v = v_ref[pl.ds(koff, NAT_KROWS * GRID_W), sl]
        kc = kc_ref[:, sl]
        vc = vc_ref[:, sl]
        v1 = jnp.concatenate([v, jnp.ones_like(v)], axis=1)
        vc1 = jnp.concatenate([vc, jnp.ones_like(vc)], axis=1)
        outs = []
        for e, msk in enumerate((lo, hi)):
            h = 2 * hp + e
            qm = jnp.where(msk, q, jnp.zeros_like(q))
            s_ref[...] = _dot_nt(qm, k)
            for i in range(NAT_QROWS):
                r = b * NAT_QROWS + i
                r0 = jnp.clip(r - NAT_WIN_R // 2, 0, rows - NAT_WIN_R)
                for jp in range(NAT_KROWS // 2):
                    kl = kb + 2 * jp
                    ok_l = jnp.logical_and(kl >= r0, kl < r0 + NAT_WIN_R)
                    ok_r = jnp.logical_and(kl + 1 >= r0, kl + 1 < r0 + NAT_WIN_R)
                    di = jnp.clip(kl - r + NAT_WIN_R, 0, NAT_NDR - 1)
                    ok = jnp.where(left, ok_l.astype(jnp.int32), ok_r.astype(jnp.int32))
                    bias = jnp.where(ok > 0, u_ref[h, di], NEG)
                    tile = (slice(i * GRID_W, (i + 1) * GRID_W), slice(jp * LANES, (jp + 1) * LANES))
                    s_ref[tile] = s_ref[tile] + bias
            s_ctx = _dot_nt(qm, kc)
            s_loc = s_ref[...]
            m = jnp.maximum(jnp.max(s_loc, axis=-1, keepdims=True), jnp.max(s_ctx, axis=-1, keepdims=True))
            ol = (_dot(jnp.exp2(s_loc - m).astype(BF16), v1) + _dot(jnp.exp2(s_ctx - m).astype(BF16), vc1))
            outs.append(ol[:, :LANES] * (1.0 / ol[:, LANES:]))
        o_ref[:, sl] = jnp.where(lo, outs[0], outs[1]).astype(o_ref.dtype)


def _nat_lat_attention(q, k, v, kc, vc, rpb_pad, batch, seq):
    nq = NAT_QROWS * GRID_W
    nblk = seq // nq
    p = kc.shape[1]
    return pl.pallas_call(
        _nat_lat_kernel,
        grid=(batch, nblk),
        in_specs=[
            pl.BlockSpec((nq, W_NAT), lambda bb, i: (bb * nblk + i, 0)),
            pl.BlockSpec((seq, W_NAT), lambda bb, i: (bb, 0)),
            pl.BlockSpec((seq, W_NAT), lambda bb, i: (bb, 0)),
            pl.BlockSpec((None, p, W_NAT), lambda bb, i: (bb, 0, 0)),
            pl.BlockSpec((None, p, W_NAT), lambda bb, i: (bb, 0, 0)),
            pl.BlockSpec(rpb_pad.shape, lambda bb, i: (0, 0, 0)),
        ],
        out_specs=pl.BlockSpec((nq, W_NAT), lambda bb, i: (bb * nblk + i, 0)),
        out_shape=jax.ShapeDtypeStruct((batch * seq, W_NAT), BF16),
        scratch_shapes=[
            pltpu.VMEM((H_NAT, NAT_NDR, GRID_W, LANES), F32),
            pltpu.VMEM((nq, NAT_KROWS * GRID_W), F32),
        ],
        compiler_params=pltpu.CompilerParams(
            dimension_semantics=("arbitrary", "arbitrary"), vmem_limit_bytes=VMEM_LIMIT),
        name="nat_latent_attention",
    )(q, k, v, kc, vc, rpb_pad)


DIFF_KCHUNK = 256


def _lane_fold(x, op):
    out = x[:, :LANES]
    for c in range(1, x.shape[1] // LANES):
        out = op(out, x[:, c * LANES:(c + 1) * LANES])
    return out


def _round_robin(*gens):
    gens = list(gens)
    while gens:
        for g in list(gens):
            try:
                next(g)
            except StopIteration:
                gens.remove(g)


def _diff_lat_kernel(q_ref, k_ref, v_ref, kc_ref, vc_ref, lam_ref, dn_ref, o_ref, sa_ref, sb_ref, *, lam_init):
    lo, hi = _half_masks()
    lam = _diff_lambda(lam_ref, lam_init)
    ck = DIFF_KCHUNK
    pieces = ([(k_ref, v_ref, c * ck) for c in range(k_ref.shape[0] // ck)]
              + [(kc_ref, vc_ref, c * ck) for c in range(kc_ref.shape[0] // ck)])
    row_max = {}

    def s_of(h):
        return (sa_ref, sb_ref)[h % 2]

    def scores(h):
        sl = slice(h * LANES, (h + 1) * LANES)
        q = q_ref[:, sl]
        ms = []
        for j, msk in enumerate((lo, hi)):
            qm = jnp.where(msk, q, jnp.zeros_like(q))
            mx = None
            for ci, (kr, _, off) in enumerate(pieces):
                s = _dot_nt(qm, kr[off:off + ck, sl])
                s_of(h)[j, :, ci * ck:(ci + 1) * ck] = s
                part = _lane_fold(s, jnp.maximum)
                mx = part if mx is None else jnp.maximum(mx, part)
                yield
            ms.append(jnp.max(mx, axis=-1, keepdims=True))
        row_max[h] = ms

    def weighted_values(h):
        sl = slice(h * LANES, (h + 1) * LANES)
        tq = q_ref.shape[0]
        acc = None
        for ci, (_, vr, off) in enumerate(pieces):
            cols = slice(ci * ck, (ci + 1) * ck)
            e = jnp.concatenate([jnp.exp2(s_of(h)[j, :, cols] - row_max[h][j]).astype(BF16) for j in range(2)], axis=0)
            vv = vr[off:off + ck, sl]
            t = _dot(e, jnp.concatenate([vv, jnp.ones_like(vv)], axis=1))
            acc = t if acc is None else acc + t
            yield
        on = acc[:, :LANES] * (1.0 / acc[:, LANES:])
        o = on[:tq] - lam * on[tq:]
        o_ref[:, sl] = _diff_out_norm(o, dn_ref, lam_init).astype(o_ref.dtype)

    _round_robin(scores(0))
    for h in range(H_DIFF):
        nxt = [scores(h + 1)] if h + 1 < H_DIFF else []
        _round_robin(weighted_values(h), *nxt)


def _diff_lat_attention(q, k, v, kc, vc, lam_p, dnorm, batch, seq, lam_init, tq=256):
    nqb = seq // tq
    p = kc.shape[1]
    s_scratch = pltpu.VMEM((2, tq, seq + p), F32)
    return pl.pallas_call(
        functools.partial(_diff_lat_kernel, lam_init=lam_init),
        grid=(batch, nqb),
        in_specs=[
            pl.BlockSpec((tq, W_DIFF), lambda bb, i: (bb * nqb + i, 0)),
            pl.BlockSpec((seq, W_DIFF), lambda bb, i: (bb, 0)),
            pl.BlockSpec((seq, W_DIFF), lambda bb, i: (bb, 0)),
            pl.BlockSpec((None, p, W_DIFF), lambda bb, i: (bb, 0, 0)),
            pl.BlockSpec((None, p, W_DIFF), lambda bb, i: (bb, 0, 0)),
            pl.BlockSpec((4, HEAD_DIM), lambda bb, i: (0, 0)),
            pl.BlockSpec((1, 2 * HEAD_DIM), lambda bb, i: (0, 0)),
        ],
        out_specs=pl.BlockSpec((tq, W_DIFF), lambda bb, i: (bb * nqb + i, 0)),
        out_shape=jax.ShapeDtypeStruct((batch * seq, W_DIFF), BF16),
        scratch_shapes=[s_scratch, s_scratch],
        compiler_params=pltpu.CompilerParams(
            dimension_semantics=("arbitrary", "arbitrary"), vmem_limit_bytes=VMEM_LIMIT),
        name="diff_latent_attention",
    )(q, k, v, kc, vc, lam_p, dnorm.reshape(1, 2 * HEAD_DIM))


def _lat_attn_kernel(nq_ref, nk_ref, nv_ref, nkc_ref, nvc_ref, rpb_ref, dq_ref, dk_ref, dv_ref, dkc_ref, dvc_ref,
                     lam_ref, dn_ref, on_ref, od_ref, u_ref, s_ref, sa_ref, sb_ref, *, lam_init):
    _nat_lat_kernel(nq_ref, nk_ref, nv_ref, nkc_ref, nvc_ref, rpb_ref, on_ref, u_ref, s_ref)
    _diff_lat_kernel(dq_ref, dk_ref, dv_ref, dkc_ref, dvc_ref, lam_ref, dn_ref, od_ref, sa_ref, sb_ref,
                     lam_init=lam_init)


def _lat_attention(nat_in, rpb_pad, diff_in, lam_p, dnorm, batch, seq, lam_init):
    nq = NAT_QROWS * GRID_W
    nblk = seq // nq
    p = nat_in[3].shape[1]
    qblk = pl.BlockSpec((nq, 512), lambda bb, i: (bb * nblk + i, 0))
    full = pl.BlockSpec((seq, 512), lambda bb, i: (bb, 0), pipeline_mode=pl.Buffered(1))
    ctxb = pl.BlockSpec((None, p, 512), lambda bb, i: (bb, 0, 0))
    return pl.pallas_call(
        functools.partial(_lat_attn_kernel, lam_init=lam_init),
        grid=(batch, nblk),
        in_specs=[qblk, full, full, ctxb, ctxb, pl.BlockSpec(rpb_pad.shape, lambda bb, i: (0, 0, 0)),
                  qblk, full, full, ctxb, ctxb,
                  pl.BlockSpec((4, HEAD_DIM), lambda bb, i: (0, 0)),
                  pl.BlockSpec((1, 2 * HEAD_DIM), lambda bb, i: (0, 0))],
        out_specs=[qblk, qblk],
        out_shape=[jax.ShapeDtypeStruct((batch * seq, 512), BF16)] * 2,
        scratch_shapes=[
            pltpu.VMEM((H_NAT, NAT_NDR, GRID_W, LANES), F32),
            pltpu.VMEM((nq, NAT_KROWS * GRID_W), F32),
            pltpu.VMEM((2, nq, seq + p), F32),
            pltpu.VMEM((2, nq, seq + p), F32),
        ],
        compiler_params=pltpu.CompilerParams(
            dimension_semantics=("arbitrary", "arbitrary"), vmem_limit_bytes=VMEM_LIMIT),
        name="latent_attention",
    )(*nat_in, rpb_pad, *diff_in, lam_p, dnorm.reshape(1, 2 * HEAD_DIM))


def _rope_tables(n):
    t = np.arange(n)
    row = (t // GRID_W).astype(np.float64)
    col = (t % GRID_W).astype(np.float64)
    quarter = HEAD_DIM // 4
    inv = 1.0 / (ROPE_THETA ** (np.arange(quarter) / quarter))
    cr, sr = np.cos(row[:, None] * inv[None]), np.sin(row[:, None] * inv[None])
    cc, sc = np.cos(col[:, None] * inv[None]), np.sin(col[:, None] * inv[None])
    zero = np.zeros_like(sr)
    cos_t = np.concatenate([cr, cr, cc, cc], axis=1)
    sin_a = np.concatenate([-sr, zero, -sc, zero], axis=1)
    sin_b = np.concatenate([zero, sr, zero, sc], axis=1)
    tile = lambda a: jnp.asarray(np.concatenate([a, a], axis=1).astype(np.float32))
    return tile(cos_t), tile(sin_a), tile(sin_b)


def kernel(x_prompt, x_sample, cache_nat_k, cache_nat_v, cache_diff_k, cache_diff_v, c, c_ctx, mod_w, mod_b, norm_ffn1, ffn1_w_in, ffn1_w_out, norm_mix, even_w_in, pool_w, pool_scale, conv_w, odd_w_in, nat_rpb, diff_lambda, diff_norm, mix_w_out, norm_ffn2, ffn2_w_in, ffn2_w_out, final_norm):
    batch, seq, _ = x_prompt.shape
    dbatch, dseq, _ = x_sample.shape
    past = cache_nat_k.shape[2]
    tm = 512

    cond = jnp.zeros((SUBLANES, D_MODEL), F32).at[0].set(c_ctx).at[1:1 + dbatch].set(c)
    mods = _adaln_mods(cond, mod_w, mod_b).reshape(DEPTH, SUBLANES, N_MOD, D_MODEL)

    ctx_row = lambda i: 0
    lat_row = lambda i: 1 + (i * tm) // dseq

    cache_bf = lambda a, w: a[:, 0].reshape(dbatch, past, w).astype(BF16)
    gain = lambda g: g.reshape(g.shape[0], 1, g.shape[1])
    g_ffn1, g_mix, g_ffn2 = gain(norm_ffn1), gain(norm_mix), gain(norm_ffn2)
    pool3 = pool_w.reshape(pool_w.shape[0], len(POOL_WINDOWS) * POOL_G, POOL_G)

    f1_in0, f1_out0, w_even0 = _cast_call([(ffn1_w_in, 0), (ffn1_w_out, 0), (even_w_in, 0)])

    ctx = x_prompt.reshape(batch * seq, D_MODEL)
    lat = x_sample.reshape(dbatch * dseq, D_MODEL)

    l = 0
    x1c, uc, f2_in0, f2_out0, wmo0, pool0 = _kernel_a(
        ctx, mods[l], ctx_row, (g_ffn1, l), (f1_in0, 0), (f1_out0, 0), (g_mix, l), (w_even0, 0), "even", tm,
        casts=[(ffn2_w_in, 0), (ffn2_w_out, 0), (mix_w_out, 0), (pool3, 0)])
    even_params = ((pool0.reshape(1, len(POOL_WINDOWS), POOL_G, POOL_G), 0), (gain(pool_scale), 0), (conv_w, 0))
    ctx, f1_in1, f1_out1 = _kernel_b(
        x1c, mods[l], ctx_row, [uc], even_params, (wmo0, 0), (g_ffn2, l), (f2_in0, 0), (f2_out0, 0), "even", tm,
        seq_len=seq, casts=[(ffn1_w_in, 1), (ffn1_w_out, 1)])
    x1l, ul, w_odd1, f2_in1 = _kernel_a(
        lat, mods[l], lat_row, (g_ffn1, l), (f1_in0, 0), (f1_out0, 0), (g_mix, l), (w_even0, 0), "even", tm,
        casts=[(odd_w_in, 0), (ffn2_w_in, 1)])
    lat, f2_out1, wmo1 = _kernel_b(
        x1l, mods[l], lat_row, [ul], even_params, (wmo0, 0), (g_ffn2, l), (f2_in0, 0), (f2_out0, 0), "even", tm,
        seq_len=dseq, casts=[(ffn2_w_out, 1), (mix_w_out, 1)])

    l = 1
    lam_init = 0.8 - 0.6 * math.exp(-0.3 * l)
    ffn1 = ((g_ffn1, l), (f1_in1, 0), (f1_out1, 0), (g_mix, l), (w_odd1, 0))
    ffn2 = ((wmo1, 0), (g_ffn2, l), (f2_in1, 0), (f2_out1, 0))
    ctx1, nq, nkt, nvt, dq, dkt, dv, dvb = _kernel_a(ctx, mods[l], ctx_row, *ffn1, "ctx", tm, seq=seq)
    o_ctx = _ctx_attention([nq, nkt, nvt, dq, dkt, dvb], diff_lambda[0], diff_norm[0], seq, lam_init)
    y_prompt, = _kernel_b(ctx1, mods[l], ctx_row, [o_ctx], None, *ffn2, "odd", tm, final_g=final_norm)

    lat1, lq, lk, lv, ldq, ldk, ldv = _kernel_a(lat, mods[l], lat_row, *ffn1, "lat", tm,
                                                 rope_tabs=_rope_tables(dseq))
    rpb_pad = jnp.zeros((H_NAT, 2 * NAT_WIN_R, LANES), F32).at[:, :2 * NAT_WIN_R - 1,
                                                                :2 * NAT_WIN_C - 1].set(nat_rpb[0])
    o_nat, o_diff = _lat_attention(
        [lq, lk, lv, cache_bf(cache_nat_k, W_NAT), cache_bf(cache_nat_v, W_NAT)], rpb_pad,
        [ldq, ldk, ldv, cache_bf(cache_diff_k, W_DIFF), cache_bf(cache_diff_v, W_DIFF)],
        diff_lambda[0], diff_norm[0], dbatch, dseq, lam_init)
    y_sample, = _kernel_b(lat1, mods[l], lat_row, [o_nat, o_diff], None, *ffn2, "odd", tm, final_g=final_norm)

    new_nat_k = nkt.reshape(batch, 1, H_NAT, HEAD_DIM, seq).transpose(0, 1, 4, 2, 3)
    new_nat_v = nvt.reshape(batch, 1, H_NAT, HEAD_DIM, seq).transpose(0, 1, 4, 2, 3)
    new_diff_k = dkt.reshape(batch, 1, H_DIFF, 2, HEAD_DIM, seq).transpose(0, 1, 5, 2, 3, 4)
    new_diff_v = dv.reshape(batch, 1, seq, H_DIFF, 2 * HEAD_DIM)
    return (y_prompt.reshape(batch, seq, D_MODEL), y_sample.reshape(dbatch, dseq, D_MODEL),
            new_nat_k, new_nat_v, new_diff_k, new_diff_v)
```

```python
import functools
import math

import numpy as np
import jax
import jax.numpy as jnp
from jax import lax
from jax.experimental import pallas as pl
from jax.experimental.pallas import tpu as pltpu

D_MODEL = 1024
D_FF = 2816
N_MOD = 9
DEPTH = 2
GRID_W = 64
HEAD_DIM = 64
W_POOL = 512
W_CONV = 512
POOL_WINDOWS = (2, 4, 8, 16)
POOL_G = 128
W_NAT = 512
W_DIFF = 512
H_NAT = 8
H_DIFF = 4
NAT_WIN_R = 8
NAT_WIN_C = 16
ROPE_THETA = 10000.0
ATTN_SCALE = HEAD_DIM ** -0.5
LOG2E = math.log2(math.e)
Q_SCALE = ATTN_SCALE * LOG2E
EPS = 1e-6
NEG = -1e30

LANES = 128
SUBLANES = 8
VMEM_LIMIT = 60 * 1024 * 1024

BF16 = jnp.bfloat16
F32 = jnp.float32


def _dot(a, b):
    return jnp.dot(a, b, preferred_element_type=F32)


def _dot_nt(a, b):
    return lax.dot_general(a, b, (((1,), (1,)), ((), ())), preferred_element_type=F32)


def _sigmoid(x):
    return 1.0 / (1.0 + jnp.exp(-x))


def _rms(x):
    return x * lax.rsqrt(jnp.mean(x * x, axis=-1, keepdims=True) + EPS)


def _modnorm(x, shift, scale, g):
    return _rms(x) * (g * (1.0 + scale)) + shift


def _ffn_half_step(x, mods, base, g, w_in_ref, w_out_ref, ck):
    shift, scale, gate = mods[base:base + 1], mods[base + 1:base + 2], mods[base + 2:base + 3]
    h = _modnorm(x, shift, scale, g).astype(BF16)
    acc = None
    for c in range(D_FF // ck):
        a = _dot(h, w_in_ref[:, c * ck:(c + 1) * ck])
        b = _dot(h, w_in_ref[:, D_FF + c * ck:D_FF + (c + 1) * ck])
        gated = ((a * _sigmoid(a)) * b).astype(BF16)
        t = _dot(gated, w_out_ref[c * ck:(c + 1) * ck, :])
        acc = t if acc is None else acc + t
    return x + (0.5 * gate) * acc


def _const_spec(shape):
    nd = len(shape)
    return pl.BlockSpec(shape, lambda i, _nd=nd: (0,) * _nd, pipeline_mode=pl.Buffered(1))


def _mods_kernel(cond_ref, w_ref, b_ref, o_ref):
    cnd = cond_ref[...]
    s = (cnd * _sigmoid(cnd)).astype(BF16)
    o_ref[0] = _dot(s, w_ref[0].astype(BF16)) + b_ref[0]


def _adaln_mods(cond, mod_w, mod_b):
    tn = 2304
    nt = (N_MOD * D_MODEL) // tn
    return pl.pallas_call(
        _mods_kernel,
        grid=(DEPTH, nt),
        in_specs=[
            pl.BlockSpec((SUBLANES, D_MODEL), lambda l, j: (0, 0)),
            pl.BlockSpec((1, D_MODEL, tn), lambda l, j: (l, 0, j)),
            pl.BlockSpec((1, 1, tn), lambda l, j: (l, 0, j)),
        ],
        out_specs=pl.BlockSpec((1, SUBLANES, tn), lambda l, j: (l, 0, j)),
        out_shape=jax.ShapeDtypeStruct((DEPTH, SUBLANES, N_MOD * D_MODEL), F32),
        compiler_params=pltpu.CompilerParams(vmem_limit_bytes=VMEM_LIMIT),
        name="adaln_mods",
    )(cond, mod_w, mod_b.reshape(DEPTH, 1, N_MOD * D_MODEL))


def _rope(x, cos_t, sin_a, sin_b):
    w = x.shape[1]
    reps = w // LANES
    c = jnp.concatenate([cos_t] * reps, axis=1)
    sa = jnp.concatenate([sin_a] * reps, axis=1)
    sb = jnp.concatenate([sin_b] * reps, axis=1)
    quarter = HEAD_DIM // 4
    up = pltpu.roll(x, w - quarter, 1)
    dn = pltpu.roll(x, quarter, 1)
    return x * c + up * sa + dn * sb


def _side_casts(cast_ins, cast_outs):
    for ci, co in zip(cast_ins, cast_outs):
        co[0] = ci[...].astype(BF16)


def _ka_kernel(*refs, mode, ck, seq, n_cast):
    n_in = {"even": 7, "ctx": 7, "lat": 10}[mode]
    x_ref, mods_ref, g1_ref, w_in_ref, w_out_ref, gm_ref, wp_ref = refs[:7]
    outs = refs[n_in + n_cast:]
    _side_casts(refs[n_in:n_in + n_cast], outs[len(outs) - n_cast:])
    mods = mods_ref[0]
    x1 = _ffn_half_step(x_ref[...], mods, 0, g1_ref[...], w_in_ref, w_out_ref, ck)
    outs[0][...] = x1
    h = _modnorm(x1, mods[3:4], mods[4:5], gm_ref[...]).astype(BF16)
    if mode == "even":
        outs[1][...] = _dot(h, wp_ref[...])
        return
    if mode == "lat":
        cos_ref, sa_ref, sb_ref = refs[7:10]
    for p in range(6):
        if mode == "ctx" and p in (1, 2, 4):
            ut = lax.dot_general(wp_ref[:, p * 512:(p + 1) * 512], h, (((0,), (1,)), ((), ())),
                                 preferred_element_type=F32)
            for bb in range(ut.shape[1] // seq):
                outs[1 + p][bb] = ut[:, bb * seq:(bb + 1) * seq]
            continue
        u = _dot(h, wp_ref[:, p * 512:(p + 1) * 512])
        if mode == "lat" and p in (3, 4):
            u = _rope(u, cos_ref[...], sa_ref[...], sb_ref[...])
        if p in (0, 3):
            u = u * Q_SCALE
        if mode == "ctx" and p == 5:
            outs[1 + p][...] = u.reshape(u.shape[0], H_DIFF, 2 * HEAD_DIM)
            outs[2 + p][...] = u.astype(BF16)
            continue
        outs[1 + p][...] = u.astype(outs[1 + p].dtype)


def _layer_spec(wl):
    w, l = wl
    nd = w.ndim - 1
    return pl.BlockSpec((None,) + w.shape[1:], lambda i, _l=l, _nd=nd: (_l,) + (0,) * _nd,
                        pipeline_mode=pl.Buffered(1))


def _cast_specs(casts, nt):
    in_specs, args, out_specs, out_shape = [], [], [], []
    for w, l in casts:
        _, rows, cols = w.shape
        rb = rows // nt
        assert rows % nt == 0 and rb % (2 * SUBLANES) == 0
        in_specs.append(pl.BlockSpec((None, rb, cols), lambda i, _l=l: (_l, i, 0)))
        args.append(w)
        out_specs.append(pl.BlockSpec((1, rb, cols), lambda i: (0, i, 0)))
        out_shape.append(jax.ShapeDtypeStruct((1, rows, cols), BF16))
    return in_specs, args, out_specs, out_shape


def _cast_kernel(*refs):
    n = len(refs) // 2
    _side_casts(refs[:n], refs[n:])


def _cast_call(casts, nt=8):
    c_in, c_args, c_out, c_shape = _cast_specs(casts, nt)
    return pl.pallas_call(
        _cast_kernel, grid=(nt,), in_specs=c_in, out_specs=c_out, out_shape=c_shape,
        compiler_params=pltpu.CompilerParams(dimension_semantics=("arbitrary",), vmem_limit_bytes=VMEM_LIMIT),
        name="cast_weights",
    )(*c_args)


def _kernel_a(x, mods_l, row_fn, g1, w_in, w_out, gm, wp, mode, tm, rope_tabs=None, seq=None, casts=(), ck=256):
    t = x.shape[0]
    nt = t // tm
    tok = lambda w: pl.BlockSpec((tm, w), lambda i: (i, 0))
    weights = [g1, w_in, w_out, gm, wp]
    in_specs = [tok(D_MODEL), pl.BlockSpec((1, N_MOD, D_MODEL), lambda i: (row_fn(i), 0, 0))]
    in_specs += [_layer_spec(w) for w in weights]
    args = [x, mods_l] + [w[0] for w in weights]
    n_proj = wp[0].shape[2]
    out_shape = [jax.ShapeDtypeStruct((t, D_MODEL), F32)]
    out_specs = [tok(D_MODEL)]
    if mode == "even":
        out_shape.append(jax.ShapeDtypeStruct((t, n_proj), F32))
        out_specs.append(tok(n_proj))
    elif mode == "lat":
        seq_tiles = rope_tabs[0].shape[0] // tm
        for tab in rope_tabs:
            in_specs.append(pl.BlockSpec((tm, LANES), lambda i, _s=seq_tiles: (i % _s, 0)))
            args.append(tab)
        for _ in range(6):
            out_shape.append(jax.ShapeDtypeStruct((t, 512), BF16))
            out_specs.append(tok(512))
    else:
        spt = tm // seq
        for p in range(6):
            if p in (1, 2, 4):
                out_shape.append(jax.ShapeDtypeStruct((t // seq, 512, seq), F32))
                out_specs.append(pl.BlockSpec((spt, 512, seq), lambda i: (i, 0, 0)))
            elif p == 5:
                out_shape.append(jax.ShapeDtypeStruct((t, H_DIFF, 2 * HEAD_DIM), F32))
                out_specs.append(pl.BlockSpec((tm, H_DIFF, 2 * HEAD_DIM), lambda i: (i, 0, 0)))
                out_shape.append(jax.ShapeDtypeStruct((t, 512), BF16))
                out_specs.append(tok(512))
            else:
                out_shape.append(jax.ShapeDtypeStruct((t, 512), BF16))
                out_specs.append(tok(512))
    c_in, c_args, c_out, c_shape = _cast_specs(casts, nt)
    return pl.pallas_call(
        functools.partial(_ka_kernel, mode=mode, ck=ck, seq=seq, n_cast=len(casts)),
        grid=(nt,),
        in_specs=in_specs + c_in,
        out_specs=out_specs + c_out,
        out_shape=out_shape + c_shape,
        compiler_params=pltpu.CompilerParams(
            dimension_semantics=("arbitrary",), vmem_limit_bytes=VMEM_LIMIT),
        name="ffn1_proj_" + mode,
    )(*args, *c_args)


HALO = 8


def _even_mix(ext_ref, base, n, pos, seq_len, pw_ref, ps_ref, cw_ref):
    def rows(j, lo, hi):
        return ext_ref[base + HALO + j:base + HALO + j + n, lo:hi]

    ya = []
    for g, win in enumerate(POOL_WINDOWS):
        half = win // 2
        lo_c, hi_c = g * POOL_G, (g + 1) * POOL_G
        s = rows(-half, lo_c, hi_c)
        for j in range(-half + 1, half):
            s = s + rows(j, lo_c, hi_c)
        lo = jnp.clip(pos - half, 0, seq_len - 1)
        hi = jnp.clip(pos + half - 1, 0, seq_len - 1)
        cnt = (hi - lo + 1).astype(F32)
        d = (s / cnt - rows(0, lo_c, hi_c)).astype(BF16)
        ya.append(_dot(d, pw_ref[g]))
    ya = jnp.concatenate(ya, axis=1) * ps_ref[...]

    def z(j):
        return rows(j, W_POOL + 2 * W_CONV, W_POOL + 3 * W_CONV) * rows(j, W_POOL, W_POOL + W_CONV)

    y = z(-1) * cw_ref[0:1, :] + z(0) * cw_ref[1:2, :] + z(1) * cw_ref[2:3, :]
    yb = rows(0, W_POOL + W_CONV, W_POOL + 2 * W_CONV) * y
    return jnp.concatenate([ya, yb], axis=1)


def _kb_kernel(*refs, mode, ck, tm, seq_len, final, n_cast):
    x_ref, mods_ref = refs[0], refs[1]
    k = 2
    if mode == "even":
        u_ref, up_ref, un_ref, pw_ref, ps_ref, cw_ref = refs[k:k + 6]
        k += 6
    else:
        n_parts = 2 if mode == "odd2" else 1
        o_refs = refs[k:k + n_parts]
        k += n_parts
    wmo_ref, g2_ref, w_in_ref, w_out_ref = refs[k:k + 4]
    k += 4
    if final:
        gf_ref = refs[k]
        k += 1
    out_ref = refs[k + n_cast]
    _side_casts(refs[k:k + n_cast], refs[k + n_cast + 1:k + 2 * n_cast + 1])
    k += 2 * n_cast + 1
    mods = mods_ref[0]
    if mode == "even":
        ext_ref = refs[k]
        n_seg = min(seq_len, tm)
        stride = n_seg + 2 * HALO
        row0 = pl.program_id(0) * tm
        feats = []
        for si in range(tm // n_seg):
            base = si * stride
            if n_seg == seq_len:
                before = after = jnp.zeros((HALO, ext_ref.shape[1]), F32)
            else:
                first = jnp.bitwise_and(row0, seq_len - 1) == 0
                last = jnp.bitwise_and(row0 + tm, seq_len - 1) == 0
                before = jnp.where(first, 0.0, up_ref[...])
                after = jnp.where(last, 0.0, un_ref[...])
            ext_ref[base:base + HALO, :] = before
            ext_ref[base + HALO:base + HALO + n_seg, :] = u_ref[si * n_seg:(si + 1) * n_seg, :]
            ext_ref[base + HALO + n_seg:base + stride, :] = after
            row = lax.broadcasted_iota(jnp.int32, (n_seg, LANES), 0) + (row0 + si * n_seg)
            pos = jnp.bitwise_and(row, seq_len - 1)
            feats.append(_even_mix(ext_ref, base, n_seg, pos, seq_len, pw_ref, ps_ref, cw_ref))
        feat = jnp.concatenate(feats, axis=0).astype(BF16)
        y = _dot(feat, wmo_ref[...])
    else:
        y = None
        off = 0
        for o_ref in o_refs:
            w = o_ref.shape[1]
            t = _dot(o_ref[...], wmo_ref[off:off + w, :])
            y = t if y is None else y + t
            off += w
    x2 = x_ref[...] + mods[5:6] * y
    x3 = _ffn_half_step(x2, mods, 6, g2_ref[...], w_in_ref, w_out_ref, ck)
    if final:
        x3 = _rms(x3) * gf_ref[...]
    out_ref[...] = x3


def _kernel_b(x, mods_l, row_fn, mix_in, mix_params, wmo, g2, w_in, w_out, mode, tm, seq_len=None,
              final_g=None, casts=(), ck=256):
    t = x.shape[0]
    nt = t // tm
    tok = lambda w: pl.BlockSpec((tm, w), lambda i: (i, 0))
    in_specs = [tok(D_MODEL), pl.BlockSpec((1, N_MOD, D_MODEL), lambda i: (row_fn(i), 0, 0))]
    args = [x, mods_l]
    scratch = []
    if mode == "even":
        u = mix_in[0]
        wu = u.shape[1]
        hb = tm // HALO
        last = t // HALO - 1
        in_specs += [
            tok(wu),
            pl.BlockSpec((HALO, wu), lambda i: (jnp.maximum(i * hb - 1, 0), 0)),
            pl.BlockSpec((HALO, wu), lambda i: (jnp.minimum((i + 1) * hb, last), 0)),
        ]
        args += [u, u, u]
        for w in mix_params:
            in_specs.append(_layer_spec(w))
            args.append(w[0])
        n_seg = min(seq_len, tm)
        assert tm % n_seg == 0 and seq_len % n_seg == 0 and seq_len & (seq_len - 1) == 0
        scratch.append(pltpu.VMEM(((tm // n_seg) * (n_seg + 2 * HALO), wu), F32))
    else:
        for o in mix_in:
            in_specs.append(tok(o.shape[1]))
            args.append(o)
    in_specs += [_layer_spec(w) for w in (wmo, g2, w_in, w_out)]
    args += [w[0] for w in (wmo, g2, w_in, w_out)]
    final = final_g is not None
    if final:
        in_specs.append(_const_spec((1, D_MODEL)))
        args.append(final_g.reshape(1, D_MODEL))
    kmode = mode if mode == "even" else ("odd2" if len(mix_in) == 2 else "odd1")
    c_in, c_args, c_out, c_shape = _cast_specs(casts, nt)
    return pl.pallas_call(
        functools.partial(_kb_kernel, mode=kmode, ck=ck, tm=tm, seq_len=seq_len, final=final, n_cast=len(casts)),
        grid=(nt,),
        in_specs=in_specs + c_in,
        out_specs=[tok(D_MODEL)] + c_out,
        out_shape=[jax.ShapeDtypeStruct((t, D_MODEL), F32)] + c_shape,
        scratch_shapes=scratch,
        compiler_params=pltpu.CompilerParams(
            dimension_semantics=("arbitrary",), vmem_limit_bytes=VMEM_LIMIT),
        name="mix_ffn2_" + kmode,
    )(*args, *c_args)


def _diff_lambda(lam_ref, lam_init):
    lp = lam_ref[...]
    s1 = jnp.sum(lp[0:1] * lp[1:2], axis=-1, keepdims=True)
    s2 = jnp.sum(lp[2:3] * lp[3:4], axis=-1, keepdims=True)
    return jnp.exp(s1) - jnp.exp(s2) + lam_init


def _half_masks():
    lane = lax.broadcasted_iota(jnp.int32, (1, LANES), 1)
    lo = lane < HEAD_DIM
    return lo, jnp.logical_not(lo)


def _diff_out_norm(o, dn_ref, lam_init):
    return (_rms(o) * dn_ref[...]) * (1.0 - lam_init)


def _ctx_attn_kernel(nq_ref, nkt_ref, nvt_ref, dq_ref, dkt_ref, dv_ref, lam_ref, dn_ref, o_ref, *, lam_init, seq):
    lo, hi = _half_masks()
    lam = _diff_lambda(lam_ref, lam_init)
    ones_t = jnp.ones((LANES, seq), BF16)
    ones_c = jnp.ones((seq, LANES), BF16)
    for b in range(nkt_ref.shape[0]):
        rows = slice(b * seq, (b + 1) * seq)
        for hp in range(H_NAT // 2):
            sl = slice(hp * LANES, (hp + 1) * LANES)
            q = nq_ref[rows, sl]
            kt = nkt_ref[b, sl, :].astype(BF16)
            vt = jnp.concatenate([nvt_ref[b, sl, :].astype(BF16), ones_t], axis=0)
            outs = []
            for msk in (lo, hi):
                qm = jnp.where(msk, q, jnp.zeros_like(q))
                s = _dot(qm, kt)
                e = jnp.exp2(s - jnp.max(s, axis=-1, keepdims=True)).astype(BF16)
                ol = _dot_nt(e, vt)
                outs.append(ol[:, :LANES] * (1.0 / ol[:, LANES:]))
            o_ref[rows, sl] = jnp.where(lo, outs[0], outs[1]).astype(o_ref.dtype)
        for h in range(H_DIFF):
            sl = slice(h * LANES, (h + 1) * LANES)
            q = dq_ref[rows, sl]
            kt = dkt_ref[b, sl, :].astype(BF16)
            v1 = jnp.concatenate([dv_ref[rows, sl], ones_c], axis=1)
            ols = []
            for msk in (lo, hi):
                qm = jnp.where(msk, q, jnp.zeros_like(q))
                s = _dot(qm, kt)
                e = jnp.exp2(s - jnp.max(s, axis=-1, keepdims=True)).astype(BF16)
                ol = _dot(e, v1)
                ols.append(ol[:, :LANES] * (1.0 / ol[:, LANES:]))
            o = _diff_out_norm(ols[0] - lam * ols[1], dn_ref, lam_init)
            o_ref[rows, W_NAT + h * LANES:W_NAT + (h + 1) * LANES] = o.astype(o_ref.dtype)


def _ctx_attention(parts, lam_p, dnorm, seq, lam_init, nb=8):
    nq, nkt, nvt, dq, dkt, dv = parts
    t = nq.shape[0]
    steps = t // (seq * nb)
    tok = pl.BlockSpec((nb * seq, 512), lambda i: (i, 0))
    fmaj = pl.BlockSpec((nb, 512, seq), lambda i: (i, 0, 0))
    return pl.pallas_call(
        functools.partial(_ctx_attn_kernel, lam_init=lam_init, seq=seq),
        grid=(steps,),
        in_specs=[tok, fmaj, fmaj, tok, fmaj, tok, _const_spec((4, HEAD_DIM)), _const_spec((1, 2 * HEAD_DIM))],
        out_specs=pl.BlockSpec((nb * seq, D_MODEL), lambda i: (i, 0)),
        out_shape=jax.ShapeDtypeStruct((t, D_MODEL), BF16),
        compiler_params=pltpu.CompilerParams(
            dimension_semantics=("arbitrary",), vmem_limit_bytes=VMEM_LIMIT),
        name="ctx_attention",
    )(nq, nkt, nvt, dq, dkt, dv, lam_p, dnorm.reshape(1, 2 * HEAD_DIM))


NAT_QROWS = 4
NAT_KROWS = 12
NAT_NDR = 2 * NAT_WIN_R


def _build_bias_tiles(rpb_ref, u_ref):
    c = lax.broadcasted_iota(jnp.int32, (GRID_W, LANES), 0)
    lane = lax.broadcasted_iota(jnp.int32, (GRID_W, LANES), 1)
    kc = jnp.bitwise_and(lane, GRID_W - 1)
    c0 = jnp.clip(c - NAT_WIN_C // 2, 0, GRID_W - NAT_WIN_C)
    col_ok = jnp.logical_and(kc >= c0, kc < c0 + NAT_WIN_C)
    left = lane < GRID_W
    n_dr = 2 * NAT_WIN_R - 1
    for h in range(H_NAT):
        tl, tr = [], []
        for dr in range(n_dr):
            row = jnp.broadcast_to(rpb_ref[h, dr:dr + 1, :] * LOG2E, (GRID_W, LANES))
            tl.append(pltpu.roll(row, LANES - (NAT_WIN_C - 1), 1, stride=1, stride_axis=0))
            tr.append(pltpu.roll(row, LANES - (NAT_WIN_C - 1) + GRID_W, 1, stride=1, stride_axis=0))
        for di in range(NAT_NDR):
            dl = di - NAT_WIN_R + (NAT_WIN_R - 1)
            drr = dl + 1
            lv = tl[dl] if 0 <= dl < n_dr else jnp.full((GRID_W, LANES), NEG, F32)
            rv = tr[drr] if 0 <= drr < n_dr else jnp.full((GRID_W, LANES), NEG, F32)
            u_ref[h, di] = jnp.where(col_ok, jnp.where(left, lv, rv), NEG)


def _nat_lat_kernel(q_ref, k_ref, v_ref, kc_ref, vc_ref, rpb_ref, o_ref, u_ref, s_ref):
    b = pl.program_id(1)

    @pl.when(jnp.logical_and(pl.program_id(0) == 0, b == 0))
    def _():
        _build_bias_tiles(rpb_ref, u_ref)

    rows = k_ref.shape[0] // GRID_W
    kb = jnp.clip(b * NAT_QROWS - NAT_WIN_R // 2, 0, rows - NAT_KROWS)
    koff = pl.multiple_of(kb * GRID_W, GRID_W)
    lo, hi = _half_masks()
    left = lax.broadcasted_iota(jnp.int32, (GRID_W, LANES), 1) < GRID_W
    nq = NAT_QROWS * GRID_W
    for hp in range(H_NAT // 2):
        sl = slice(hp * LANES, (hp + 1) * LANES)
        q = q_ref[:, sl]
        k = k_ref[pl.ds(koff, NAT_KROWS * GRID_W), sl]
        v = v_ref[pl.ds(koff, NAT_KROWS * GRID_W), sl]
        kc = kc_ref[:, sl]
        vc = vc_ref[:, sl]
        v1 = jnp.concatenate([v, jnp.ones_like(v)], axis=1)
        vc1 = jnp.concatenate([vc, jnp.ones_like(vc)], axis=1)
        outs = []
        for e, msk in enumerate((lo, hi)):
            h = 2 * hp + e
            qm = jnp.where(msk, q, jnp.zeros_like(q))
            s_ref[...] = _dot_nt(qm, k)
            for i in range(NAT_QROWS):
                r = b * NAT_QROWS + i
                r0 = jnp.clip(r - NAT_WIN_R // 2, 0, rows - NAT_WIN_R)
                for jp in range(NAT_KROWS // 2):
                    kl = kb + 2 * jp
                    ok_l = jnp.logical_and(kl >= r0, kl < r0 + NAT_WIN_R)
                    ok_r = jnp.logical_and(kl + 1 >= r0, kl + 1 < r0 + NAT_WIN_R)
                    di = jnp.clip(kl - r + NAT_WIN_R, 0, NAT_NDR - 1)
                    ok = jnp.where(left, ok_l.astype(jnp.int32), ok_r.astype(jnp.int32))
                    bias = jnp.where(ok > 0, u_ref[h, di], NEG)
                    tile = (slice(i * GRID_W, (i + 1) * GRID_W), slice(jp * LANES, (jp + 1) * LANES))
                    s_ref[tile] = s_ref[tile] + bias
            s_ctx = _dot_nt(qm, kc)
            s_loc = s_ref[...]
            m = jnp.maximum(jnp.max(s_loc, axis=-1, keepdims=True), jnp.max(s_ctx, axis=-1, keepdims=True))
            ol = (_dot(jnp.exp2(s_loc - m).astype(BF16), v1) + _dot(jnp.exp2(s_ctx - m).astype(BF16), vc1))
            outs.append(ol[:, :LANES] * (1.0 / ol[:, LANES:]))
        o_ref[:, sl] = jnp.where(lo, outs[0], outs[1]).astype(o_ref.dtype)


def _nat_lat_attention(q, k, v, kc, vc, rpb_pad, batch, seq):
    nq = NAT_QROWS * GRID_W
    nblk = seq // nq
    p = kc.shape[1]
    return pl.pallas_call(
        _nat_lat_kernel,
        grid=(batch, nblk),
        in_specs=[
            pl.BlockSpec((nq, W_NAT), lambda bb, i: (bb * nblk + i, 0)),
            pl.BlockSpec((seq, W_NAT), lambda bb, i: (bb, 0)),
            pl.BlockSpec((seq, W_NAT), lambda bb, i: (bb, 0)),
            pl.BlockSpec((None, p, W_NAT), lambda bb, i: (bb, 0, 0)),
            pl.BlockSpec((None, p, W_NAT), lambda bb, i: (bb, 0, 0)),
            pl.BlockSpec(rpb_pad.shape, lambda bb, i: (0, 0, 0)),
        ],
        out_specs=pl.BlockSpec((nq, W_NAT), lambda bb, i: (bb * nblk + i, 0)),
        out_shape=jax.ShapeDtypeStruct((batch * seq, W_NAT), BF16),
        scratch_shapes=[
            pltpu.VMEM((H_NAT, NAT_NDR, GRID_W, LANES), F32),
            pltpu.VMEM((nq, NAT_KROWS * GRID_W), F32),
        ],
        compiler_params=pltpu.CompilerParams(
            dimension_semantics=("arbitrary", "arbitrary"), vmem_limit_bytes=VMEM_LIMIT),
        name="nat_latent_attention",
    )(q, k, v, kc, vc, rpb_pad)


DIFF_KCHUNK = 256


def _lane_fold(x, op):
    out = x[:, :LANES]
    for c in range(1, x.shape[1] // LANES):
        out = op(out, x[:, c * LANES:(c + 1) * LANES])
    return out


def _round_robin(*gens):
    gens = list(gens)
    while gens:
        for g in list(gens):
            try:
                next(g)
            except StopIteration:
                gens.remove(g)


def _diff_lat_kernel(q_ref, k_ref, v_ref, kc_ref, vc_ref, lam_ref, dn_ref, o_ref, sa_ref, sb_ref, *, lam_init):
    lo, hi = _half_masks()
    lam = _diff_lambda(lam_ref, lam_init)
    ck = DIFF_KCHUNK
    pieces = ([(k_ref, v_ref, c * ck) for c in range(k_ref.shape[0] // ck)]
              + [(kc_ref, vc_ref, c * ck) for c in range(kc_ref.shape[0] // ck)])
    row_max = {}

    def s_of(h):
        return (sa_ref, sb_ref)[h % 2]

    def scores(h):
        sl = slice(h * LANES, (h + 1) * LANES)
        q = q_ref[:, sl]
        ms = []
        for j, msk in enumerate((lo, hi)):
            qm = jnp.where(msk, q, jnp.zeros_like(q))
            mx = None
            for ci, (kr, _, off) in enumerate(pieces):
                s = _dot_nt(qm, kr[off:off + ck, sl])
                s_of(h)[j, :, ci * ck:(ci + 1) * ck] = s
                part = _lane_fold(s, jnp.maximum)
                mx = part if mx is None else jnp.maximum(mx, part)
                yield
            ms.append(jnp.max(mx, axis=-1, keepdims=True))
        row_max[h] = ms

    def weighted_values(h):
        sl = slice(h * LANES, (h + 1) * LANES)
        tq = q_ref.shape[0]
        acc = None
        for ci, (_, vr, off) in enumerate(pieces):
            cols = slice(ci * ck, (ci + 1) * ck)
            e = jnp.concatenate([jnp.exp2(s_of(h)[j, :, cols] - row_max[h][j]).astype(BF16) for j in range(2)], axis=0)
            vv = vr[off:off + ck, sl]
            t = _dot(e, jnp.concatenate([vv, jnp.ones_like(vv)], axis=1))
            acc = t if acc is None else acc + t
            yield
        on = acc[:, :LANES] * (1.0 / acc[:, LANES:])
        o = on[:tq] - lam * on[tq:]
        o_ref[:, sl] = _diff_out_norm(o, dn_ref, lam_init).astype(o_ref.dtype)

    _round_robin(scores(0))
    for h in range(H_DIFF):
        nxt = [scores(h + 1)] if h + 1 < H_DIFF else []
        _round_robin(weighted_values(h), *nxt)


def _diff_lat_attention(q, k, v, kc, vc, lam_p, dnorm, batch, seq, lam_init, tq=256):
    nqb = seq // tq
    p = kc.shape[1]
    s_scratch = pltpu.VMEM((2, tq, seq + p), F32)
    return pl.pallas_call(
        functools.partial(_diff_lat_kernel, lam_init=lam_init),
        grid=(batch, nqb),
        in_specs=[
            pl.BlockSpec((tq, W_DIFF), lambda bb, i: (bb * nqb + i, 0)),
            pl.BlockSpec((seq, W_DIFF), lambda bb, i: (bb, 0)),
            pl.BlockSpec((seq, W_DIFF), lambda bb, i: (bb, 0)),
            pl.BlockSpec((None, p, W_DIFF), lambda bb, i: (bb, 0, 0)),
            pl.BlockSpec((None, p, W_DIFF), lambda bb, i: (bb, 0, 0)),
            pl.BlockSpec((4, HEAD_DIM), lambda bb, i: (0, 0)),
            pl.BlockSpec((1, 2 * HEAD_DIM), lambda bb, i: (0, 0)),
        ],
        out_specs=pl.BlockSpec((tq, W_DIFF), lambda bb, i: (bb * nqb + i, 0)),
        out_shape=jax.ShapeDtypeStruct((batch * seq, W_DIFF), BF16),
        scratch_shapes=[s_scratch, s_scratch],
        compiler_params=pltpu.CompilerParams(
            dimension_semantics=("arbitrary", "arbitrary"), vmem_limit_bytes=VMEM_LIMIT),
        name="diff_latent_attention",
    )(q, k, v, kc, vc, lam_p, dnorm.reshape(1, 2 * HEAD_DIM))


def _lat_attn_kernel(nq_ref, nk_ref, nv_ref, nkc_ref, nvc_ref, rpb_ref, dq_ref, dk_ref, dv_ref, dkc_ref, dvc_ref,
                     lam_ref, dn_ref, on_ref, od_ref, u_ref, s_ref, sa_ref, sb_ref, *, lam_init):
    _nat_lat_kernel(nq_ref, nk_ref, nv_ref, nkc_ref, nvc_ref, rpb_ref, on_ref, u_ref, s_ref)
    _diff_lat_kernel(dq_ref, dk_ref, dv_ref, dkc_ref, dvc_ref, lam_ref, dn_ref, od_ref, sa_ref, sb_ref,
                     lam_init=lam_init)


def _all_attn_kernel(cq_ref, ckt_ref, cvt_ref, cdq_ref, cdkt_ref, cdv_ref, *refs, lam_init, cseq):
    lam_ref, dn_ref = refs[11], refs[12]
    _ctx_attn_kernel(cq_ref, ckt_ref, cvt_ref, cdq_ref, cdkt_ref, cdv_ref, lam_ref, dn_ref, refs[13],
                     lam_init=lam_init, seq=cseq)
    _lat_attn_kernel(*refs[:13], *refs[14:], lam_init=lam_init)


def _lat_attention(nat_in, rpb_pad, diff_in, lam_p, dnorm, batch, seq, lam_init, ctx_in, cseq):
    nq = NAT_QROWS * GRID_W
    nblk = seq // nq
    p = nat_in[3].shape[1]
    t_ctx = ctx_in[0].shape[0]
    assert t_ctx == batch * nblk * cseq
    qblk = pl.BlockSpec((nq, 512), lambda bb, i: (bb * nblk + i, 0))
    full = pl.BlockSpec((seq, 512), lambda bb, i: (bb, 0), pipeline_mode=pl.Buffered(1))
    ctxb = pl.BlockSpec((None, p, 512), lambda bb, i: (bb, 0, 0))
    ctok = pl.BlockSpec((cseq, 512), lambda bb, i: (bb * nblk + i, 0))
    cfm = pl.BlockSpec((1, 512, cseq), lambda bb, i: (bb * nblk + i, 0, 0))
    return pl.pallas_call(
        functools.partial(_all_attn_kernel, lam_init=lam_init, cseq=cseq),
        grid=(batch, nblk),
        in_specs=[ctok, cfm, cfm, ctok, cfm, ctok,
                  qblk, full, full, ctxb, ctxb, pl.BlockSpec(rpb_pad.shape, lambda bb, i: (0, 0, 0)),
                  qblk, full, full, ctxb, ctxb,
                  pl.BlockSpec((4, HEAD_DIM), lambda bb, i: (0, 0)),
                  pl.BlockSpec((1, 2 * HEAD_DIM), lambda bb, i: (0, 0))],
        out_specs=[pl.BlockSpec((cseq, D_MODEL), lambda bb, i: (bb * nblk + i, 0)), qblk, qblk],
        out_shape=[jax.ShapeDtypeStruct((t_ctx, D_MODEL), BF16)]
        + [jax.ShapeDtypeStruct((batch * seq, 512), BF16)] * 2,
        scratch_shapes=[
            pltpu.VMEM((H_NAT, NAT_NDR, GRID_W, LANES), F32),
            pltpu.VMEM((nq, NAT_KROWS * GRID_W), F32),
            pltpu.VMEM((2, nq, seq + p), F32),
            pltpu.VMEM((2, nq, seq + p), F32),
        ],
        compiler_params=pltpu.CompilerParams(
            dimension_semantics=("arbitrary", "arbitrary"), vmem_limit_bytes=VMEM_LIMIT),
        name="latent_attention",
    )(*ctx_in, *nat_in, rpb_pad, *diff_in, lam_p, dnorm.reshape(1, 2 * HEAD_DIM))


def _rope_tables(n):
    t = np.arange(n)
    row = (t // GRID_W).astype(np.float64)
    col = (t % GRID_W).astype(np.float64)
    quarter = HEAD_DIM // 4
    inv = 1.0 / (ROPE_THETA ** (np.arange(quarter) / quarter))
    cr, sr = np.cos(row[:, None] * inv[None]), np.sin(row[:, None] * inv[None])
    cc, sc = np.cos(col[:, None] * inv[None]), np.sin(col[:, None] * inv[None])
    zero = np.zeros_like(sr)
    cos_t = np.concatenate([cr, cr, cc, cc], axis=1)
    sin_a = np.concatenate([-sr, zero, -sc, zero], axis=1)
    sin_b = np.concatenate([zero, sr, zero, sc], axis=1)
    tile = lambda a: jnp.asarray(np.concatenate([a, a], axis=1).astype(np.float32))
    return tile(cos_t), tile(sin_a), tile(sin_b)


def kernel(x_prompt, x_sample, cache_nat_k, cache_nat_v, cache_diff_k, cache_diff_v, c, c_ctx, mod_w, mod_b, norm_ffn1, ffn1_w_in, ffn1_w_out, norm_mix, even_w_in, pool_w, pool_scale, conv_w, odd_w_in, nat_rpb, diff_lambda, diff_norm, mix_w_out, norm_ffn2, ffn2_w_in, ffn2_w_out, final_norm):
    batch, seq, _ = x_prompt.shape
    dbatch, dseq, _ = x_sample.shape
    past = cache_nat_k.shape[2]
    tm = 512

    cond = jnp.zeros((SUBLANES, D_MODEL), F32).at[0].set(c_ctx).at[1:1 + dbatch].set(c)
    mods = _adaln_mods(cond, mod_w, mod_b).reshape(DEPTH, SUBLANES, N_MOD, D_MODEL)

    ctx_row = lambda i: 0
    lat_row = lambda i: 1 + (i * tm) // dseq

    cache_bf = lambda a, w: a[:, 0].reshape(dbatch, past, w).astype(BF16)
    gain = lambda g: g.reshape(g.shape[0], 1, g.shape[1])
    g_ffn1, g_mix, g_ffn2 = gain(norm_ffn1), gain(norm_mix), gain(norm_ffn2)
    pool3 = pool_w.reshape(pool_w.shape[0], len(POOL_WINDOWS) * POOL_G, POOL_G)

    f1_in0, f1_out0, w_even0 = _cast_call([(ffn1_w_in, 0), (ffn1_w_out, 0), (even_w_in, 0)])

    ctx = x_prompt.reshape(batch * seq, D_MODEL)
    lat = x_sample.reshape(dbatch * dseq, D_MODEL)

    l = 0
    x1c, uc, f2_in0, f2_out0, wmo0, pool0 = _kernel_a(
        ctx, mods[l], ctx_row, (g_ffn1, l), (f1_in0, 0), (f1_out0, 0), (g_mix, l), (w_even0, 0), "even", tm,
        casts=[(ffn2_w_in, 0), (ffn2_w_out, 0), (mix_w_out, 0), (pool3, 0)])
    even_params = ((pool0.reshape(1, len(POOL_WINDOWS), POOL_G, POOL_G), 0), (gain(pool_scale), 0), (conv_w, 0))
    ctx, f1_in1, f1_out1 = _kernel_b(
        x1c, mods[l], ctx_row, [uc], even_params, (wmo0, 0), (g_ffn2, l), (f2_in0, 0), (f2_out0, 0), "even", tm,
        seq_len=seq, casts=[(ffn1_w_in, 1), (ffn1_w_out, 1)])
    x1l, ul, w_odd1, f2_in1 = _kernel_a(
        lat, mods[l], lat_row, (g_ffn1, l), (f1_in0, 0), (f1_out0, 0), (g_mix, l), (w_even0, 0), "even", tm,
        casts=[(odd_w_in, 0), (ffn2_w_in, 1)])
    lat, f2_out1, wmo1 = _kernel_b(
        x1l, mods[l], lat_row, [ul], even_params, (wmo0, 0), (g_ffn2, l), (f2_in0, 0), (f2_out0, 0), "even", tm,
        seq_len=dseq, casts=[(ffn2_w_out, 1), (mix_w_out, 1)])

    l = 1
    lam_init = 0.8 - 0.6 * math.exp(-0.3 * l)
    ffn1 = ((g_ffn1, l), (f1_in1, 0), (f1_out1, 0), (g_mix, l), (w_odd1, 0))
    ffn2 = ((wmo1, 0), (g_ffn2, l), (f2_in1, 0), (f2_out1, 0))
    ctx1, nq, nkt, nvt, dq, dkt, dv, dvb = _kernel_a(ctx, mods[l], ctx_row, *ffn1, "ctx", tm, seq=seq)
    lat1, lq, lk, lv, ldq, ldk, ldv = _kernel_a(lat, mods[l], lat_row, *ffn1, "lat", tm,
                                                 rope_tabs=_rope_tables(dseq))
    rpb_pad = jnp.zeros((H_NAT, 2 * NAT_WIN_R, LANES), F32).at[:, :2 * NAT_WIN_R - 1,
                                                                :2 * NAT_WIN_C - 1].set(nat_rpb[0])
    o_ctx, o_nat, o_diff = _lat_attention(
        [lq, lk, lv, cache_bf(cache_nat_k, W_NAT), cache_bf(cache_nat_v, W_NAT)], rpb_pad,
        [ldq, ldk, ldv, cache_bf(cache_diff_k, W_DIFF), cache_bf(cache_diff_v, W_DIFF)],
        diff_lambda[0], diff_norm[0], dbatch, dseq, lam_init, [nq, nkt, nvt, dq, dkt, dvb], seq)
    y_prompt, = _kernel_b(ctx1, mods[l], ctx_row, [o_ctx], None, *ffn2, "odd", tm, final_g=final_norm)
    y_sample, = _kernel_b(lat1, mods[l], lat_row, [o_nat, o_diff], None, *ffn2, "odd", tm, final_g=final_norm)

    new_nat_k = nkt.reshape(batch, 1, H_NAT, HEAD_DIM, seq).transpose(0, 1, 4, 2, 3)
    new_nat_v = nvt.reshape(batch, 1, H_NAT, HEAD_DIM, seq).transpose(0, 1, 4, 2, 3)
    new_diff_k = dkt.reshape(batch, 1, H_DIFF, 2, HEAD_DIM, seq).transpose(0, 1, 5, 2, 3, 4)
    new_diff_v = dv.reshape(batch, 1, seq, H_DIFF, 2 * HEAD_DIM)
    return (y_prompt.reshape(batch, seq, D_MODEL), y_sample.reshape(dbatch, dseq, D_MODEL),
            new_nat_k, new_nat_v, new_diff_k, new_diff_v)
```
